```python
import math
import jax, jax.numpy as jnp
from jax import lax
import numpy as np

D_MODEL = 2048
BATCH = 8
SEQ = 8192
DEPTH = 4

N_MEM = 256
MIX_WIDTH = D_MODEL
ATTN_WIDTH = MIX_WIDTH // 2
HG_WIDTH = MIX_WIDTH - ATTN_WIDTH
ATTN_HEAD_DIM = 128
ATTN_HEADS = ATTN_WIDTH // ATTN_HEAD_DIM
HG_EXPAND = 128
HG_HEADS = HG_WIDTH // HG_EXPAND
HG_VDIM = HG_WIDTH // HG_HEADS
HG_CHUNK = 64
DILATED_BRANCHES = ((128, 1), (512, 4), (2048, 16))
Q_BLOCK = 128
REL_BUCKETS = 32
REL_MAX_DIST = 2048
CROSS_HEADS = 4
CROSS_HEAD_DIM = D_MODEL // CROSS_HEADS
D_FF = ((8 * D_MODEL + 3 * 256 - 1) // (3 * 256)) * 256
IN_SIZES = (ATTN_WIDTH, ATTN_WIDTH, ATTN_WIDTH, HG_WIDTH, HG_WIDTH, HG_WIDTH, HG_WIDTH)
IN_COLS = sum(IN_SIZES)
N_NORMS = 7
RMS_EPS = 1e-6
NEG_INF = -1e30

kernel_name = "hybrid_dilated_attn_hgrn2_trunk"


def rmsnorm(x, g):
    xf = x.astype(jnp.float32)
    y = xf * lax.rsqrt(jnp.mean(xf * xf, axis=-1, keepdims=True) + RMS_EPS)
    return (y * g.astype(jnp.float32)).astype(x.dtype)


def rel_bucket(dist):
    max_exact = REL_BUCKETS // 2
    d_f = jnp.maximum(dist, 1).astype(jnp.float32)
    large = max_exact + (jnp.log(d_f / max_exact) / math.log(REL_MAX_DIST / max_exact)
                         * (REL_BUCKETS - max_exact)).astype(jnp.int32)
    large = jnp.minimum(large, REL_BUCKETS - 1)
    return jnp.where(dist < max_exact, dist, large)


def dilated_branch(q, k, v, rel_bias, window, dilation):
    B, S, H, Dh = q.shape
    band = window // dilation
    assert band <= Q_BLOCK
    M = S // dilation
    Mp = -(-M // Q_BLOCK) * Q_BLOCK
    nb = Mp // Q_BLOCK

    def split(t):
        t = t.reshape(B, M, dilation, H, Dh)
        return jnp.pad(t, ((0, 0), (0, Mp - M), (0, 0), (0, 0), (0, 0)))

    def kwin(t):
        tp = jnp.pad(split(t), ((0, 0), (Q_BLOCK, 0), (0, 0), (0, 0), (0, 0)))
        tp = tp.reshape(B, nb + 1, Q_BLOCK, dilation, H, Dh)
        return jnp.concatenate([tp[:, :-1], tp[:, 1:]], axis=2)

    qs = split(q).reshape(B, nb, Q_BLOCK, dilation, H, Dh)
    ks, vs = kwin(k), kwin(v)
    scores = jnp.einsum('bnqrhd,bnkrhd->bnrhqk', qs, ks).astype(jnp.float32) * (Dh ** -0.5)

    qi = jnp.arange(Q_BLOCK)[:, None]
    kj = jnp.arange(2 * Q_BLOCK)[None, :]
    m = qi - kj + Q_BLOCK
    bucket = rel_bucket(jnp.maximum(m, 0) * dilation)
    bias = jnp.transpose(rel_bias[bucket].astype(jnp.float32), (2, 0, 1))
    band_ok = (m >= 0) & (m <= band)
    blk = jnp.arange(nb)[:, None, None]
    start_ok = (blk * Q_BLOCK + kj[None] - Q_BLOCK) >= 0
    valid = band_ok[None] & start_ok
    logits = jnp.where(valid[None, :, None, None], scores + bias, NEG_INF)
    lse = jax.nn.logsumexp(logits, axis=-1)
    p = jnp.exp(logits - lse[..., None])
    o = jnp.einsum('bnrhqk,bnkrhd->bnqrhd', p.astype(vs.dtype), vs)
    o = o.reshape(B, Mp, dilation, H, Dh)[:, :M].reshape(B, S, H, Dh)
    lse = jnp.transpose(lse, (0, 1, 4, 2, 3)).reshape(B, Mp, dilation, H)[:, :M].reshape(B, S, H)
    return o, lse


def dilated_attention(q, k, v, rel_bias):
    outs, lses = [], []
    for window, dilation in DILATED_BRANCHES:
        o, l = dilated_branch(q, k, v, rel_bias, window, dilation)
        outs.append(o)
        lses.append(l)
    w = jax.nn.softmax(jnp.stack(lses), axis=0)
    o = jnp.einsum('gbsh,gbshd->bshd', w.astype(q.dtype), jnp.stack(outs))
    return o


def hgrn2(fz, iv, qz, gz, lb, hg_gain):
    B, S, _ = fz.shape
    nc = S // HG_CHUNK
    f = lb + (1.0 - lb) * jax.nn.sigmoid(fz.astype(jnp.float32))
    log_f = jnp.log(f)
    kk = 1.0 - f
    qq = jax.nn.silu(qz.astype(jnp.float32))
    vv = iv.astype(jnp.float32)

    def chunks(t, dim):
        return jnp.transpose(t.reshape(B, nc, HG_CHUNK, HG_HEADS, dim), (1, 0, 3, 2, 4))

    xs = (chunks(qq, HG_EXPAND), chunks(kk, HG_EXPAND), chunks(vv, HG_VDIM), chunks(log_f, HG_EXPAND))
    tri = jnp.tril(jnp.ones((HG_CHUNK, HG_CHUNK), dtype=bool))

    def step(state, inp):
        qc, kc, vc, lfc = inp
        b = jnp.cumsum(lfc, axis=-2)
        o_inter = jnp.einsum('bhck,bhkv->bhcv', qc * jnp.exp(b), state)
        diff = b[:, :, :, None, :] - b[:, :, None, :, :]
        decay = jnp.exp(jnp.where(tri[:, :, None], diff, -jnp.inf))
        A = jnp.einsum('bhtk,bhtsk,bhsk->bhts', qc, decay, kc)
        o = o_inter + jnp.einsum('bhts,bhsv->bhtv', A, vc)
        b_last = b[:, :, -1:, :]
        new_state = jnp.exp(b_last[:, :, 0, :])[..., None] * state + \
            jnp.einsum('bhsk,bhsv->bhkv', kc * jnp.exp(b_last - b), vc)
        return new_state, o

    s0 = jnp.zeros((B, HG_HEADS, HG_EXPAND, HG_VDIM), jnp.float32)
    _, o = lax.scan(step, s0, xs)
    o = jnp.transpose(o, (1, 0, 3, 2, 4)).reshape(B, S, HG_HEADS, HG_VDIM)
    o = o * lax.rsqrt(jnp.mean(o * o, axis=-1, keepdims=True) + RMS_EPS)
    o = o * hg_gain.astype(jnp.float32).reshape(HG_HEADS, HG_VDIM)
    o = o.reshape(B, S, HG_WIDTH) * jax.nn.silu(gz.astype(jnp.float32))
    return o.astype(fz.dtype)


def _fwd_setup_inputs(seed: int = 0) -> dict:
    key = jax.random.key(seed)
    ks = jax.random.split(key, 14)
    f32 = jnp.float32
    nrm = lambda k, shape, scale: jax.random.normal(k, shape, f32) * scale
    return {
        "x": nrm(ks[0], (BATCH, SEQ, D_MODEL), 1.0),
        "mem": nrm(ks[1], (BATCH, N_MEM, D_MODEL), 1.0),
        "rel_bias": nrm(ks[2], (REL_BUCKETS, ATTN_HEADS), 0.5),
        "lb_logits": nrm(ks[3], (DEPTH, HG_WIDTH), 0.5),
        "norm_gains": 1.0 + nrm(ks[4], (DEPTH, N_NORMS, D_MODEL), 0.1),
        "w_in": nrm(ks[5], (DEPTH, D_MODEL, IN_COLS), D_MODEL ** -0.5),
        "hg_norm": 1.0 + nrm(ks[6], (DEPTH, HG_WIDTH), 0.1),
        "w_out": nrm(ks[7], (DEPTH, MIX_WIDTH, D_MODEL), MIX_WIDTH ** -0.5),
        "w_cq": nrm(ks[8], (DEPTH, D_MODEL, D_MODEL), D_MODEL ** -0.5),
        "w_ckv": nrm(ks[9], (DEPTH, D_MODEL, 2 * D_MODEL), D_MODEL ** -0.5),
        "w_co": nrm(ks[10], (DEPTH, D_MODEL, D_MODEL), D_MODEL ** -0.5),
        "w_gate_up": nrm(ks[11], (DEPTH, D_MODEL, 2 * D_FF), D_MODEL ** -0.5),
        "w_down": nrm(ks[12], (DEPTH, D_FF, D_MODEL), D_FF ** -0.5),
    }


def _fwd_reference(x, mem, rel_bias, lb_logits, norm_gains, w_in, hg_norm, w_out,
              w_cq, w_ckv, w_co, w_gate_up, w_down):
    B, S, D = x.shape
    n_mem = mem.shape[1]
    P = jax.nn.softmax(lb_logits.astype(jnp.float32), axis=0)
    lb_all = jnp.cumsum(P, axis=0) - P
    split_idx = [int(i) for i in np.cumsum(IN_SIZES)[:-1]]

    for l in range(DEPTH):
        g = norm_gains[l]
        h = rmsnorm(x, g[0])
        proj = h @ w_in[l]
        aq, ak, av, fz, iv, qz, gz = jnp.split(proj, split_idx, axis=-1)
        shp = (B, S, ATTN_HEADS, ATTN_HEAD_DIM)
        attn = dilated_attention(aq.reshape(shp), ak.reshape(shp), av.reshape(shp), rel_bias)
        attn = attn.reshape(B, S, ATTN_WIDTH)
        hg = hgrn2(fz, iv, qz, gz, lb_all[l], hg_norm[l])
        mix = jnp.concatenate([attn, hg], axis=-1) @ w_out[l]
        x = x + rmsnorm(mix, g[1])
        hc = rmsnorm(x, g[2])
        mn = rmsnorm(mem, g[3])
        cq = (hc @ w_cq[l]).reshape(B, S, CROSS_HEADS, CROSS_HEAD_DIM)
        ck, cv = jnp.split(mn @ w_ckv[l], 2, axis=-1)
        ck = ck.reshape(B, n_mem, CROSS_HEADS, CROSS_HEAD_DIM)
        cv = cv.reshape(B, n_mem, CROSS_HEADS, CROSS_HEAD_DIM)
        s = jnp.einsum('bshd,bmhd->bhsm', cq, ck).astype(jnp.float32) * (CROSS_HEAD_DIM ** -0.5)
        p = jax.nn.softmax(s, axis=-1).astype(cv.dtype)
        co = jnp.einsum('bhsm,bmhd->bshd', p, cv).reshape(B, S, D) @ w_co[l]
        x = x + rmsnorm(co, g[4])
        hf = rmsnorm(x, g[5])
        gate, up = jnp.split(hf @ w_gate_up[l], 2, axis=-1)
        y = (jax.nn.silu(gate) * up) @ w_down[l]
        x = x + rmsnorm(y, g[6])
    return x


import jax as _jax
import jax.numpy as _jnp

TWIN_FORMAT = 'train_step'
FWD_PARAMS = ['x', 'mem', 'rel_bias', 'lb_logits', 'norm_gains', 'w_in', 'hg_norm', 'w_out', 'w_cq', 'w_ckv', 'w_co', 'w_gate_up', 'w_down']
TWIN_WEIGHTS = ['rel_bias', 'lb_logits', 'norm_gains', 'w_in', 'hg_norm', 'w_out', 'w_cq', 'w_ckv', 'w_co', 'w_gate_up', 'w_down']
TWIN_DIFF_INPUT = 'x'
TWIN_INPUTS = ['x', 'mem', 'rel_bias', 'lb_logits', 'norm_gains', 'w_in', 'hg_norm', 'w_out', 'w_cq', 'w_ckv', 'w_co', 'w_gate_up', 'w_down', 'loss_target', 'm_rel_bias', 'm_lb_logits', 'm_norm_gains', 'm_w_in', 'm_hg_norm', 'm_w_out', 'm_w_cq', 'm_w_ckv', 'm_w_co', 'm_w_gate_up', 'm_w_down', 'v_rel_bias', 'v_lb_logits', 'v_norm_gains', 'v_w_in', 'v_hg_norm', 'v_w_out', 'v_w_cq', 'v_w_ckv', 'v_w_co', 'v_w_gate_up', 'v_w_down']
TWIN_OUTPUTS = ['loss', 'grad_x', 'grad_rel_bias', 'grad_lb_logits', 'grad_norm_gains', 'grad_w_in', 'grad_hg_norm', 'grad_w_out', 'grad_w_cq', 'grad_w_ckv', 'grad_w_co', 'grad_w_gate_up', 'grad_w_down', 'delta_rel_bias', 'delta_lb_logits', 'delta_norm_gains', 'delta_w_in', 'delta_hg_norm', 'delta_w_out', 'delta_w_cq', 'delta_w_ckv', 'delta_w_co', 'delta_w_gate_up', 'delta_w_down', 'new_m_rel_bias', 'new_m_lb_logits', 'new_m_norm_gains', 'new_m_w_in', 'new_m_hg_norm', 'new_m_w_out', 'new_m_w_cq', 'new_m_w_ckv', 'new_m_w_co', 'new_m_w_gate_up', 'new_m_w_down', 'new_v_rel_bias', 'new_v_lb_logits', 'new_v_norm_gains', 'new_v_w_in', 'new_v_hg_norm', 'new_v_w_out', 'new_v_w_cq', 'new_v_w_ckv', 'new_v_w_co', 'new_v_w_gate_up', 'new_v_w_down']
TWIN_LEAF_KINDS = {'loss': 'loss', 'grad_x': 'grad_x', 'grad_rel_bias': 'grad_w', 'grad_lb_logits': 'grad_w', 'grad_norm_gains': 'grad_w', 'grad_w_in': 'grad_w', 'grad_hg_norm': 'grad_w', 'grad_w_out': 'grad_w', 'grad_w_cq': 'grad_w', 'grad_w_ckv': 'grad_w', 'grad_w_co': 'grad_w', 'grad_w_gate_up': 'grad_w', 'grad_w_down': 'grad_w', 'delta_rel_bias': 'delta_w', 'delta_lb_logits': 'delta_w', 'delta_norm_gains': 'delta_w', 'delta_w_in': 'delta_w', 'delta_hg_norm': 'delta_w', 'delta_w_out': 'delta_w', 'delta_w_cq': 'delta_w', 'delta_w_ckv': 'delta_w', 'delta_w_co': 'delta_w', 'delta_w_gate_up': 'delta_w', 'delta_w_down': 'delta_w', 'new_m_rel_bias': 'new_m', 'new_m_lb_logits': 'new_m', 'new_m_norm_gains': 'new_m', 'new_m_w_in': 'new_m', 'new_m_hg_norm': 'new_m', 'new_m_w_out': 'new_m', 'new_m_w_cq': 'new_m', 'new_m_w_ckv': 'new_m', 'new_m_w_co': 'new_m', 'new_m_w_gate_up': 'new_m', 'new_m_w_down': 'new_m', 'new_v_rel_bias': 'new_v', 'new_v_lb_logits': 'new_v', 'new_v_norm_gains': 'new_v', 'new_v_w_in': 'new_v', 'new_v_hg_norm': 'new_v', 'new_v_w_out': 'new_v', 'new_v_w_cq': 'new_v', 'new_v_w_ckv': 'new_v', 'new_v_w_co': 'new_v', 'new_v_w_gate_up': 'new_v', 'new_v_w_down': 'new_v'}


def _forward(args):
    return _fwd_reference(*[args[k] for k in FWD_PARAMS])


def _output_shape():
    def fwd():
        inp = _fwd_setup_inputs(0)
        return _fwd_reference(*[inp[k] for k in FWD_PARAMS])
    out = _jax.eval_shape(fwd)
    return out.shape, out.dtype

N_MICROBATCH = 1
ADAM_LR = 0.001
ADAM_B1 = 0.9
ADAM_B2 = 0.999
ADAM_EPS = 1e-08
ADAM_WD = 0.01
ADAM_STEP = 10
PER_EXAMPLE_BATCH_AXIS = {'x': 0, 'mem': 0, 'loss_target': 0}
SHARED_INPUTS = []
_WEIGHT_DTYPES = {'rel_bias': _jnp.float32, 'lb_logits': _jnp.float32, 'norm_gains': _jnp.float32, 'w_in': _jnp.float32, 'hg_norm': _jnp.float32, 'w_out': _jnp.float32, 'w_cq': _jnp.float32, 'w_ckv': _jnp.float32, 'w_co': _jnp.float32, 'w_gate_up': _jnp.float32, 'w_down': _jnp.float32}
MOMENT_SCALE = {'rel_bias': 1.855263e+00, 'lb_logits': 8.735659e-02, 'norm_gains': 2.387070e+01, 'w_in': 5.614190e+00, 'hg_norm': 6.357555e+00, 'w_out': 1.054400e+01, 'w_cq': 4.353193e+00, 'w_ckv': 1.220018e+01, 'w_co': 1.709858e+01, 'w_gate_up': 2.412640e+00, 'w_down': 4.718071e+00}


def _to_microbatches(a, axis):
    t = _jnp.moveaxis(a, axis, 0)
    t = t.reshape((N_MICROBATCH, t.shape[0] // N_MICROBATCH) + t.shape[1:])
    return _jnp.moveaxis(t, 1, axis + 1)


def setup_inputs(seed: int = 0) -> dict:
    inp = _fwd_setup_inputs(seed)
    key = _jax.random.fold_in(_jax.random.key(seed), 7919)
    shape, _ = _output_shape()
    out = dict(inp)
    out["loss_target"] = _jax.random.normal(_jax.random.fold_in(key, 0), shape, _jnp.float32)
    for i, name in enumerate(TWIN_WEIGHTS):
        w = inp[name].astype(_jnp.float32)
        if MOMENT_SCALE is None:
            s = _jnp.sqrt(_jnp.mean(_jnp.square(w)) + 1e-30)
        else:
            s = MOMENT_SCALE[name]
        km, kv = _jax.random.split(_jax.random.fold_in(key, i + 1))
        out[name] = w
        out["m_" + name] = s * _jax.random.normal(km, w.shape, _jnp.float32)
        out["v_" + name] = (s * s) * _jax.random.uniform(kv, w.shape, _jnp.float32, 0.5, 1.5)
    if N_MICROBATCH > 1:
        for name, axis in PER_EXAMPLE_BATCH_AXIS.items():
            out[name] = _to_microbatches(out[name], axis)
    return {'x': out['x'], 'mem': out['mem'], 'rel_bias': out['rel_bias'], 'lb_logits': out['lb_logits'], 'norm_gains': out['norm_gains'], 'w_in': out['w_in'], 'hg_norm': out['hg_norm'], 'w_out': out['w_out'], 'w_cq': out['w_cq'], 'w_ckv': out['w_ckv'], 'w_co': out['w_co'], 'w_gate_up': out['w_gate_up'], 'w_down': out['w_down'], 'loss_target': out['loss_target'], 'm_rel_bias': out['m_rel_bias'], 'm_lb_logits': out['m_lb_logits'], 'm_norm_gains': out['m_norm_gains'], 'm_w_in': out['m_w_in'], 'm_hg_norm': out['m_hg_norm'], 'm_w_out': out['m_w_out'], 'm_w_cq': out['m_w_cq'], 'm_w_ckv': out['m_w_ckv'], 'm_w_co': out['m_w_co'], 'm_w_gate_up': out['m_w_gate_up'], 'm_w_down': out['m_w_down'], 'v_rel_bias': out['v_rel_bias'], 'v_lb_logits': out['v_lb_logits'], 'v_norm_gains': out['v_norm_gains'], 'v_w_in': out['v_w_in'], 'v_hg_norm': out['v_hg_norm'], 'v_w_out': out['v_w_out'], 'v_w_cq': out['v_w_cq'], 'v_w_ckv': out['v_w_ckv'], 'v_w_co': out['v_w_co'], 'v_w_gate_up': out['v_w_gate_up'], 'v_w_down': out['v_w_down']}


def _loss(weights, diff, rest, loss_target):
    with _jax.named_scope("forward"):
        args = {**rest, TWIN_DIFF_INPUT: diff, **{k: w.astype(_WEIGHT_DTYPES[k]) for k, w in weights.items()}}
        y = _forward(args)
    with _jax.named_scope("loss_head"):
        err = _jnp.square(y.astype(_jnp.float32) - loss_target)
        return 0.5 * _jnp.sum(_jnp.mean(err, axis=-1)) if err.ndim else 0.5 * err


def _adamw(w, g, m, v):
    m = ADAM_B1 * m + (1.0 - ADAM_B1) * g
    v = ADAM_B2 * v + (1.0 - ADAM_B2) * _jnp.square(g)
    m_hat = m / (1.0 - ADAM_B1 ** ADAM_STEP)
    v_hat = v / (1.0 - ADAM_B2 ** ADAM_STEP)
    delta = -ADAM_LR * (m_hat / (_jnp.sqrt(v_hat) + ADAM_EPS) + ADAM_WD * w)
    return delta, m, v


def reference(x, mem, rel_bias, lb_logits, norm_gains, w_in, hg_norm, w_out, w_cq, w_ckv, w_co, w_gate_up, w_down, loss_target, m_rel_bias, m_lb_logits, m_norm_gains, m_w_in, m_hg_norm, m_w_out, m_w_cq, m_w_ckv, m_w_co, m_w_gate_up, m_w_down, v_rel_bias, v_lb_logits, v_norm_gains, v_w_in, v_hg_norm, v_w_out, v_w_cq, v_w_ckv, v_w_co, v_w_gate_up, v_w_down):
    given = dict(x=x, mem=mem, rel_bias=rel_bias, lb_logits=lb_logits, norm_gains=norm_gains, w_in=w_in, hg_norm=hg_norm, w_out=w_out, w_cq=w_cq, w_ckv=w_ckv, w_co=w_co, w_gate_up=w_gate_up, w_down=w_down, loss_target=loss_target, m_rel_bias=m_rel_bias, m_lb_logits=m_lb_logits, m_norm_gains=m_norm_gains, m_w_in=m_w_in, m_hg_norm=m_hg_norm, m_w_out=m_w_out, m_w_cq=m_w_cq, m_w_ckv=m_w_ckv, m_w_co=m_w_co, m_w_gate_up=m_w_gate_up, m_w_down=m_w_down, v_rel_bias=v_rel_bias, v_lb_logits=v_lb_logits, v_norm_gains=v_norm_gains, v_w_in=v_w_in, v_hg_norm=v_hg_norm, v_w_out=v_w_out, v_w_cq=v_w_cq, v_w_ckv=v_w_ckv, v_w_co=v_w_co, v_w_gate_up=v_w_gate_up, v_w_down=v_w_down)
    weights = {n: given[n] for n in TWIN_WEIGHTS}
    shared = {n: given[n] for n in SHARED_INPUTS}
    per_example = {n: given[n] for n in ['x', 'mem']}
    grad_fn = _jax.value_and_grad(_loss, argnums=(0, 1))

    def one_microbatch(ex, loss_target):
        ex = dict(ex)
        diff = ex.pop(TWIN_DIFF_INPUT)
        return grad_fn(weights, diff, {**shared, **ex}, loss_target)

    if N_MICROBATCH == 1:
        loss, (grad_w, grad_x) = one_microbatch(per_example, given["loss_target"])
    else:
        def body(carry, xs):
            loss_sum, grad_sum = carry
            l_k, (gw_k, gx_k) = one_microbatch(xs[0], xs[1])
            with _jax.named_scope("update"):
                return (loss_sum + l_k, _jax.tree.map(_jnp.add, grad_sum, gw_k)), gx_k

        init = (_jnp.zeros((), _jnp.float32), _jax.tree.map(_jnp.zeros_like, weights))
        (loss, grad_w), grad_x = _jax.lax.scan(body, init, (per_example, given["loss_target"]))
    with _jax.named_scope("update"):
        delta_w, new_m, new_v = {}, {}, {}
        for n in TWIN_WEIGHTS:
            delta_w[n], new_m[n], new_v[n] = _adamw(weights[n], grad_w[n], given["m_" + n], given["v_" + n])
    return (loss, grad_x, *[grad_w[n] for n in TWIN_WEIGHTS], *[delta_w[n] for n in TWIN_WEIGHTS],
            *[new_m[n] for n in TWIN_WEIGHTS], *[new_v[n] for n in TWIN_WEIGHTS])
```

```python
import functools
import math

import numpy as np
import jax
import jax.numpy as jnp
from jax import lax
from jax.experimental import pallas as pl
from jax.experimental.pallas import tpu as pltpu

F32 = jnp.float32
BF16 = jnp.bfloat16
MESH_ID = pl.DeviceIdType.MESH

D_MODEL = 2048
HEAD_DIM = 128
N_HEADS = 8
ATTN_W = 1024
HG_W = 1024
HG_CHUNK = 64
Q_BLOCK = 128
DILATIONS = (16, 4, 1)
REL_BUCKETS = 32
REL_MAX_DIST = 2048
CROSS_HEADS = 4
CROSS_DIM = 512
D_FF = 5632
RMS_EPS = 1e-6
NEG_INF = -1e30
N_NORMS = 7

ADAM_LR = 0.001
ADAM_B1 = 0.9
ADAM_B2 = 0.999
ADAM_EPS = 1e-08
ADAM_WD = 0.01
ADAM_STEP = 10

VMEM_LIMIT_V7X = 56 * 1024 * 1024
LANES = 128
N_CHIPS = 4


def _cparams(sem=None):
    if sem is None:
        return pltpu.CompilerParams(vmem_limit_bytes=VMEM_LIMIT_V7X)
    return pltpu.CompilerParams(dimension_semantics=sem, vmem_limit_bytes=VMEM_LIMIT_V7X)


def _blk(n, prefs):
    for p in prefs:
        if p <= n and n % p == 0:
            return p
    return n


def _dot(a, b, dims):
    return lax.dot_general(a.astype(BF16), b.astype(BF16), (dims, ((), ())), preferred_element_type=F32)


def _dot_nn(a, b):
    return _dot(a, b, ((1,), (0,)))


def _dot_nt(a, b):
    return _dot(a, b, ((1,), (1,)))


def _dot_tn(a, b):
    return _dot(a, b, ((0,), (0,)))


def _sigmoid(x):
    return 1.0 / (1.0 + jnp.exp(-x))


def _mm(a_list, b_list, mode, out_dtype, name):
    if not isinstance(a_list, (list, tuple)):
        a_list = [a_list]
    if not isinstance(b_list, (list, tuple)):
        b_list = [b_list]
    na, nb = len(a_list), len(b_list)
    big = (1024, 512, 256, 128)
    if mode == "nn":
        M, K = a_list[0].shape
        N = b_list[0].shape[1]
    elif mode == "nt":
        M = a_list[0].shape[0]
        K = sum(a.shape[1] for a in a_list)
        N = b_list[0].shape[0]
    else:
        K, M = a_list[0].shape
        N = sum(b.shape[1] for b in b_list)
    bm = _blk(M, big)
    if mode == "nt":
        bk = _blk(math.gcd(*[a.shape[1] for a in a_list]), (1024, 1408, 512, 256, 128))
    else:
        bk = _blk(K, (1024, 1408, 512, 256, 128))
    if mode == "tn":
        bn = _blk(math.gcd(*[b.shape[1] for b in b_list]), big)
    else:
        bn = _blk(N, big)
    nk = K // bk
    grid = (M // bm, N // bn, nk)

    if mode == "nn":
        a_specs = [pl.BlockSpec((bm, bk), lambda i, j, k: (i, k))]
        b_specs = [pl.BlockSpec((bk, bn), lambda i, j, k: (k, j))]
        dims = ((1,), (0,))
    elif mode == "nt":
        a_specs, off = [], 0
        for a in a_list:
            n_i = a.shape[1] // bk
            a_specs.append(pl.BlockSpec((bm, bk), functools.partial(
                lambda i, j, k, off, n_i: (i, jnp.clip(k - off, 0, n_i - 1)), off=off, n_i=n_i)))
            off += n_i
        b_specs = [pl.BlockSpec((bn, bk), lambda i, j, k: (j, k))]
        dims = ((1,), (1,))
    else:
        a_specs = [pl.BlockSpec((bk, bm), lambda i, j, k: (k, i))]
        b_specs, off = [], 0
        for b in b_list:
            n_j = b.shape[1] // bn
            b_specs.append(pl.BlockSpec((bk, bn), functools.partial(
                lambda i, j, k, off, n_j: (jnp.where((j >= off) & (j < off + n_j), k, 0),
                                           jnp.clip(j - off, 0, n_j - 1)), off=off, n_j=n_j)))
            off += n_j
        dims = ((0,), (0,))
    a_bounds = np.cumsum([0] + [a.shape[1] // bk for a in a_list]) if mode == "nt" else None
    b_bounds = np.cumsum([0] + [b.shape[1] // bn for b in b_list]) if mode == "tn" else None

    def body(*refs):
        a_refs = refs[:na]
        b_refs = refs[na:na + nb]
        o_ref = refs[na + nb]
        acc_ref = refs[na + nb + 1] if nk > 1 else None
        j = pl.program_id(1)
        k = pl.program_id(2)

        def accumulate(p):
            if nk == 1:
                o_ref[...] = p.astype(out_dtype)
                return

            @pl.when(k == 0)
            def _():
                acc_ref[...] = p

            @pl.when(k > 0)
            def _():
                acc_ref[...] += p

        if na > 1:
            for t in range(na):
                @pl.when((k >= int(a_bounds[t])) & (k < int(a_bounds[t + 1])))
                def _(t=t):
                    accumulate(_dot(a_refs[t][...], b_refs[0][...], dims))
        elif nb > 1:
            for t in range(nb):
                @pl.when((j >= int(b_bounds[t])) & (j < int(b_bounds[t + 1])))
                def _(t=t):
                    accumulate(_dot(a_refs[0][...], b_refs[t][...], dims))
        else:
            accumulate(_dot(a_refs[0][...], b_refs[0][...], dims))

        if nk > 1:
            @pl.when(k == nk - 1)
            def _():
                o_ref[...] = acc_ref[...].astype(out_dtype)

    return pl.pallas_call(
        body, name=name, grid=grid,
        in_specs=a_specs + b_specs,
        out_specs=pl.BlockSpec((bm, bn), lambda i, j, k: (i, j)),
        out_shape=jax.ShapeDtypeStruct((M, N), out_dtype),
        scratch_shapes=[pltpu.VMEM((bm, bn), F32)] if nk > 1 else [],
        compiler_params=_cparams(("parallel", "parallel", "arbitrary")),
    )(*a_list, *b_list)


def _rms_fwd(x, g, name):
    S, D = x.shape
    tm = _blk(S, (256, 128, 64, 8))

    def body(x_ref, g_ref, o_ref):
        xv = x_ref[...]
        r = lax.rsqrt(jnp.mean(xv * xv, axis=-1, keepdims=True) + RMS_EPS)
        o_ref[...] = ((xv * r) * g_ref[...]).astype(o_ref.dtype)

    return pl.pallas_call(
        body, name=name, grid=(S // tm,),
        in_specs=[pl.BlockSpec((tm, D), lambda i: (i, 0)), pl.BlockSpec((1, D), lambda i: (0, 0))],
        out_specs=pl.BlockSpec((tm, D), lambda i: (i, 0)),
        out_shape=jax.ShapeDtypeStruct((S, D), BF16),
        compiler_params=_cparams(("parallel",)),
    )(x, g)


def _norm_residual(x, t, g, name):
    S, D = x.shape
    tm = _blk(S, (256, 128, 64, 8))

    def body(x_ref, t_ref, g_ref, o_ref):
        tv = t_ref[...]
        r = lax.rsqrt(jnp.mean(tv * tv, axis=-1, keepdims=True) + RMS_EPS)
        o_ref[...] = x_ref[...] + (tv * r) * g_ref[...]

    return pl.pallas_call(
        body, name=name, grid=(S // tm,),
        in_specs=[pl.BlockSpec((tm, D), lambda i: (i, 0)), pl.BlockSpec((tm, D), lambda i: (i, 0)),
                  pl.BlockSpec((1, D), lambda i: (0, 0))],
        out_specs=pl.BlockSpec((tm, D), lambda i: (i, 0)),
        out_shape=jax.ShapeDtypeStruct((S, D), F32),
        compiler_params=_cparams(("parallel",)),
    )(x, t, g)


def _rms_bwd(x, g, dh, res, out_dtype, name):
    S, D = x.shape
    tm = _blk(S, (256, 128, 64, 8))
    has_res = res is not None

    def body(*refs):
        if has_res:
            x_ref, g_ref, dh_ref, res_ref, dx_ref, dg_ref = refs
        else:
            x_ref, g_ref, dh_ref, dx_ref, dg_ref = refs
        xv = x_ref[...]
        r = lax.rsqrt(jnp.mean(xv * xv, axis=-1, keepdims=True) + RMS_EPS)
        xh = xv * r
        dhv = dh_ref[...].astype(F32)
        gd = dhv * g_ref[...]
        dx = r * (gd - xh * jnp.mean(gd * xh, axis=-1, keepdims=True))
        if has_res:
            dx = dx + res_ref[...]
        dx_ref[...] = dx.astype(out_dtype)
        part = jnp.sum(dhv * xh, axis=0, keepdims=True)

        @pl.when(pl.program_id(0) == 0)
        def _():
            dg_ref[...] = part

        @pl.when(pl.program_id(0) > 0)
        def _():
            dg_ref[...] += part

    row = pl.BlockSpec((tm, D), lambda i: (i, 0))
    vec = pl.BlockSpec((1, D), lambda i: (0, 0))
    ins = [x, g, dh] + ([res] if has_res else [])
    return pl.pallas_call(
        body, name=name, grid=(S // tm,),
        in_specs=[row, vec, row] + ([row] if has_res else []),
        out_specs=[row, vec],
        out_shape=[jax.ShapeDtypeStruct((S, D), out_dtype), jax.ShapeDtypeStruct((1, D), F32)],
        compiler_params=_cparams(("arbitrary",)),
    )(*ins)


def _rel_bucket(dist):
    max_exact = REL_BUCKETS // 2
    d_f = jnp.maximum(dist, 1).astype(jnp.float32)
    large = max_exact + (jnp.log(d_f / max_exact) / math.log(REL_MAX_DIST / max_exact)
                         * (REL_BUCKETS - max_exact)).astype(jnp.int32)
    large = jnp.minimum(large, REL_BUCKETS - 1)
    return jnp.where(dist < max_exact, dist, large)


def _bias_tables(rel_bias):
    qi = jnp.arange(Q_BLOCK)[:, None]
    kj = jnp.arange(2 * Q_BLOCK)[None, :]
    m = qi - kj + Q_BLOCK
    band_ok = (m >= 0) & (m <= Q_BLOCK)
    tabs = []
    for d in DILATIONS:
        bucket = _rel_bucket(jnp.maximum(m, 0) * d)
        bias = jnp.transpose(rel_bias[bucket].astype(F32), (2, 0, 1))
        tabs.append(jnp.where(band_ok[None], bias, NEG_INF))
    return jnp.stack(tabs)


def _rows(r, d):
    return pl.ds(r, Q_BLOCK, stride=d) if d > 1 else pl.ds(0, Q_BLOCK)


def _attn_fwd(proj, bias, state, d, last, cat_in, name):
    S = proj.shape[0]
    R = Q_BLOCK * d
    nsb = S // R
    first = state is None
    scale = HEAD_DIM ** -0.5

    def body(*refs):
        q_ref, kp_ref, kc_ref, vp_ref, vc_ref, b_ref = refs[:6]
        pos = 6
        if not first:
            m_in, l_in, a_in = refs[pos:pos + 3]
            pos += 3
        if last:
            pos += 1
            o_ref, lse_ref, o_tmp = refs[pos:pos + 3]
        else:
            m_out, l_out, a_out = refs[pos:pos + 3]
        n = pl.program_id(0)
        bp = b_ref[0, :, :Q_BLOCK]
        bc = b_ref[0, :, Q_BLOCK:]
        for r in range(d):
            rows = _rows(r, d)
            q = q_ref[rows, :]
            sp = _dot_nt(q, kp_ref[rows, :]) * scale + bp
            sp = jnp.where(n == 0, NEG_INF, sp)
            sc = _dot_nt(q, kc_ref[rows, :]) * scale + bc
            mrow = jnp.maximum(jnp.max(sp, axis=-1, keepdims=True), jnp.max(sc, axis=-1, keepdims=True))
            if first:
                m_new = mrow
            else:
                m_old = m_in[rows, :][:, :1]
                m_new = jnp.maximum(m_old, mrow)
            pp = jnp.exp(sp - m_new)
            pc = jnp.exp(sc - m_new)
            lrow = jnp.sum(pp, axis=-1, keepdims=True) + jnp.sum(pc, axis=-1, keepdims=True)
            pv = _dot_nn(pp, vp_ref[rows, :]) + _dot_nn(pc, vc_ref[rows, :])
            if first:
                l_new, a_new = lrow, pv
            else:
                alpha = jnp.exp(m_old - m_new)
                l_new = alpha * l_in[rows, :][:, :1] + lrow
                a_new = alpha * a_in[rows, :] + pv
            if last:
                o_tmp[rows, :] = a_new / l_new
                lse_ref[rows, :] = jnp.broadcast_to(m_new + jnp.log(l_new), (Q_BLOCK, HEAD_DIM))
            else:
                m_out[rows, :] = jnp.broadcast_to(m_new, (Q_BLOCK, HEAD_DIM))
                l_out[rows, :] = jnp.broadcast_to(l_new, (Q_BLOCK, HEAD_DIM))
                a_out[rows, :] = a_new
        if last:
            o_ref[...] = o_tmp[...].astype(BF16)

    def col(c0):
        return pl.BlockSpec((R, HEAD_DIM), lambda n, h: (n, c0 + h))

    def col_prev(c0):
        return pl.BlockSpec((R, HEAD_DIM), lambda n, h: (jnp.maximum(n - 1, 0), c0 + h))

    in_specs = [col(0), col_prev(N_HEADS), col(N_HEADS), col_prev(2 * N_HEADS), col(2 * N_HEADS),
                pl.BlockSpec((1, Q_BLOCK, 2 * Q_BLOCK), lambda n, h: (h, 0, 0))]
    ins = [proj, proj, proj, proj, proj, bias]
    if not first:
        in_specs += [col(0)] * 3
        ins += list(state)
    st = jax.ShapeDtypeStruct((S, ATTN_W), F32)
    if last:
        in_specs.append(pl.BlockSpec(memory_space=pl.ANY))
        ins.append(cat_in)
        out_specs = [col(0), col(0)]
        out_shape = [jax.ShapeDtypeStruct(cat_in.shape, BF16), st]
        scratch = [pltpu.VMEM((R, HEAD_DIM), F32)]
        aliases = {len(ins) - 1: 0}
    else:
        out_specs = [col(0)] * 3
        out_shape = [st, st, st]
        scratch = []
        aliases = {}
    return pl.pallas_call(
        body, name=name, grid=(nsb, N_HEADS), in_specs=in_specs, out_specs=out_specs,
        out_shape=out_shape, scratch_shapes=scratch, input_output_aliases=aliases,
        compiler_params=_cparams(("arbitrary", "arbitrary")),
    )(*ins)


def _attn_bwd(proj, bias, cat, d_cat, lse, acc, d, last, name):
    S = proj.shape[0]
    R = Q_BLOCK * d
    nsb = S // R
    first = acc is None
    scale = HEAD_DIM ** -0.5
    odt = BF16 if last else F32

    def body(*refs):
        q_ref, kp_ref, kc_ref, vp_ref, vc_ref, b_ref, o_ref, do_ref, lse_ref = refs[:9]
        pos = 9
        if not first:
            dq_in, dk_in, dv_in = refs[pos:pos + 3]
            pos += 3
        dq_out, dk_out, dv_out, db_out = refs[pos:pos + 4]
        pos += 4
        ck, cv, o_f, do_f, dq_t, dk_t, dv_t, db_acc = refs[pos:pos + 8]
        n = pl.program_id(1)

        @pl.when(n == 0)
        def _():
            db_acc[...] = jnp.zeros_like(db_acc)
            ck[...] = jnp.zeros_like(ck)
            cv[...] = jnp.zeros_like(cv)

        @pl.when(n < nsb)
        def _():
            o_f[...] = o_ref[...].astype(F32)
            do_f[...] = do_ref[...].astype(F32)
            bp = b_ref[0, :, :Q_BLOCK]
            bc = b_ref[0, :, Q_BLOCK:]
            for r in range(d):
                rows = _rows(r, d)
                q = q_ref[rows, :]
                kp = kp_ref[rows, :]
                kc = kc_ref[rows, :]
                vp = vp_ref[rows, :]
                vc = vc_ref[rows, :]
                do = do_f[rows, :]
                lse_r = lse_ref[rows, :][:, :1]
                sp = _dot_nt(q, kp) * scale + bp
                sp = jnp.where(n == 0, NEG_INF, sp)
                sc = _dot_nt(q, kc) * scale + bc
                pp = jnp.exp(sp - lse_r)
                pc = jnp.exp(sc - lse_r)
                dd = jnp.sum(do * o_f[rows, :], axis=-1, keepdims=True)
                dsp = pp * (_dot_nt(do, vp) - dd)
                dsc = pc * (_dot_nt(do, vc) - dd)
                db_acc[:, :Q_BLOCK] += dsp
                db_acc[:, Q_BLOCK:] += dsc
                dq = (_dot_nn(dsp, kp) + _dot_nn(dsc, kc)) * scale
                dk_prev = ck[rows, :] + _dot_tn(dsp, q) * scale
                dv_prev = cv[rows, :] + _dot_tn(pp, do)
                ck[rows, :] = _dot_tn(dsc, q) * scale
                cv[rows, :] = _dot_tn(pc, do)
                if not first:
                    dq = dq + dq_in[rows, :]
                    dk_prev = dk_prev + dk_in[rows, :]
                    dv_prev = dv_prev + dv_in[rows, :]
                dq_t[rows, :] = dq
                dk_t[rows, :] = dk_prev
                dv_t[rows, :] = dv_prev
            dq_out[...] = dq_t[...].astype(odt)

            @pl.when(n > 0)
            def _():
                dk_out[...] = dk_t[...].astype(odt)
                dv_out[...] = dv_t[...].astype(odt)

        @pl.when(n == nsb)
        def _():
            if first:
                dk_out[...] = ck[...].astype(odt)
                dv_out[...] = cv[...].astype(odt)
            else:
                dk_out[...] = (ck[...] + dk_in[...]).astype(odt)
                dv_out[...] = (cv[...] + dv_in[...]).astype(odt)
            db_out[0] = db_acc[...]

    last_n = nsb - 1

    def cur(c0):
        return pl.BlockSpec((R, HEAD_DIM), lambda h, n: (jnp.minimum(n, last_n), c0 + h))

    def prev(c0):
        return pl.BlockSpec((R, HEAD_DIM), lambda h, n: (jnp.maximum(jnp.minimum(n, last_n) - 1, 0), c0 + h))

    delayed = pl.BlockSpec((R, HEAD_DIM), lambda h, n: (jnp.maximum(n - 1, 0), h))
    in_specs = [cur(0), prev(N_HEADS), cur(N_HEADS), prev(2 * N_HEADS), cur(2 * N_HEADS),
                pl.BlockSpec((1, Q_BLOCK, 2 * Q_BLOCK), lambda h, n: (h, 0, 0)),
                cur(0), cur(0), cur(0)]
    ins = [proj, proj, proj, proj, proj, bias, cat, d_cat, lse]
    if not first:
        in_specs += [cur(0), delayed, delayed]
        ins += list(acc)
    st = jax.ShapeDtypeStruct((S, ATTN_W), odt)
    tile = pltpu.VMEM((R, HEAD_DIM), F32)
    return pl.pallas_call(
        body, name=name, grid=(N_HEADS, nsb + 1), in_specs=in_specs,
        out_specs=[cur(0), delayed, delayed,
                   pl.BlockSpec((1, Q_BLOCK, 2 * Q_BLOCK), lambda h, n: (h, 0, 0))],
        out_shape=[st, st, st, jax.ShapeDtypeStruct((N_HEADS, Q_BLOCK, 2 * Q_BLOCK), F32)],
        scratch_shapes=[tile] * 7 + [pltpu.VMEM((Q_BLOCK, 2 * Q_BLOCK), F32)],
        compiler_params=_cparams(("arbitrary", "arbitrary")),
    )(*ins)


HG_LEVELS = (32, 16, 8, 4, 2, 1)
N_LEV = len(HG_LEVELS)


def _hg_consts():
    C = HG_CHUNK
    t = np.arange(C)
    mq, mk, masks = [], [], []
    for B in HG_LEVELS:
        up = (t // B) % 2 == 1
        bs = (t // B) * B
        be = bs + B - 1
        mq.append(up[:, None] & (t[None, :] >= bs[:, None]) & (t[None, :] <= t[:, None]))
        mk.append((~up)[:, None] & (t[None, :] > t[:, None]) & (t[None, :] <= be[:, None]))
        masks.append(up[:, None] & (~up)[None, :] & ((t[:, None] // (2 * B)) == (t[None, :] // (2 * B))))
    masks.append(np.eye(C, dtype=bool))
    mb = t[None, :] <= t[:, None]
    mw = t[None, :] > t[:, None]
    m_all = np.concatenate(mq + mk + [mb, mw], axis=0).astype(np.float32)
    return jnp.asarray(m_all, BF16), jnp.asarray(np.stack(masks).astype(np.float32))


def _split3(v):
    hi = v.astype(BF16)
    r1 = v - hi.astype(F32)
    mid = r1.astype(BF16)
    lo = (r1 - mid.astype(F32)).astype(BF16)
    return jnp.concatenate([hi, mid, lo], axis=1)


def _hg_chunk_fwd(fz, iv, qz, lbh, m_all, mask_ref, st_t):
    C = HG_CHUNK
    sig = _sigmoid(fz)
    f = lbh + (1.0 - lbh) * sig
    lf = jnp.log(f)
    kk = 1.0 - f
    sq = _sigmoid(qz)
    qq = qz * sq
    a3 = lax.dot_general(m_all, _split3(lf), (((1,), (0,)), ((), ())), preferred_element_type=F32)
    args = a3[:, :HEAD_DIM] + a3[:, HEAD_DIM:2 * HEAD_DIM] + a3[:, 2 * HEAD_DIM:]
    e = jnp.exp(args)
    qs = [qq * e[j * C:(j + 1) * C] for j in range(N_LEV)]
    ks = [kk * e[(N_LEV + j) * C:(N_LEV + j + 1) * C] for j in range(N_LEV)]
    a = mask_ref[N_LEV] * _dot_nt(qq, kk)
    for j in range(N_LEV):
        a = a + mask_ref[j] * _dot_nt(qs[j], ks[j])
    eb = e[2 * N_LEV * C:(2 * N_LEV + 1) * C]
    ew = e[(2 * N_LEV + 1) * C:]
    qe = qq * eb
    w = kk * ew
    o = _dot_nt(qe, st_t) + _dot_nn(a, iv)
    eb_last = eb[C - 1:C, :]
    new_st = st_t * eb_last + _dot_tn(iv, w)
    return dict(sig=sig, f=f, kk=kk, sq=sq, qq=qq, e=e, qs=qs, ks=ks, a=a, eb=eb, ew=ew, qe=qe, w=w, o=o,
                eb_last=eb_last, new_st=new_st)


def _hg_rows(S):
    return _blk(S, (256, 128, 64))


def _hgrn_fwd(proj, lb, gain, cat_in, name):
    S = proj.shape[0]
    rows = _hg_rows(S)
    cb = rows // HG_CHUNK
    m_all, masks = _hg_consts()

    def body(fz_ref, iv_ref, qz_ref, gz_ref, lb_ref, gain_ref, m_ref, mask_ref, cat_any, o_ref, st_ref, st_scr):
        @pl.when(pl.program_id(0) == 0)
        def _():
            st_scr[...] = jnp.zeros_like(st_scr)

        m_all_v = m_ref[...]
        for h in range(N_HEADS):
            sl = slice(h * HEAD_DIM, (h + 1) * HEAD_DIM)
            lbh = lb_ref[:, sl]
            gh = gain_ref[:, sl]

            def chunk(c, carry, sl=sl, lbh=lbh, gh=gh, h=h):
                rs = pl.ds(pl.multiple_of(c * HG_CHUNK, HG_CHUNK), HG_CHUNK)
                st_t = st_scr[h]
                st_ref[c, h] = st_t
                gz = gz_ref[rs, sl]
                iv = iv_ref[rs, sl]
                q = _hg_chunk_fwd(fz_ref[rs, sl], iv, qz_ref[rs, sl], lbh, m_all_v, mask_ref, st_t)
                st_scr[h] = q["new_st"]
                o = q["o"]
                r = lax.rsqrt(jnp.mean(o * o, axis=-1, keepdims=True) + RMS_EPS)
                y = ((o * r) * gh) * (gz * _sigmoid(gz))
                o_ref[rs, sl] = y.astype(BF16)
                return carry

            lax.fori_loop(0, cb, chunk, 0)

    def col(c):
        return pl.BlockSpec((rows, HG_W), lambda i: (i, c))

    vec = pl.BlockSpec((1, HG_W), lambda i: (0, 0))
    return pl.pallas_call(
        body, name=name, grid=(S // rows,),
        in_specs=[col(3), col(4), col(5), col(6), vec, vec,
                  pl.BlockSpec(m_all.shape, lambda i: (0, 0)), pl.BlockSpec(masks.shape, lambda i: (0, 0, 0)),
                  pl.BlockSpec(memory_space=pl.ANY)],
        out_specs=[col(1), pl.BlockSpec((cb, N_HEADS, HEAD_DIM, HEAD_DIM), lambda i: (i, 0, 0, 0))],
        out_shape=[jax.ShapeDtypeStruct(cat_in.shape, BF16),
                   jax.ShapeDtypeStruct((S // HG_CHUNK, N_HEADS, HEAD_DIM, HEAD_DIM), F32)],
        scratch_shapes=[pltpu.VMEM((N_HEADS, HEAD_DIM, HEAD_DIM), F32)],
        input_output_aliases={8: 0},
        compiler_params=_cparams(("arbitrary",)),
    )(proj, proj, proj, proj, lb, gain, m_all, masks, cat_in)


def _hgrn_bwd(proj, lb, gain, states, d_cat, name):
    S = proj.shape[0]
    rows = _hg_rows(S)
    cb = rows // HG_CHUNK
    nblk = S // rows
    C = HG_CHUNK
    m_all, masks = _hg_consts()

    def body(fz_ref, iv_ref, qz_ref, gz_ref, lb_ref, gain_ref, m_ref, mask_ref, st_ref, dy_ref,
             dz_ref, dlb_ref, dgain_ref, dst_scr):
        @pl.when(pl.program_id(0) == 0)
        def _():
            dst_scr[...] = jnp.zeros_like(dst_scr)
            dlb_ref[...] = jnp.zeros_like(dlb_ref)
            dgain_ref[...] = jnp.zeros_like(dgain_ref)

        m_all_v = m_ref[...]
        last_row = lax.broadcasted_iota(jnp.int32, (C, HEAD_DIM), 0) == C - 1
        for h in range(N_HEADS):
            sl = slice(h * HEAD_DIM, (h + 1) * HEAD_DIM)
            lbh = lb_ref[:, sl]
            gh = gain_ref[:, sl]

            def chunk(ci, carry, sl=sl, lbh=lbh, gh=gh, h=h):
                c = cb - 1 - ci
                rs = pl.ds(pl.multiple_of(c * C, C), C)
                st_t = st_ref[c, h]
                fz = fz_ref[rs, sl]
                iv = iv_ref[rs, sl]
                qz = qz_ref[rs, sl]
                gz = gz_ref[rs, sl]
                q = _hg_chunk_fwd(fz, iv, qz, lbh, m_all_v, mask_ref, st_t)
                o = q["o"]
                r = lax.rsqrt(jnp.mean(o * o, axis=-1, keepdims=True) + RMS_EPS)
                on = o * r
                sg = _sigmoid(gz)
                gate = gz * sg
                dy = dy_ref[rs, sl].astype(F32)
                dgain_ref[:, sl] += jnp.sum(dy * on * gate, axis=0, keepdims=True)
                dgz = (dy * on * gh) * (sg * (1.0 + gz * (1.0 - sg)))
                don = dy * gh * gate
                do = r * (don - on * jnp.mean(don * on, axis=-1, keepdims=True))
                da = _dot_nt(do, iv)
                dv = _dot_tn(q["a"], do)
                dqe = _dot_nn(do, st_t)
                dst_new = _dot_tn(do, q["qe"])
                dsp = dst_scr[h]
                dw = _dot_nn(iv, dsp)
                dv = dv + _dot_nt(q["w"], dsp)
                d_eb_last = jnp.sum(dsp * st_t, axis=0, keepdims=True)
                dst_scr[h] = dsp * q["eb_last"] + dst_new
                dad = da * mask_ref[N_LEV]
                dq = _dot_nn(dad, q["kk"])
                dk = _dot_tn(dad, q["qq"])
                dargs_q = []
                dargs_k = []
                for j in range(N_LEV):
                    daj = da * mask_ref[j]
                    dqj = _dot_nn(daj, q["ks"][j])
                    dkj = _dot_tn(daj, q["qs"][j])
                    dq = dq + dqj * q["e"][j * C:(j + 1) * C]
                    dk = dk + dkj * q["e"][(N_LEV + j) * C:(N_LEV + j + 1) * C]
                    dargs_q.append(dqj * q["qs"][j])
                    dargs_k.append(dkj * q["ks"][j])
                dq = dq + dqe * q["eb"]
                darg_b = dqe * q["qe"] + jnp.where(last_row, d_eb_last * q["eb_last"], 0.0)
                dk = dk + dw * q["ew"]
                darg_w = dw * q["w"]
                dall = jnp.concatenate(dargs_q + dargs_k + [darg_b, darg_w], axis=0)
                hi = dall.astype(BF16)
                lo = (dall - hi.astype(F32)).astype(BF16)
                dl2 = lax.dot_general(m_all_v, jnp.concatenate([hi, lo], axis=1), (((0,), (0,)), ((), ())),
                                      preferred_element_type=F32)
                dlf = dl2[:, :HEAD_DIM] + dl2[:, HEAD_DIM:]
                df = dlf / q["f"] - dk
                sig = q["sig"]
                dlb_ref[:, sl] += jnp.sum(df * (1.0 - sig), axis=0, keepdims=True)
                dfz = df * (1.0 - lbh) * (sig * (1.0 - sig))
                sq = q["sq"]
                dqz = dq * (sq * (1.0 + qz * (1.0 - sq)))
                dz_ref[rs, h * HEAD_DIM:(h + 1) * HEAD_DIM] = dfz.astype(BF16)
                dz_ref[rs, HG_W + h * HEAD_DIM:HG_W + (h + 1) * HEAD_DIM] = dv.astype(BF16)
                dz_ref[rs, 2 * HG_W + h * HEAD_DIM:2 * HG_W + (h + 1) * HEAD_DIM] = dqz.astype(BF16)
                dz_ref[rs, 3 * HG_W + h * HEAD_DIM:3 * HG_W + (h + 1) * HEAD_DIM] = dgz.astype(BF16)
                return carry

            lax.fori_loop(0, cb, chunk, 0)

    def col(c):
        return pl.BlockSpec((rows, HG_W), lambda i: (nblk - 1 - i, c))

    vec = pl.BlockSpec((1, HG_W), lambda i: (0, 0))
    return pl.pallas_call(
        body, name=name, grid=(nblk,),
        in_specs=[col(3), col(4), col(5), col(6), vec, vec,
                  pl.BlockSpec(m_all.shape, lambda i: (0, 0)), pl.BlockSpec(masks.shape, lambda i: (0, 0, 0)),
                  pl.BlockSpec((cb, N_HEADS, HEAD_DIM, HEAD_DIM), lambda i: (nblk - 1 - i, 0, 0, 0)),
                  col(1)],
        out_specs=[pl.BlockSpec((rows, 4 * HG_W), lambda i: (nblk - 1 - i, 0)), vec, vec],
        out_shape=[jax.ShapeDtypeStruct((S, 4 * HG_W), BF16), jax.ShapeDtypeStruct((1, HG_W), F32),
                   jax.ShapeDtypeStruct((1, HG_W), F32)],
        scratch_shapes=[pltpu.VMEM((N_HEADS, HEAD_DIM, HEAD_DIM), F32)],
        compiler_params=_cparams(("arbitrary",)),
    )(proj, proj, proj, proj, lb, gain, m_all, masks, states, d_cat)


def _cross_fwd(cq, ckv, name):
    S = cq.shape[0]
    n_mem = ckv.shape[0]
    tq = _blk(S, (256, 128))
    scale = CROSS_DIM ** -0.5

    def body(q_ref, kv_ref, o_ref):
        for h in range(CROSS_HEADS):
            sl = slice(h * CROSS_DIM, (h + 1) * CROSS_DIM)
            k = kv_ref[:, sl]
            v = kv_ref[:, D_MODEL + h * CROSS_DIM:D_MODEL + (h + 1) * CROSS_DIM]
            s = _dot_nt(q_ref[:, sl], k) * scale
            m = jnp.max(s, axis=-1, keepdims=True)
            p = jnp.exp(s - m)
            p = p / jnp.sum(p, axis=-1, keepdims=True)
            o_ref[:, sl] = _dot_nn(p, v).astype(BF16)

    return pl.pallas_call(
        body, name=name, grid=(S // tq,),
        in_specs=[pl.BlockSpec((tq, D_MODEL), lambda i: (i, 0)), pl.BlockSpec((n_mem, 2 * D_MODEL), lambda i: (0, 0))],
        out_specs=pl.BlockSpec((tq, D_MODEL), lambda i: (i, 0)),
        out_shape=jax.ShapeDtypeStruct((S, D_MODEL), BF16),
        compiler_params=_cparams(("parallel",)),
    )(cq, ckv)


def _cross_bwd(cq, ckv, d_o, name):
    S = cq.shape[0]
    n_mem = ckv.shape[0]
    tq = _blk(S, (256, 128))
    scale = CROSS_DIM ** -0.5

    def body(q_ref, kv_ref, do_ref, dq_ref, dkv_ref):
        @pl.when(pl.program_id(0) == 0)
        def _():
            dkv_ref[...] = jnp.zeros_like(dkv_ref)

        for h in range(CROSS_HEADS):
            sl = slice(h * CROSS_DIM, (h + 1) * CROSS_DIM)
            slv = slice(D_MODEL + h * CROSS_DIM, D_MODEL + (h + 1) * CROSS_DIM)
            q = q_ref[:, sl]
            k = kv_ref[:, sl]
            v = kv_ref[:, slv]
            do = do_ref[:, sl]
            s = _dot_nt(q, k) * scale
            m = jnp.max(s, axis=-1, keepdims=True)
            p = jnp.exp(s - m)
            p = p / jnp.sum(p, axis=-1, keepdims=True)
            dp = _dot_nt(do, v)
            ds = p * (dp - jnp.sum(dp * p, axis=-1, keepdims=True)) * scale
            dq_ref[:, sl] = _dot_nn(ds, k).astype(BF16)
            dkv_ref[:, sl] += _dot_tn(ds, q)
            dkv_ref[:, slv] += _dot_tn(p, do)

    row = pl.BlockSpec((tq, D_MODEL), lambda i: (i, 0))
    kv = pl.BlockSpec((n_mem, 2 * D_MODEL), lambda i: (0, 0))
    return pl.pallas_call(
        body, name=name, grid=(S // tq,),
        in_specs=[row, kv, row], out_specs=[row, kv],
        out_shape=[jax.ShapeDtypeStruct((S, D_MODEL), BF16), jax.ShapeDtypeStruct((n_mem, 2 * D_MODEL), F32)],
        compiler_params=_cparams(("arbitrary",)),
    )(cq, ckv, d_o)


FF_BLOCK = 1408


def _swiglu_fwd(gu, name):
    S = gu.shape[0]
    tm = _blk(S, (256, 128))
    nj = D_FF // FF_BLOCK

    def body(g_ref, u_ref, o_ref):
        g = g_ref[...].astype(F32)
        o_ref[...] = ((g * _sigmoid(g)) * u_ref[...].astype(F32)).astype(BF16)

    return pl.pallas_call(
        body, name=name, grid=(S // tm, nj),
        in_specs=[pl.BlockSpec((tm, FF_BLOCK), lambda i, j: (i, j)),
                  pl.BlockSpec((tm, FF_BLOCK), lambda i, j: (i, j + nj))],
        out_specs=pl.BlockSpec((tm, FF_BLOCK), lambda i, j: (i, j)),
        out_shape=jax.ShapeDtypeStruct((S, D_FF), BF16),
        compiler_params=_cparams(("parallel", "parallel")),
    )(gu, gu)


def _swiglu_bwd(gu, d_act, name):
    S = gu.shape[0]
    tm = _blk(S, (256, 128))
    nj = D_FF // FF_BLOCK

    def body(g_ref, u_ref, da_ref, o_ref):
        part = pl.program_id(2)
        g = g_ref[...].astype(F32)
        u = u_ref[...].astype(F32)
        da = da_ref[...].astype(F32)
        sg = _sigmoid(g)
        dgate = da * u * (sg * (1.0 + g * (1.0 - sg)))
        dup = da * (g * sg)
        o_ref[...] = jnp.where(part == 0, dgate, dup).astype(BF16)

    return pl.pallas_call(
        body, name=name, grid=(S // tm, nj, 2),
        in_specs=[pl.BlockSpec((tm, FF_BLOCK), lambda i, j, p: (i, j)),
                  pl.BlockSpec((tm, FF_BLOCK), lambda i, j, p: (i, j + nj)),
                  pl.BlockSpec((tm, FF_BLOCK), lambda i, j, p: (i, j))],
        out_specs=pl.BlockSpec((tm, FF_BLOCK), lambda i, j, p: (i, j + p * nj)),
        out_shape=jax.ShapeDtypeStruct((S, 2 * D_FF), BF16),
        compiler_params=_cparams(("parallel", "parallel", "arbitrary")),
    )(gu, gu, d_act)


def _loss_head(y, target, name):
    S, D = y.shape
    tm = _blk(S, (256, 128, 64, 8))

    def body(y_ref, t_ref, dy_ref, l_ref):
        diff = y_ref[...] - t_ref[...]
        dy_ref[...] = diff * (1.0 / D)
        sq = (diff * diff) * (0.5 / D)
        part = jnp.sum(sq.reshape(tm // 8, 8, D), axis=0)

        @pl.when(pl.program_id(0) == 0)
        def _():
            l_ref[...] = part

        @pl.when(pl.program_id(0) > 0)
        def _():
            l_ref[...] += part

    row = pl.BlockSpec((tm, D), lambda i: (i, 0))
    return pl.pallas_call(
        body, name=name, grid=(S // tm,), in_specs=[row, row],
        out_specs=[row, pl.BlockSpec((8, D), lambda i: (0, 0))],
        out_shape=[jax.ShapeDtypeStruct((S, D), F32), jax.ShapeDtypeStruct((8, D), F32)],
        compiler_params=_cparams(("arbitrary",)),
    )(y, target)


def _layer_fwd(x0, mem, g, w, lb, hg_gain, bias, tag):
    S = x0.shape[0]
    h0 = _rms_fwd(x0, g[0], f"rms0_{tag}")
    proj = _mm(h0, w["w_in"], "nn", F32, f"mm_in_{tag}")
    state = None
    cat = lax.empty((S, D_MODEL), BF16)
    for bi, d in enumerate(DILATIONS):
        last = bi == len(DILATIONS) - 1
        out = _attn_fwd(proj, bias[bi], state, d, last, cat if last else None, f"attn_fwd{d}_{tag}")
        if last:
            cat, lse = out
        else:
            state = tuple(out)
    cat, states = _hgrn_fwd(proj, lb, hg_gain, cat, f"hgrn_fwd_{tag}")
    mix = _mm(cat, w["w_out"], "nn", F32, f"mm_out_{tag}")
    x1 = _norm_residual(x0, mix, g[1], f"res1_{tag}")
    hc = _rms_fwd(x1, g[2], f"rms2_{tag}")
    cq = _mm(hc, w["w_cq"], "nn", BF16, f"mm_cq_{tag}")
    mn = _rms_fwd(mem, g[3], f"rms3_{tag}")
    ckv = _mm(mn, w["w_ckv"], "nn", BF16, f"mm_ckv_{tag}")
    cop = _cross_fwd(cq, ckv, f"cross_fwd_{tag}")
    co = _mm(cop, w["w_co"], "nn", F32, f"mm_co_{tag}")
    x2 = _norm_residual(x1, co, g[4], f"res2_{tag}")
    hf = _rms_fwd(x2, g[5], f"rms5_{tag}")
    gu = _mm(hf, w["w_gate_up"], "nn", BF16, f"mm_gu_{tag}")
    act = _swiglu_fwd(gu, f"swiglu_fwd_{tag}")
    y = _mm(act, w["w_down"], "nn", F32, f"mm_down_{tag}")
    x3 = _norm_residual(x2, y, g[6], f"res3_{tag}")
    saved = dict(x0=x0, h0=h0, proj=proj, cat=cat, lse=lse, states=states, mix=mix, x1=x1, hc=hc, cq=cq, mn=mn,
                 ckv=ckv, cop=cop, co=co, x2=x2, hf=hf, gu=gu, act=act, y=y)
    return x3, saved


def _layer_bwd(dx3, sv, mem, g, w, lb, hg_gain, bias, tag):
    dg = [None] * N_NORMS
    gw = {}
    dy, dg[6] = _rms_bwd(sv["y"], g[6], dx3, None, BF16, f"rmsb6_{tag}")
    d_act = _mm(dy, w["w_down"], "nt", BF16, f"mmb_down_x_{tag}")
    gw["w_down"] = _mm(sv["act"], dy, "tn", BF16, f"mmb_down_w_{tag}")
    dgu = _swiglu_bwd(sv["gu"], d_act, f"swiglu_bwd_{tag}")
    gw["w_gate_up"] = _mm(sv["hf"], dgu, "tn", BF16, f"mmb_gu_w_{tag}")
    d_hf = _mm(dgu, w["w_gate_up"], "nt", F32, f"mmb_gu_x_{tag}")
    dx2, dg[5] = _rms_bwd(sv["x2"], g[5], d_hf, dx3, F32, f"rmsb5_{tag}")
    d_co, dg[4] = _rms_bwd(sv["co"], g[4], dx2, None, BF16, f"rmsb4_{tag}")
    d_cop = _mm(d_co, w["w_co"], "nt", BF16, f"mmb_co_x_{tag}")
    gw["w_co"] = _mm(sv["cop"], d_co, "tn", BF16, f"mmb_co_w_{tag}")
    d_cq, d_ckv = _cross_bwd(sv["cq"], sv["ckv"], d_cop, f"cross_bwd_{tag}")
    gw["w_cq"] = _mm(sv["hc"], d_cq, "tn", BF16, f"mmb_cq_w_{tag}")
    d_hc = _mm(d_cq, w["w_cq"], "nt", F32, f"mmb_cq_x_{tag}")
    gw["w_ckv"] = _mm(sv["mn"], d_ckv, "tn", BF16, f"mmb_ckv_w_{tag}")
    d_mn = _mm(d_ckv, w["w_ckv"], "nt", F32, f"mmb_ckv_x_{tag}")
    _, dg[3] = _rms_bwd(mem, g[3], d_mn, None, BF16, f"rmsb3_{tag}")
    dx1, dg[2] = _rms_bwd(sv["x1"], g[2], d_hc, dx2, F32, f"rmsb2_{tag}")
    d_mix, dg[1] = _rms_bwd(sv["mix"], g[1], dx1, None, BF16, f"rmsb1_{tag}")
    d_cat = _mm(d_mix, w["w_out"], "nt", BF16, f"mmb_out_x_{tag}")
    gw["w_out"] = _mm(sv["cat"], d_mix, "tn", BF16, f"mmb_out_w_{tag}")
    acc = None
    dbias = []
    for bi, d in enumerate(DILATIONS):
        last = bi == len(DILATIONS) - 1
        dq, dk, dv, db = _attn_bwd(sv["proj"], bias[bi], sv["cat"], d_cat, sv["lse"], acc, d, last,
                                   f"attn_bwd{d}_{tag}")
        acc = (dq, dk, dv)
        dbias.append(db)
    d_hz, dlb, dgain = _hgrn_bwd(sv["proj"], lb, hg_gain, sv["states"], d_cat, f"hgrn_bwd_{tag}")
    parts = [acc[0], acc[1], acc[2], d_hz]
    gw["w_in"] = _mm(sv["h0"], parts, "tn", BF16, f"mmb_in_w_{tag}")
    d_h0 = _mm(parts, w["w_in"], "nt", F32, f"mmb_in_x_{tag}")
    dx0, dg[0] = _rms_bwd(sv["x0"], g[0], d_h0, dx1, F32, f"rmsb0_{tag}")
    return dx0, gw, jnp.stack(dg), dlb, dgain, jnp.stack(dbias)


def _lb_all(lb_logits):
    p = jax.nn.softmax(lb_logits.astype(F32), axis=0)
    return jnp.cumsum(p, axis=0) - p


def _local_step(x, mem, target, rel_bias, lb_logits, norm_gains, hg_norm, weights_of_layer, on_layer_grads=None):
    L = lb_logits.shape[0]
    bias, bias_vjp = jax.vjp(_bias_tables, rel_bias)
    lb_all, lb_vjp = jax.vjp(_lb_all, lb_logits)
    gains = norm_gains.reshape(L, N_NORMS, 1, D_MODEL)
    saved = []
    h = x
    ws = []
    for l in range(L):
        w = weights_of_layer(l)
        ws.append(w)
        h, sv = _layer_fwd(h, mem, gains[l], w, lb_all[l:l + 1], hg_norm[l:l + 1], bias, f"l{l}")
        saved.append(sv)
    dy, lparts = _loss_head(h, target, "loss_head")
    loss = jnp.sum(lparts)
    d_gains, d_lb, d_hg, gws = [None] * L, [None] * L, [None] * L, [None] * L
    d_bias = jnp.zeros_like(bias)
    dh = dy
    for l in reversed(range(L)):
        dh, gw, dgl, dlbl, dhgl, dbl = _layer_bwd(dh, saved[l], mem, gains[l], ws[l], lb_all[l:l + 1],
                                                  hg_norm[l:l + 1], bias, f"l{l}")
        d_gains[l], d_lb[l], d_hg[l], gws[l] = dgl.reshape(N_NORMS, D_MODEL), dlbl[0], dhgl[0], gw
        d_bias = d_bias + dbl
        if on_layer_grads is not None:
            gws[l] = on_layer_grads(l, gw)
    (d_rel_bias,) = bias_vjp(d_bias)
    (d_lb_logits,) = lb_vjp(jnp.stack(d_lb))
    return loss, dh, gws, d_rel_bias, d_lb_logits, jnp.stack(d_gains), jnp.stack(d_hg)


def kernel(x, mem, rel_bias, lb_logits, norm_gains, w_in, hg_norm, w_out, w_cq, w_ckv, w_co, w_gate_up, w_down, loss_target, m_rel_bias, m_lb_logits, m_norm_gains, m_w_in, m_hg_norm, m_w_out, m_w_cq, m_w_ckv, m_w_co, m_w_gate_up, m_w_down, v_rel_bias, v_lb_logits, v_norm_gains, v_w_in, v_hg_norm, v_w_out, v_w_cq, v_w_ckv, v_w_co, v_w_gate_up, v_w_down):
    return _train_step(
        x, mem, loss_target,
        dict(rel_bias=rel_bias, lb_logits=lb_logits, norm_gains=norm_gains, hg_norm=hg_norm, w_in=w_in, w_out=w_out,
             w_cq=w_cq, w_ckv=w_ckv, w_co=w_co, w_gate_up=w_gate_up, w_down=w_down),
        dict(rel_bias=m_rel_bias, lb_logits=m_lb_logits, norm_gains=m_norm_gains, hg_norm=m_hg_norm, w_in=m_w_in,
             w_out=m_w_out, w_cq=m_w_cq, w_ckv=m_w_ckv, w_co=m_w_co, w_gate_up=m_w_gate_up, w_down=m_w_down),
        dict(rel_bias=v_rel_bias, lb_logits=v_lb_logits, norm_gains=v_norm_gains, hg_norm=v_hg_norm, w_in=v_w_in,
             w_out=v_w_out, w_cq=v_w_cq, w_ckv=v_w_ckv, w_co=v_w_co, w_gate_up=v_w_gate_up, w_down=v_w_down))


BIG = ("w_in", "w_out", "w_cq", "w_ckv", "w_co", "w_gate_up", "w_down")
SHARD_AXIS = dict(w_in=1, w_out=0, w_cq=0, w_ckv=1, w_co=0, w_gate_up=1, w_down=0)
NB = len(BIG)
OUT_ORDER = ("rel_bias", "lb_logits", "norm_gains", "w_in", "hg_norm", "w_out", "w_cq", "w_ckv", "w_co",
             "w_gate_up", "w_down")
ANY = pl.BlockSpec(memory_space=pl.ANY)


def _place():
    x, y, c = lax.axis_index("x"), lax.axis_index("y"), lax.axis_index("c")
    chips = [(1 - x, y), (x, 1 - y), (1 - x, 1 - y)]
    return x, y, c, chips


def _remote(src, dst, send_sem, recv_sem, to):
    return pltpu.make_async_remote_copy(src_ref=src, dst_ref=dst, send_sem=send_sem, recv_sem=recv_sem,
                                        device_id=to, device_id_type=MESH_ID)


def _half_region(ref, axis, chip, half, lead=()):
    R, C = ref.shape[-2:]
    if axis == 0:
        rs = R // N_CHIPS
        return ref.at[(*lead, pl.ds(chip * rs + half * (rs // 2), rs // 2), slice(None))]
    cs = C // N_CHIPS
    return ref.at[(*lead, pl.ds(half * (R // 2), R // 2), pl.ds(chip * cs, cs))]


def _cast_bf16(w, name):
    L, A, B = w.shape
    tr = _blk(A, (256, 128, 64, 32, 16))

    def body(w_ref, o_ref):
        o_ref[...] = w_ref[...].astype(BF16)

    spec = pl.BlockSpec((1, tr, B), lambda l, i: (l, i, 0))
    return pl.pallas_call(
        body, name=name, grid=(L, A // tr), in_specs=[spec], out_specs=spec,
        out_shape=jax.ShapeDtypeStruct(w.shape, BF16), compiler_params=_cparams(("parallel", "parallel")),
    )(w)


def _allgather_weights(shards, l, name):
    names = list(BIG)
    full_shapes = []
    for n in names:
        _, rs, cs = shards[n].shape
        full_shapes.append((rs * N_CHIPS, cs) if SHARD_AXIS[n] == 0 else (rs, cs * N_CHIPS))

    def body(*refs):
        ins = refs[:NB]
        outs = refs[NB:2 * NB]
        send, recv, local = refs[2 * NB:]
        x, y, c, chips = _place()
        me = 2 * x + y
        sib = (x, y, 1 - c)
        locals_, sends = [], []
        for wi, n in enumerate(names):
            ax = SHARD_AXIS[n]
            R, C = full_shapes[wi]
            if ax == 0:
                dst = outs[wi].at[pl.ds(me * (R // N_CHIPS), R // N_CHIPS), :]
            else:
                dst = outs[wi].at[:, pl.ds(me * (C // N_CHIPS), C // N_CHIPS)]
            cp = pltpu.make_async_copy(ins[wi].at[l], dst, local.at[wi])
            cp.start()
            locals_.append(cp)
        for j, chip in enumerate(chips):
            for wi, n in enumerate(names):
                ax = SHARD_AXIS[n]
                rs = ins[wi].shape[1]
                src = ins[wi].at[l, pl.ds(c * (rs // 2), rs // 2), :]
                cp = _remote(src, _half_region(outs[wi], ax, me, c), send.at[j * NB + wi], recv.at[j * NB + wi],
                             (*chip, c))
                cp.start()
                sends.append(cp)
        for j, chip in enumerate(chips):
            them = 2 * chip[0] + chip[1]
            for wi, n in enumerate(names):
                reg = _half_region(outs[wi], SHARD_AXIS[n], them, c)
                _remote(reg, reg, send.at[j * NB + wi], recv.at[j * NB + wi], (*chip, c)).wait_recv()
                cp = _remote(reg, reg, send.at[(3 + j) * NB + wi], recv.at[(3 + j) * NB + wi], sib)
                cp.start()
                sends.append(cp)
        for j, chip in enumerate(chips):
            them = 2 * chip[0] + chip[1]
            for wi, n in enumerate(names):
                reg = _half_region(outs[wi], SHARD_AXIS[n], them, 1 - c)
                _remote(reg, reg, send.at[(3 + j) * NB + wi], recv.at[(3 + j) * NB + wi], sib).wait_recv()
        for cp in sends:
            cp.wait_send()
        for cp in locals_:
            cp.wait()

    outs = pl.pallas_call(
        body, name=name, in_specs=[ANY] * NB, out_specs=[ANY] * NB,
        out_shape=[jax.ShapeDtypeStruct(s, BF16) for s in full_shapes],
        scratch_shapes=[pltpu.SemaphoreType.DMA((6 * NB,)), pltpu.SemaphoreType.DMA((6 * NB,)),
                        pltpu.SemaphoreType.DMA((NB,))],
    )(*[shards[n] for n in names])
    return dict(zip(names, outs))


def _allgather_gains(g_shard, name):
    A, Cs = g_shard.shape

    def body(in_ref, out_ref, send, recv, local):
        x, y, c, chips = _place()
        me = 2 * x + y
        mine = pltpu.make_async_copy(in_ref, out_ref.at[me], local)
        mine.start()
        cps = [_remote(in_ref, out_ref.at[me], send.at[j], recv.at[j], (*chip, c)) for j, chip in enumerate(chips)]
        for cp in cps:
            cp.start()
        for j, chip in enumerate(chips):
            them = 2 * chip[0] + chip[1]
            _remote(in_ref, out_ref.at[them], send.at[j], recv.at[j], (*chip, c)).wait_recv()
        for cp in cps:
            cp.wait_send()
        mine.wait()

    return pl.pallas_call(
        body, name=name, in_specs=[ANY], out_specs=ANY,
        out_shape=jax.ShapeDtypeStruct((N_CHIPS, A, Cs), F32),
        scratch_shapes=[pltpu.SemaphoreType.DMA((3,)), pltpu.SemaphoreType.DMA((3,)), pltpu.SemaphoreType.DMA],
    )(g_shard)


def _half_shape(n, shape):
    R, C = shape
    if SHARD_AXIS[n] == 0:
        return (N_CHIPS, R // N_CHIPS // 2, C)
    return (R // 2, C)


def _as_halves(n, g):
    R, C = g.shape
    if SHARD_AXIS[n] == 0:
        return g.reshape(N_CHIPS, R // N_CHIPS, C)
    return g


def _my_half(n, ref, half):
    if SHARD_AXIS[n] == 0:
        hs = ref.shape[1] // 2
        return ref.at[:, pl.ds(half * hs, hs), :]
    hs = ref.shape[0] // 2
    return ref.at[pl.ds(half * hs, hs), :]


def _swap_sibling_halves(gw, name):
    names = list(BIG)

    def body(*refs):
        ins = refs[:NB]
        outs = refs[NB:2 * NB]
        send, recv = refs[2 * NB:]
        x, y, c, _ = _place()
        sib = (x, y, 1 - c)
        cps = []
        for wi, n in enumerate(names):
            cp = _remote(_my_half(n, ins[wi], 1 - c), outs[wi], send.at[wi], recv.at[wi], sib)
            cp.start()
            cps.append(cp)
        for cp in cps:
            cp.wait()

    shapes = []
    for n in names:
        g = gw[n]
        shapes.append((g.shape[0], g.shape[1] // 2, g.shape[2]) if g.ndim == 3 else (g.shape[0] // 2, g.shape[1]))
    outs = pl.pallas_call(
        body, name=name, in_specs=[ANY] * NB, out_specs=[ANY] * NB,
        out_shape=[jax.ShapeDtypeStruct(s, BF16) for s in shapes],
        scratch_shapes=[pltpu.SemaphoreType.DMA((NB,)), pltpu.SemaphoreType.DMA((NB,))],
    )(*[gw[n] for n in names])
    return dict(zip(names, outs))


def _pair_sum(n, g, other, c_arr, name):
    if g.ndim == 3:
        nc, rs, C = g.shape
        hs = rs // 2
        tr = _blk(hs, (256, 128, 64, 32, 16))
        nt = hs // tr
        grid = (nc, nt)
        g_spec = pl.BlockSpec((1, tr, C), lambda k, i, c_ref: (k, c_ref[0] * nt + i, 0))
        o_spec = pl.BlockSpec((1, tr, C), lambda k, i, c_ref: (k, i, 0))
    else:
        R, C = g.shape
        hs = R // 2
        cs = C // N_CHIPS
        tr = _blk(hs, (256, 128, 64, 32, 16))
        nt = hs // tr
        grid = (nt, N_CHIPS)
        g_spec = pl.BlockSpec((tr, cs), lambda i, k, c_ref: (c_ref[0] * nt + i, k))
        o_spec = pl.BlockSpec((tr, cs), lambda i, k, c_ref: (i, k))

    def body(c_ref, g_ref, o_ref, out_ref):
        out_ref[...] = (g_ref[...].astype(F32) + o_ref[...].astype(F32)).astype(BF16)

    return pl.pallas_call(
        body, name=name,
        grid_spec=pltpu.PrefetchScalarGridSpec(num_scalar_prefetch=1, grid=grid, in_specs=[g_spec, o_spec],
                                               out_specs=o_spec),
        out_shape=jax.ShapeDtypeStruct(other.shape, BF16),
        compiler_params=_cparams(("parallel", "parallel")),
    )(c_arr, g, other)


def _scatter_to_owners(pairs, name):
    names = list(BIG)

    def piece(n, ref, chip):
        if SHARD_AXIS[n] == 0:
            return ref.at[chip]
        cs = ref.shape[1] // N_CHIPS
        return ref.at[:, pl.ds(chip * cs, cs)]

    def body(*refs):
        ins = refs[:NB]
        outs = refs[NB:2 * NB]
        send, recv, local = refs[2 * NB:]
        x, y, c, chips = _place()
        me = 2 * x + y
        locals_, sends = [], []
        for wi, n in enumerate(names):
            cp = pltpu.make_async_copy(piece(n, ins[wi], me), outs[wi].at[me], local.at[wi])
            cp.start()
            locals_.append(cp)
        for j, chip in enumerate(chips):
            them = 2 * chip[0] + chip[1]
            for wi, n in enumerate(names):
                cp = _remote(piece(n, ins[wi], them), outs[wi].at[me], send.at[j * NB + wi], recv.at[j * NB + wi],
                             (*chip, c))
                cp.start()
                sends.append(cp)
        for j, chip in enumerate(chips):
            them = 2 * chip[0] + chip[1]
            for wi, n in enumerate(names):
                _remote(piece(n, ins[wi], me), outs[wi].at[them], send.at[j * NB + wi], recv.at[j * NB + wi],
                        (*chip, c)).wait_recv()
        for cp in sends:
            cp.wait_send()
        for cp in locals_:
            cp.wait()

    shapes = []
    for n in names:
        p = pairs[n]
        shapes.append((N_CHIPS,) + (p.shape[1:] if p.ndim == 3 else (p.shape[0], p.shape[1] // N_CHIPS)))
    outs = pl.pallas_call(
        body, name=name, in_specs=[ANY] * NB, out_specs=[ANY] * NB,
        out_shape=[jax.ShapeDtypeStruct(s, BF16) for s in shapes],
        scratch_shapes=[pltpu.SemaphoreType.DMA((3 * NB,)), pltpu.SemaphoreType.DMA((3 * NB,)),
                        pltpu.SemaphoreType.DMA((NB,))],
    )(*[pairs[n] for n in names])
    return dict(zip(names, outs))


def _sum_slots(q, name):
    K, A, B = q.shape
    tr = _blk(A, (256, 128, 64, 32, 16, 8))

    def body(q_ref, o_ref):
        s = q_ref[0].astype(F32)
        for k in range(1, K):
            s = s + q_ref[k].astype(F32)
        o_ref[...] = s

    return pl.pallas_call(
        body, name=name, grid=(A // tr,),
        in_specs=[pl.BlockSpec((K, tr, B), lambda i: (0, i, 0))], out_specs=pl.BlockSpec((tr, B), lambda i: (i, 0)),
        out_shape=jax.ShapeDtypeStruct((A, B), F32), compiler_params=_cparams(("parallel",)),
    )(q)


def _finish_grads(halves, small, name):
    names = list(BIG)
    L = len(halves)
    A = small.shape[0]
    out_shapes = []
    for n in names:
        hr, cc = halves[0][n].shape
        out_shapes.append((L, 2 * hr, cc))
    nh = L * NB

    def body(*refs):
        ins = refs[:nh]
        small_ref = refs[nh]
        outs = refs[nh + 1:nh + 1 + NB]
        slots = refs[nh + 1 + NB]
        send, recv, local = refs[nh + 2 + NB:]
        x, y, c, _ = _place()
        sib = (x, y, 1 - c)
        me = 4 * x + 2 * y + c
        locals_, sends = [], []
        for l in range(L):
            for wi, n in enumerate(names):
                src = ins[l * NB + wi]
                hr = src.shape[0]
                dst = outs[wi].at[l, pl.ds(c * hr, hr), :]
                cp = pltpu.make_async_copy(src, dst, local.at[l * NB + wi])
                cp.start()
                locals_.append(cp)
                cp = _remote(src, dst, send.at[l * NB + wi], recv.at[l * NB + wi], sib)
                cp.start()
                sends.append(cp)
        cp = pltpu.make_async_copy(small_ref, slots.at[me], local.at[nh])
        cp.start()
        locals_.append(cp)
        peers = []
        for dx in range(2):
            for dy in range(2):
                for dc in range(2):
                    if dx or dy or dc:
                        peers.append((dx, dy, dc))
        for j, (dx, dy, dc) in enumerate(peers):
            to = (jnp.bitwise_xor(x, dx), jnp.bitwise_xor(y, dy), jnp.bitwise_xor(c, dc))
            cp = _remote(small_ref, slots.at[me], send.at[nh + j], recv.at[nh + j], to)
            cp.start()
            sends.append(cp)
        for l in range(L):
            for wi, n in enumerate(names):
                src = ins[l * NB + wi]
                hr = src.shape[0]
                dst = outs[wi].at[l, pl.ds((1 - c) * hr, hr), :]
                _remote(src, dst, send.at[l * NB + wi], recv.at[l * NB + wi], sib).wait_recv()
        for j, (dx, dy, dc) in enumerate(peers):
            frm = 4 * jnp.bitwise_xor(x, dx) + 2 * jnp.bitwise_xor(y, dy) + jnp.bitwise_xor(c, dc)
            _remote(small_ref, slots.at[frm], send.at[nh + j], recv.at[nh + j], sib).wait_recv()
        for cp in sends:
            cp.wait_send()
        for cp in locals_:
            cp.wait()

    res = pl.pallas_call(
        body, name=name, in_specs=[ANY] * (nh + 1), out_specs=[ANY] * (NB + 1),
        out_shape=[jax.ShapeDtypeStruct(s, F32) for s in out_shapes] + [jax.ShapeDtypeStruct((8, A, LANES), F32)],
        scratch_shapes=[pltpu.SemaphoreType.DMA((nh + 7,)), pltpu.SemaphoreType.DMA((nh + 7,)),
                        pltpu.SemaphoreType.DMA((nh + 1,))],
    )(*[halves[l][n] for l in range(L) for n in names], small)
    return dict(zip(names, res[:NB])), res[NB]


def _adamw(w, g, m, v, name):
    shape = w.shape
    if w.ndim == 2:
        w, g, m, v = (t.reshape((1,) + shape) for t in (w, g, m, v))
    L, A, B = w.shape
    tr = _blk(A, (128, 64, 32, 16, 8))
    c1 = 1.0 - ADAM_B1 ** ADAM_STEP
    c2 = 1.0 - ADAM_B2 ** ADAM_STEP

    def body(w_ref, g_ref, m_ref, v_ref, d_ref, nm_ref, nv_ref):
        gv = g_ref[...]
        nm = ADAM_B1 * m_ref[...] + (1.0 - ADAM_B1) * gv
        nv = ADAM_B2 * v_ref[...] + (1.0 - ADAM_B2) * (gv * gv)
        m_hat = nm / c1
        v_hat = nv / c2
        d_ref[...] = -ADAM_LR * (m_hat / (jnp.sqrt(v_hat) + ADAM_EPS) + ADAM_WD * w_ref[...])
        nm_ref[...] = nm
        nv_ref[...] = nv

    spec = pl.BlockSpec((1, tr, B), lambda l, i: (l, i, 0))
    sds = jax.ShapeDtypeStruct((L, A, B), F32)
    outs = pl.pallas_call(
        body, name=name, grid=(L, A // tr), in_specs=[spec] * 4, out_specs=[spec] * 3, out_shape=[sds] * 3,
        compiler_params=_cparams(("parallel", "parallel")),
    )(w, g, m, v)
    return tuple(o.reshape(shape) for o in outs)


SMALL_ROWS = 520


def _train_step(x, mem, target, w, m, v):
    L = w["lb_logits"].shape[0]
    cx, cy, cc = lax.axis_index("x"), lax.axis_index("y"), lax.axis_index("c")
    me = 2 * cx + cy
    c_arr = jnp.reshape(cc, (1,)).astype(jnp.int32)

    shards = {n: _cast_bf16(w[n], f"cast_{n}") for n in BIG}
    full = [_allgather_weights(shards, l, f"allgather_l{l}") for l in range(L)]
    gs = _allgather_gains(w["norm_gains"].reshape(L * N_NORMS, -1), "allgather_gains")
    gains = jnp.transpose(gs, (1, 0, 2)).reshape(L, N_NORMS, D_MODEL)

    def reduce_layer(l, gw):
        views = {n: _as_halves(n, gw[n]) for n in BIG}
        theirs = _swap_sibling_halves(views, f"swap_halves_l{l}")
        pairs = {n: _pair_sum(n, views[n], theirs[n], c_arr, f"pair_sum_{n}_l{l}") for n in BIG}
        slots = _scatter_to_owners(pairs, f"scatter_l{l}")
        out = {}
        for n in BIG:
            q = slots[n]
            if q.ndim == 4:
                q = q.reshape(q.shape[0], q.shape[1] * q.shape[2], q.shape[3])
            out[n] = _sum_slots(q, f"sum_chips_{n}_l{l}")
        return out

    loss, dx, halves, d_rb, d_lb, d_gains, d_hg = _local_step(
        x[0], mem[0], target[0], w["rel_bias"], w["lb_logits"], gains, w["hg_norm"], lambda l: full[l], reduce_layer)

    flat = jnp.concatenate([d_rb.reshape(-1), d_lb.reshape(-1), d_hg.reshape(-1), d_gains.reshape(-1)])
    small = jnp.pad(flat, (0, SMALL_ROWS * LANES - flat.shape[0])).reshape(SMALL_ROWS, LANES)
    grads, slots = _finish_grads(halves, small, "finish_grads")
    tot = _sum_slots(slots, "sum_small").reshape(-1)
    n_rb, n_lb = d_rb.size, d_lb.size
    grads["rel_bias"] = tot[:n_rb].reshape(d_rb.shape)
    grads["lb_logits"] = tot[n_rb:n_rb + n_lb].reshape(d_lb.shape)
    grads["hg_norm"] = tot[n_rb + n_lb:n_rb + 2 * n_lb].reshape(d_hg.shape)
    g_full = tot[n_rb + 2 * n_lb:n_rb + 2 * n_lb + d_gains.size].reshape(d_gains.shape)
    cs = D_MODEL // N_CHIPS
    grads["norm_gains"] = lax.dynamic_slice_in_dim(g_full, me * cs, cs, axis=2)

    delta, new_m, new_v = {}, {}, {}
    for n in OUT_ORDER:
        delta[n], new_m[n], new_v[n] = _adamw(w[n], grads[n], m[n], v[n], f"adamw_{n}")
    loss = lax.psum(loss, ("x", "y", "c"))
    return (loss, dx[None], *[grads[n] for n in OUT_ORDER], *[delta[n] for n in OUT_ORDER],
            *[new_m[n] for n in OUT_ORDER], *[new_v[n] for n in OUT_ORDER])
```

```python
import functools
import math

import numpy as np
import jax
import jax.numpy as jnp
from jax import lax
from jax.experimental import pallas as pl
from jax.experimental.pallas import tpu as pltpu

F32 = jnp.float32
BF16 = jnp.bfloat16
MESH_ID = pl.DeviceIdType.MESH

D_MODEL = 2048
HEAD_DIM = 128
N_HEADS = 8
ATTN_W = 1024
HG_W = 1024
HG_CHUNK = 64
Q_BLOCK = 128
DILATIONS = (16, 4, 1)
REL_BUCKETS = 32
REL_MAX_DIST = 2048
CROSS_HEADS = 4
CROSS_DIM = 512
D_FF = 5632
RMS_EPS = 1e-6
NEG_INF = -1e30
N_NORMS = 7

ADAM_LR = 0.001
ADAM_B1 = 0.9
ADAM_B2 = 0.999
ADAM_EPS = 1e-08
ADAM_WD = 0.01
ADAM_STEP = 10

VMEM_LIMIT_V7X = 56 * 1024 * 1024
LANES = 128
N_CHIPS = 4


def _cparams(sem=None):
    if sem is None:
        return pltpu.CompilerParams(vmem_limit_bytes=VMEM_LIMIT_V7X)
    return pltpu.CompilerParams(dimension_semantics=sem, vmem_limit_bytes=VMEM_LIMIT_V7X)


def _blk(n, prefs):
    for p in prefs:
        if p <= n and n % p == 0:
            return p
    return n


def _dot(a, b, dims):
    return lax.dot_general(a.astype(BF16), b.astype(BF16), (dims, ((), ())), preferred_element_type=F32)


def _dot_nn(a, b):
    return _dot(a, b, ((1,), (0,)))


def _dot_nt(a, b):
    return _dot(a, b, ((1,), (1,)))


def _dot_tn(a, b):
    return _dot(a, b, ((0,), (0,)))


def _sigmoid(x):
    return 1.0 / (1.0 + jnp.exp(-x))


def _mm(a_list, b_list, mode, out_dtype, name):
    if not isinstance(a_list, (list, tuple)):
        a_list = [a_list]
    if not isinstance(b_list, (list, tuple)):
        b_list = [b_list]
    na, nb = len(a_list), len(b_list)
    big = (1024, 512, 256, 128)
    if mode == "nn":
        M, K = a_list[0].shape
        N = b_list[0].shape[1]
    elif mode == "nt":
        M = a_list[0].shape[0]
        K = sum(a.shape[1] for a in a_list)
        N = b_list[0].shape[0]
    else:
        K, M = a_list[0].shape
        N = sum(b.shape[1] for b in b_list)
    bm = _blk(M, big)
    if mode == "nt":
        bk = _blk(math.gcd(*[a.shape[1] for a in a_list]), (1024, 1408, 512, 256, 128))
    else:
        bk = _blk(K, (1024, 1408, 512, 256, 128))
    if mode == "tn":
        bn = _blk(math.gcd(*[b.shape[1] for b in b_list]), big)
    else:
        bn = _blk(N, big)
    nk = K // bk
    grid = (M // bm, N // bn, nk)

    if mode == "nn":
        a_specs = [pl.BlockSpec((bm, bk), lambda i, j, k: (i, k))]
        b_specs = [pl.BlockSpec((bk, bn), lambda i, j, k: (k, j))]
        dims = ((1,), (0,))
    elif mode == "nt":
        a_specs, off = [], 0
        for a in a_list:
            n_i = a.shape[1] // bk
            a_specs.append(pl.BlockSpec((bm, bk), functools.partial(
                lambda i, j, k, off, n_i: (i, jnp.clip(k - off, 0, n_i - 1)), off=off, n_i=n_i)))
            off += n_i
        b_specs = [pl.BlockSpec((bn, bk), lambda i, j, k: (j, k))]
        dims = ((1,), (1,))
    else:
        a_specs = [pl.BlockSpec((bk, bm), lambda i, j, k: (k, i))]
        b_specs, off = [], 0
        for b in b_list:
            n_j = b.shape[1] // bn
            b_specs.append(pl.BlockSpec((bk, bn), functools.partial(
                lambda i, j, k, off, n_j: (jnp.where((j >= off) & (j < off + n_j), k, 0),
                                           jnp.clip(j - off, 0, n_j - 1)), off=off, n_j=n_j)))
            off += n_j
        dims = ((0,), (0,))
    a_bounds = np.cumsum([0] + [a.shape[1] // bk for a in a_list]) if mode == "nt" else None
    b_bounds = np.cumsum([0] + [b.shape[1] // bn for b in b_list]) if mode == "tn" else None

    def body(*refs):
        a_refs = refs[:na]
        b_refs = refs[na:na + nb]
        o_ref = refs[na + nb]
        acc_ref = refs[na + nb + 1] if nk > 1 else None
        j = pl.program_id(1)
        k = pl.program_id(2)

        def accumulate(p):
            if nk == 1:
                o_ref[...] = p.astype(out_dtype)
                return

            @pl.when(k == 0)
            def _():
                acc_ref[...] = p

            @pl.when(k > 0)
            def _():
                acc_ref[...] += p

        if na > 1:
            for t in range(na):
                @pl.when((k >= int(a_bounds[t])) & (k < int(a_bounds[t + 1])))
                def _(t=t):
                    accumulate(_dot(a_refs[t][...], b_refs[0][...], dims))
        elif nb > 1:
            for t in range(nb):
                @pl.when((j >= int(b_bounds[t])) & (j < int(b_bounds[t + 1])))
                def _(t=t):
                    accumulate(_dot(a_refs[0][...], b_refs[t][...], dims))
        else:
            accumulate(_dot(a_refs[0][...], b_refs[0][...], dims))

        if nk > 1:
            @pl.when(k == nk - 1)
            def _():
                o_ref[...] = acc_ref[...].astype(out_dtype)

    return pl.pallas_call(
        body, name=name, grid=grid,
        in_specs=a_specs + b_specs,
        out_specs=pl.BlockSpec((bm, bn), lambda i, j, k: (i, j)),
        out_shape=jax.ShapeDtypeStruct((M, N), out_dtype),
        scratch_shapes=[pltpu.VMEM((bm, bn), F32)] if nk > 1 else [],
        compiler_params=_cparams(("parallel", "parallel", "arbitrary")),
    )(*a_list, *b_list)


def _rms_fwd(x, g, name):
    S, D = x.shape
    tm = _blk(S, (256, 128, 64, 8))

    def body(x_ref, g_ref, o_ref):
        xv = x_ref[...]
        r = lax.rsqrt(jnp.mean(xv * xv, axis=-1, keepdims=True) + RMS_EPS)
        o_ref[...] = ((xv * r) * g_ref[...]).astype(o_ref.dtype)

    return pl.pallas_call(
        body, name=name, grid=(S // tm,),
        in_specs=[pl.BlockSpec((tm, D), lambda i: (i, 0)), pl.BlockSpec((1, D), lambda i: (0, 0))],
        out_specs=pl.BlockSpec((tm, D), lambda i: (i, 0)),
        out_shape=jax.ShapeDtypeStruct((S, D), BF16),
        compiler_params=_cparams(("parallel",)),
    )(x, g)


def _norm_residual(x, t, g, name):
    S, D = x.shape
    tm = _blk(S, (256, 128, 64, 8))

    def body(x_ref, t_ref, g_ref, o_ref):
        tv = t_ref[...]
        r = lax.rsqrt(jnp.mean(tv * tv, axis=-1, keepdims=True) + RMS_EPS)
        o_ref[...] = x_ref[...] + (tv * r) * g_ref[...]

    return pl.pallas_call(
        body, name=name, grid=(S // tm,),
        in_specs=[pl.BlockSpec((tm, D), lambda i: (i, 0)), pl.BlockSpec((tm, D), lambda i: (i, 0)),
                  pl.BlockSpec((1, D), lambda i: (0, 0))],
        out_specs=pl.BlockSpec((tm, D), lambda i: (i, 0)),
        out_shape=jax.ShapeDtypeStruct((S, D), F32),
        compiler_params=_cparams(("parallel",)),
    )(x, t, g)


def _rms_bwd(x, g, dh, res, out_dtype, name):
    S, D = x.shape
    tm = _blk(S, (256, 128, 64, 8))
    has_res = res is not None

    def body(*refs):
        if has_res:
            x_ref, g_ref, dh_ref, res_ref, dx_ref, dg_ref = refs
        else:
            x_ref, g_ref, dh_ref, dx_ref, dg_ref = refs
        xv = x_ref[...]
        r = lax.rsqrt(jnp.mean(xv * xv, axis=-1, keepdims=True) + RMS_EPS)
        xh = xv * r
        dhv = dh_ref[...].astype(F32)
        gd = dhv * g_ref[...]
        dx = r * (gd - xh * jnp.mean(gd * xh, axis=-1, keepdims=True))
        if has_res:
            dx = dx + res_ref[...]
        dx_ref[...] = dx.astype(out_dtype)
        part = jnp.sum(dhv * xh, axis=0, keepdims=True)

        @pl.when(pl.program_id(0) == 0)
        def _():
            dg_ref[...] = part

        @pl.when(pl.program_id(0) > 0)
        def _():
            dg_ref[...] += part

    row = pl.BlockSpec((tm, D), lambda i: (i, 0))
    vec = pl.BlockSpec((1, D), lambda i: (0, 0))
    ins = [x, g, dh] + ([res] if has_res else [])
    return pl.pallas_call(
        body, name=name, grid=(S // tm,),
        in_specs=[row, vec, row] + ([row] if has_res else []),
        out_specs=[row, vec],
        out_shape=[jax.ShapeDtypeStruct((S, D), out_dtype), jax.ShapeDtypeStruct((1, D), F32)],
        compiler_params=_cparams(("arbitrary",)),
    )(*ins)


def _rel_bucket(dist):
    max_exact = REL_BUCKETS // 2
    d_f = jnp.maximum(dist, 1).astype(jnp.float32)
    large = max_exact + (jnp.log(d_f / max_exact) / math.log(REL_MAX_DIST / max_exact)
                         * (REL_BUCKETS - max_exact)).astype(jnp.int32)
    large = jnp.minimum(large, REL_BUCKETS - 1)
    return jnp.where(dist < max_exact, dist, large)


def _bias_tables(rel_bias):
    qi = jnp.arange(Q_BLOCK)[:, None]
    kj = jnp.arange(2 * Q_BLOCK)[None, :]
    m = qi - kj + Q_BLOCK
    band_ok = (m >= 0) & (m <= Q_BLOCK)
    tabs = []
    for d in DILATIONS:
        bucket = _rel_bucket(jnp.maximum(m, 0) * d)
        onehot = (bucket[:, :, None] == jnp.arange(REL_BUCKETS)[None, None, :]).astype(F32)
        bias = jnp.einsum("qkb,bh->hqk", onehot, rel_bias.astype(F32), precision=lax.Precision.HIGHEST)
        tabs.append(jnp.where(band_ok[None], bias, NEG_INF))
    return jnp.stack(tabs)


HEADS_PER_STEP = {16: 1, 4: 1, 1: 8}


def _rows(r, d):
    return pl.ds(r, Q_BLOCK, stride=d) if d > 1 else pl.ds(0, Q_BLOCK)


def _attn_fwd(proj, bias, state, d, last, cat_in, name):
    S = proj.shape[0]
    R = Q_BLOCK * d
    nsb = S // R
    first = state is None
    scale = HEAD_DIM ** -0.5
    hb = HEADS_PER_STEP[d]
    W = hb * HEAD_DIM

    def body(*refs):
        q_ref, kp_ref, kc_ref, vp_ref, vc_ref, b_ref = refs[:6]
        pos = 6
        if not first:
            m_in, l_in, a_in = refs[pos:pos + 3]
            pos += 3
        if last:
            pos += 1
            o_ref, lse_ref, o_tmp = refs[pos:pos + 3]
        else:
            m_out, l_out, a_out = refs[pos:pos + 3]
        n = pl.program_id(0)
        for hh in range(hb):
            ls = slice(hh * HEAD_DIM, (hh + 1) * HEAD_DIM)
            bp = b_ref[hh, :, :Q_BLOCK]
            bc = b_ref[hh, :, Q_BLOCK:]
            for r in range(d):
                rows = _rows(r, d)
                q = q_ref[rows, ls]
                sp = _dot_nt(q, kp_ref[rows, ls]) * scale + bp
                sp = jnp.where(n == 0, NEG_INF, sp)
                sc = _dot_nt(q, kc_ref[rows, ls]) * scale + bc
                mrow = jnp.maximum(jnp.max(sp, axis=-1, keepdims=True), jnp.max(sc, axis=-1, keepdims=True))
                if first:
                    m_new = mrow
                else:
                    m_old = m_in[rows, ls][:, :1]
                    m_new = jnp.maximum(m_old, mrow)
                pp = jnp.exp(sp - m_new)
                pc = jnp.exp(sc - m_new)
                lrow = jnp.sum(pp, axis=-1, keepdims=True) + jnp.sum(pc, axis=-1, keepdims=True)
                pv = _dot_nn(pp, vp_ref[rows, ls]) + _dot_nn(pc, vc_ref[rows, ls])
                if first:
                    l_new, a_new = lrow, pv
                else:
                    alpha = jnp.exp(m_old - m_new)
                    l_new = alpha * l_in[rows, ls][:, :1] + lrow
                    a_new = alpha * a_in[rows, ls] + pv
                if last:
                    o_tmp[rows, ls] = a_new / l_new
                    lse_ref[rows, ls] = jnp.broadcast_to(m_new + jnp.log(l_new), (Q_BLOCK, HEAD_DIM))
                else:
                    m_out[rows, ls] = jnp.broadcast_to(m_new, (Q_BLOCK, HEAD_DIM))
                    l_out[rows, ls] = jnp.broadcast_to(l_new, (Q_BLOCK, HEAD_DIM))
                    a_out[rows, ls] = a_new
        if last:
            o_ref[...] = o_tmp[...].astype(BF16)

    ng = N_HEADS // hb

    def col(g0):
        return pl.BlockSpec((R, W), lambda n, h: (n, g0 * ng + h))

    def col_prev(g0):
        return pl.BlockSpec((R, W), lambda n, h: (jnp.maximum(n - 1, 0), g0 * ng + h))

    in_specs = [col(0), col_prev(1), col(1), col_prev(2), col(2),
                pl.BlockSpec((hb, Q_BLOCK, 2 * Q_BLOCK), lambda n, h: (h, 0, 0))]
    ins = [proj, proj, proj, proj, proj, bias]
    if not first:
        in_specs += [col(0)] * 3
        ins += list(state)
    st = jax.ShapeDtypeStruct((S, ATTN_W), F32)
    if last:
        in_specs.append(pl.BlockSpec(memory_space=pl.ANY))
        ins.append(cat_in)
        out_specs = [col(0), col(0)]
        out_shape = [jax.ShapeDtypeStruct(cat_in.shape, BF16), st]
        scratch = [pltpu.VMEM((R, W), F32)]
        aliases = {len(ins) - 1: 0}
    else:
        out_specs = [col(0)] * 3
        out_shape = [st, st, st]
        scratch = []
        aliases = {}
    return pl.pallas_call(
        body, name=name, grid=(nsb, ng), in_specs=in_specs, out_specs=out_specs,
        out_shape=out_shape, scratch_shapes=scratch, input_output_aliases=aliases,
        compiler_params=_cparams(("arbitrary", "arbitrary")),
    )(*ins)


def _attn_bwd(proj, bias, cat, d_cat, lse, acc, d, last, name):
    S = proj.shape[0]
    R = Q_BLOCK * d
    nsb = S // R
    first = acc is None
    scale = HEAD_DIM ** -0.5
    odt = BF16 if last else F32
    hb = HEADS_PER_STEP[d]
    W = hb * HEAD_DIM

    def body(*refs):
        q_ref, kp_ref, kc_ref, vp_ref, vc_ref, b_ref, o_ref, do_ref, lse_ref = refs[:9]
        pos = 9
        if not first:
            dq_in, dk_in, dv_in = refs[pos:pos + 3]
            pos += 3
        dq_out, dk_out, dv_out, db_out = refs[pos:pos + 4]
        pos += 4
        ck, cv, o_f, do_f, dq_t, dk_t, dv_t, db_acc = refs[pos:pos + 8]
        n = pl.program_id(1)

        @pl.when(n == 0)
        def _():
            db_acc[...] = jnp.zeros_like(db_acc)
            ck[...] = jnp.zeros_like(ck)
            cv[...] = jnp.zeros_like(cv)

        @pl.when(n < nsb)
        def _():
            o_f[...] = o_ref[...].astype(F32)
            do_f[...] = do_ref[...].astype(F32)
            for hh in range(hb):
                ls = slice(hh * HEAD_DIM, (hh + 1) * HEAD_DIM)
                bp = b_ref[hh, :, :Q_BLOCK]
                bc = b_ref[hh, :, Q_BLOCK:]
                for r in range(d):
                    rows = _rows(r, d)
                    q = q_ref[rows, ls]
                    kp = kp_ref[rows, ls]
                    kc = kc_ref[rows, ls]
                    vp = vp_ref[rows, ls]
                    vc = vc_ref[rows, ls]
                    do = do_f[rows, ls]
                    lse_r = lse_ref[rows, ls][:, :1]
                    sp = _dot_nt(q, kp) * scale + bp
                    sp = jnp.where(n == 0, NEG_INF, sp)
                    sc = _dot_nt(q, kc) * scale + bc
                    pp = jnp.exp(sp - lse_r)
                    pc = jnp.exp(sc - lse_r)
                    dd = jnp.sum(do * o_f[rows, ls], axis=-1, keepdims=True)
                    dsp = pp * (_dot_nt(do, vp) - dd)
                    dsc = pc * (_dot_nt(do, vc) - dd)
                    db_acc[hh, :, :Q_BLOCK] += dsp
                    db_acc[hh, :, Q_BLOCK:] += dsc
                    dq = (_dot_nn(dsp, kp) + _dot_nn(dsc, kc)) * scale
                    dk_prev = ck[rows, ls] + _dot_tn(dsp, q) * scale
                    dv_prev = cv[rows, ls] + _dot_tn(pp, do)
                    ck[rows, ls] = _dot_tn(dsc, q) * scale
                    cv[rows, ls] = _dot_tn(pc, do)
                    if not first:
                        dq = dq + dq_in[rows, ls]
                        dk_prev = dk_prev + dk_in[rows, ls]
                        dv_prev = dv_prev + dv_in[rows, ls]
                    dq_t[rows, ls] = dq
                    dk_t[rows, ls] = dk_prev
                    dv_t[rows, ls] = dv_prev
            dq_out[...] = dq_t[...].astype(odt)

            @pl.when(n > 0)
            def _():
                dk_out[...] = dk_t[...].astype(odt)
                dv_out[...] = dv_t[...].astype(odt)

        @pl.when(n == nsb)
        def _():
            if first:
                dk_out[...] = ck[...].astype(odt)
                dv_out[...] = cv[...].astype(odt)
            else:
                dk_out[...] = (ck[...] + dk_in[...]).astype(odt)
                dv_out[...] = (cv[...] + dv_in[...]).astype(odt)
            db_out[...] = db_acc[...]

    last_n = nsb - 1
    ng = N_HEADS // hb

    def cur(g0):
        return pl.BlockSpec((R, W), lambda h, n: (jnp.minimum(n, last_n), g0 * ng + h))

    def prev(g0):
        return pl.BlockSpec((R, W), lambda h, n: (jnp.maximum(jnp.minimum(n, last_n) - 1, 0), g0 * ng + h))

    delayed = pl.BlockSpec((R, W), lambda h, n: (jnp.maximum(n - 1, 0), h))
    in_specs = [cur(0), prev(1), cur(1), prev(2), cur(2),
                pl.BlockSpec((hb, Q_BLOCK, 2 * Q_BLOCK), lambda h, n: (h, 0, 0)),
                cur(0), cur(0), cur(0)]
    ins = [proj, proj, proj, proj, proj, bias, cat, d_cat, lse]
    if not first:
        in_specs += [cur(0), delayed, delayed]
        ins += list(acc)
    st = jax.ShapeDtypeStruct((S, ATTN_W), odt)
    tile = pltpu.VMEM((R, W), F32)
    return pl.pallas_call(
        body, name=name, grid=(ng, nsb + 1), in_specs=in_specs,
        out_specs=[cur(0), delayed, delayed,
                   pl.BlockSpec((hb, Q_BLOCK, 2 * Q_BLOCK), lambda h, n: (h, 0, 0))],
        out_shape=[st, st, st, jax.ShapeDtypeStruct((N_HEADS, Q_BLOCK, 2 * Q_BLOCK), F32)],
        scratch_shapes=[tile] * 7 + [pltpu.VMEM((hb, Q_BLOCK, 2 * Q_BLOCK), F32)],
        compiler_params=_cparams(("arbitrary", "arbitrary")),
    )(*ins)


HG_LEVELS = (32, 16, 8, 4, 2, 1)
N_LEV = len(HG_LEVELS)


def _hg_consts():
    C = HG_CHUNK
    t = np.arange(C)
    mq, mk, masks = [], [], []
    for B in HG_LEVELS:
        up = (t // B) % 2 == 1
        bs = (t // B) * B
        be = bs + B - 1
        mq.append(up[:, None] & (t[None, :] >= bs[:, None]) & (t[None, :] <= t[:, None]))
        mk.append((~up)[:, None] & (t[None, :] > t[:, None]) & (t[None, :] <= be[:, None]))
        masks.append(up[:, None] & (~up)[None, :] & ((t[:, None] // (2 * B)) == (t[None, :] // (2 * B))))
    masks.append(np.eye(C, dtype=bool))
    mb = t[None, :] <= t[:, None]
    mw = t[None, :] > t[:, None]
    m_all = np.concatenate(mq + mk + [mb, mw], axis=0).astype(np.float32)
    return jnp.asarray(m_all, BF16), jnp.asarray(np.stack(masks).astype(np.float32))


def _split3(v):
    hi = v.astype(BF16)
    r1 = v - hi.astype(F32)
    mid = r1.astype(BF16)
    lo = (r1 - mid.astype(F32)).astype(BF16)
    return jnp.concatenate([hi, mid, lo], axis=1)


def _hg_chunk_fwd(fz, iv, qz, lbh, m_all, mask_ref, st_t):
    C = HG_CHUNK
    sig = _sigmoid(fz)
    f = lbh + (1.0 - lbh) * sig
    lf = jnp.log(f)
    kk = 1.0 - f
    sq = _sigmoid(qz)
    qq = qz * sq
    a3 = lax.dot_general(m_all, _split3(lf), (((1,), (0,)), ((), ())), preferred_element_type=F32)
    args = a3[:, :HEAD_DIM] + a3[:, HEAD_DIM:2 * HEAD_DIM] + a3[:, 2 * HEAD_DIM:]
    e = jnp.exp(args)
    qs = [qq * e[j * C:(j + 1) * C] for j in range(N_LEV)]
    ks = [kk * e[(N_LEV + j) * C:(N_LEV + j + 1) * C] for j in range(N_LEV)]
    a = mask_ref[N_LEV] * _dot_nt(qq, kk)
    for j in range(N_LEV):
        a = a + mask_ref[j] * _dot_nt(qs[j], ks[j])
    eb = e[2 * N_LEV * C:(2 * N_LEV + 1) * C]
    ew = e[(2 * N_LEV + 1) * C:]
    qe = qq * eb
    w = kk * ew
    o = _dot_nt(qe, st_t) + _dot_nn(a, iv)
    eb_last = eb[C - 1:C, :]
    new_st = st_t * eb_last + _dot_tn(iv, w)
    return dict(sig=sig, f=f, kk=kk, sq=sq, qq=qq, e=e, qs=qs, ks=ks, a=a, eb=eb, ew=ew, qe=qe, w=w, o=o,
                eb_last=eb_last, new_st=new_st)


def _hg_rows(S):
    return _blk(S, (256, 128, 64))


def _hgrn_fwd(proj, lb, gain, cat_in, name):
    S = proj.shape[0]
    rows = _hg_rows(S)
    cb = rows // HG_CHUNK
    m_all, masks = _hg_consts()

    def body(fz_ref, iv_ref, qz_ref, gz_ref, lb_ref, gain_ref, m_ref, mask_ref, cat_any, o_ref, st_ref, st_scr):
        @pl.when(pl.program_id(0) == 0)
        def _():
            st_scr[...] = jnp.zeros_like(st_scr)

        m_all_v = m_ref[...]

        def chunk(c, carry):
            rs = pl.ds(pl.multiple_of(c * HG_CHUNK, HG_CHUNK), HG_CHUNK)
            for h in range(N_HEADS):
                sl = slice(h * HEAD_DIM, (h + 1) * HEAD_DIM)
                st_t = st_scr[h]
                st_ref[c, h] = st_t
                gz = gz_ref[rs, sl]
                iv = iv_ref[rs, sl]
                q = _hg_chunk_fwd(fz_ref[rs, sl], iv, qz_ref[rs, sl], lb_ref[:, sl], m_all_v, mask_ref, st_t)
                st_scr[h] = q["new_st"]
                o = q["o"]
                r = lax.rsqrt(jnp.mean(o * o, axis=-1, keepdims=True) + RMS_EPS)
                y = ((o * r) * gain_ref[:, sl]) * (gz * _sigmoid(gz))
                o_ref[rs, sl] = y.astype(BF16)
            return carry

        lax.fori_loop(0, cb, chunk, 0)

    def col(c):
        return pl.BlockSpec((rows, HG_W), lambda i: (i, c))

    vec = pl.BlockSpec((1, HG_W), lambda i: (0, 0))
    return pl.pallas_call(
        body, name=name, grid=(S // rows,),
        in_specs=[col(3), col(4), col(5), col(6), vec, vec,
                  pl.BlockSpec(m_all.shape, lambda i: (0, 0)), pl.BlockSpec(masks.shape, lambda i: (0, 0, 0)),
                  pl.BlockSpec(memory_space=pl.ANY)],
        out_specs=[col(1), pl.BlockSpec((cb, N_HEADS, HEAD_DIM, HEAD_DIM), lambda i: (i, 0, 0, 0))],
        out_shape=[jax.ShapeDtypeStruct(cat_in.shape, BF16),
                   jax.ShapeDtypeStruct((S // HG_CHUNK, N_HEADS, HEAD_DIM, HEAD_DIM), F32)],
        scratch_shapes=[pltpu.VMEM((N_HEADS, HEAD_DIM, HEAD_DIM), F32)],
        input_output_aliases={8: 0},
        compiler_params=_cparams(("arbitrary",)),
    )(proj, proj, proj, proj, lb, gain, m_all, masks, cat_in)


def _hgrn_bwd(proj, lb, gain, states, d_cat, name):
    S = proj.shape[0]
    rows = _hg_rows(S)
    cb = rows // HG_CHUNK
    nblk = S // rows
    C = HG_CHUNK
    m_all, masks = _hg_consts()

    def body(fz_ref, iv_ref, qz_ref, gz_ref, lb_ref, gain_ref, m_ref, mask_ref, st_ref, dy_ref,
             dz_ref, dlb_ref, dgain_ref, dst_scr):
        @pl.when(pl.program_id(0) == 0)
        def _():
            dst_scr[...] = jnp.zeros_like(dst_scr)
            dlb_ref[...] = jnp.zeros_like(dlb_ref)
            dgain_ref[...] = jnp.zeros_like(dgain_ref)

        m_all_v = m_ref[...]
        last_row = lax.broadcasted_iota(jnp.int32, (C, HEAD_DIM), 0) == C - 1

        def chunk(ci, carry):
            c = cb - 1 - ci
            rs = pl.ds(pl.multiple_of(c * C, C), C)
            for h in range(N_HEADS):
                sl = slice(h * HEAD_DIM, (h + 1) * HEAD_DIM)
                lbh = lb_ref[:, sl]
                gh = gain_ref[:, sl]
                st_t = st_ref[c, h]
                fz = fz_ref[rs, sl]
                iv = iv_ref[rs, sl]
                qz = qz_ref[rs, sl]
                gz = gz_ref[rs, sl]
                q = _hg_chunk_fwd(fz, iv, qz, lbh, m_all_v, mask_ref, st_t)
                o = q["o"]
                r = lax.rsqrt(jnp.mean(o * o, axis=-1, keepdims=True) + RMS_EPS)
                on = o * r
                sg = _sigmoid(gz)
                gate = gz * sg
                dy = dy_ref[rs, sl].astype(F32)
                dgain_ref[:, sl] += jnp.sum(dy * on * gate, axis=0, keepdims=True)
                dgz = (dy * on * gh) * (sg * (1.0 + gz * (1.0 - sg)))
                don = dy * gh * gate
                do = r * (don - on * jnp.mean(don * on, axis=-1, keepdims=True))
                da = _dot_nt(do, iv)
                dv = _dot_tn(q["a"], do)
                dqe = _dot_nn(do, st_t)
                dst_new = _dot_tn(do, q["qe"])
                dsp = dst_scr[h]
                dw = _dot_nn(iv, dsp)
                dv = dv + _dot_nt(q["w"], dsp)
                d_eb_last = jnp.sum(dsp * st_t, axis=0, keepdims=True)
                dst_scr[h] = dsp * q["eb_last"] + dst_new
                dad = da * mask_ref[N_LEV]
                dq = _dot_nn(dad, q["kk"])
                dk = _dot_tn(dad, q["qq"])
                dargs_q = []
                dargs_k = []
                for j in range(N_LEV):
                    daj = da * mask_ref[j]
                    dqj = _dot_nn(daj, q["ks"][j])
                    dkj = _dot_tn(daj, q["qs"][j])
                    dq = dq + dqj * q["e"][j * C:(j + 1) * C]
                    dk = dk + dkj * q["e"][(N_LEV + j) * C:(N_LEV + j + 1) * C]
                    dargs_q.append(dqj * q["qs"][j])
                    dargs_k.append(dkj * q["ks"][j])
                dq = dq + dqe * q["eb"]
                darg_b = dqe * q["qe"] + jnp.where(last_row, d_eb_last * q["eb_last"], 0.0)
                dk = dk + dw * q["ew"]
                darg_w = dw * q["w"]
                dall = jnp.concatenate(dargs_q + dargs_k + [darg_b, darg_w], axis=0)
                hi = dall.astype(BF16)
                lo = (dall - hi.astype(F32)).astype(BF16)
                dl2 = lax.dot_general(m_all_v, jnp.concatenate([hi, lo], axis=1), (((0,), (0,)), ((), ())),
                                      preferred_element_type=F32)
                dlf = dl2[:, :HEAD_DIM] + dl2[:, HEAD_DIM:]
                df = dlf / q["f"] - dk
                sig = q["sig"]
                dlb_ref[:, sl] += jnp.sum(df * (1.0 - sig), axis=0, keepdims=True)
                dfz = df * (1.0 - lbh) * (sig * (1.0 - sig))
                sq = q["sq"]
                dqz = dq * (sq * (1.0 + qz * (1.0 - sq)))
                dz_ref[rs, h * HEAD_DIM:(h + 1) * HEAD_DIM] = dfz.astype(BF16)
                dz_ref[rs, HG_W + h * HEAD_DIM:HG_W + (h + 1) * HEAD_DIM] = dv.astype(BF16)
                dz_ref[rs, 2 * HG_W + h * HEAD_DIM:2 * HG_W + (h + 1) * HEAD_DIM] = dqz.astype(BF16)
                dz_ref[rs, 3 * HG_W + h * HEAD_DIM:3 * HG_W + (h + 1) * HEAD_DIM] = dgz.astype(BF16)
            return carry

        lax.fori_loop(0, cb, chunk, 0)

    def col(c):
        return pl.BlockSpec((rows, HG_W), lambda i: (nblk - 1 - i, c))

    vec = pl.BlockSpec((1, HG_W), lambda i: (0, 0))
    return pl.pallas_call(
        body, name=name, grid=(nblk,),
        in_specs=[col(3), col(4), col(5), col(6), vec, vec,
                  pl.BlockSpec(m_all.shape, lambda i: (0, 0)), pl.BlockSpec(masks.shape, lambda i: (0, 0, 0)),
                  pl.BlockSpec((cb, N_HEADS, HEAD_DIM, HEAD_DIM), lambda i: (nblk - 1 - i, 0, 0, 0)),
                  col(1)],
        out_specs=[pl.BlockSpec((rows, 4 * HG_W), lambda i: (nblk - 1 - i, 0)), vec, vec],
        out_shape=[jax.ShapeDtypeStruct((S, 4 * HG_W), BF16), jax.ShapeDtypeStruct((1, HG_W), F32),
                   jax.ShapeDtypeStruct((1, HG_W), F32)],
        scratch_shapes=[pltpu.VMEM((N_HEADS, HEAD_DIM, HEAD_DIM), F32)],
        compiler_params=_cparams(("arbitrary",)),
    )(proj, proj, proj, proj, lb, gain, m_all, masks, states, d_cat)


def _cross_fwd(cq, ckv, name):
    S = cq.shape[0]
    n_mem = ckv.shape[0]
    tq = _blk(S, (256, 128))
    scale = CROSS_DIM ** -0.5

    def body(q_ref, kv_ref, o_ref):
        for h in range(CROSS_HEADS):
            sl = slice(h * CROSS_DIM, (h + 1) * CROSS_DIM)
            k = kv_ref[:, sl]
            v = kv_ref[:, D_MODEL + h * CROSS_DIM:D_MODEL + (h + 1) * CROSS_DIM]
            s = _dot_nt(q_ref[:, sl], k) * scale
            m = jnp.max(s, axis=-1, keepdims=True)
            p = jnp.exp(s - m)
            p = p / jnp.sum(p, axis=-1, keepdims=True)
            o_ref[:, sl] = _dot_nn(p, v).astype(BF16)

    return pl.pallas_call(
        body, name=name, grid=(S // tq,),
        in_specs=[pl.BlockSpec((tq, D_MODEL), lambda i: (i, 0)), pl.BlockSpec((n_mem, 2 * D_MODEL), lambda i: (0, 0))],
        out_specs=pl.BlockSpec((tq, D_MODEL), lambda i: (i, 0)),
        out_shape=jax.ShapeDtypeStruct((S, D_MODEL), BF16),
        compiler_params=_cparams(("parallel",)),
    )(cq, ckv)


def _cross_bwd(cq, ckv, d_o, name):
    S = cq.shape[0]
    n_mem = ckv.shape[0]
    tq = _blk(S, (256, 128))
    scale = CROSS_DIM ** -0.5

    def body(q_ref, kv_ref, do_ref, dq_ref, dkv_ref):
        @pl.when(pl.program_id(0) == 0)
        def _():
            dkv_ref[...] = jnp.zeros_like(dkv_ref)

        for h in range(CROSS_HEADS):
            sl = slice(h * CROSS_DIM, (h + 1) * CROSS_DIM)
            slv = slice(D_MODEL + h * CROSS_DIM, D_MODEL + (h + 1) * CROSS_DIM)
            q = q_ref[:, sl]
            k = kv_ref[:, sl]
            v = kv_ref[:, slv]
            do = do_ref[:, sl]
            s = _dot_nt(q, k) * scale
            m = jnp.max(s, axis=-1, keepdims=True)
            p = jnp.exp(s - m)
            p = p / jnp.sum(p, axis=-1, keepdims=True)
            dp = _dot_nt(do, v)
            ds = p * (dp - jnp.sum(dp * p, axis=-1, keepdims=True)) * scale
            dq_ref[:, sl] = _dot_nn(ds, k).astype(BF16)
            dkv_ref[:, sl] += _dot_tn(ds, q)
            dkv_ref[:, slv] += _dot_tn(p, do)

    row = pl.BlockSpec((tq, D_MODEL), lambda i: (i, 0))
    kv = pl.BlockSpec((n_mem, 2 * D_MODEL), lambda i: (0, 0))
    return pl.pallas_call(
        body, name=name, grid=(S // tq,),
        in_specs=[row, kv, row], out_specs=[row, kv],
        out_shape=[jax.ShapeDtypeStruct((S, D_MODEL), BF16), jax.ShapeDtypeStruct((n_mem, 2 * D_MODEL), F32)],
        compiler_params=_cparams(("arbitrary",)),
    )(cq, ckv, d_o)


FF_BLOCK = 1408


def _swiglu_fwd(gu, name):
    S = gu.shape[0]
    tm = _blk(S, (256, 128))
    nj = D_FF // FF_BLOCK

    def body(g_ref, u_ref, o_ref):
        g = g_ref[...].astype(F32)
        o_ref[...] = ((g * _sigmoid(g)) * u_ref[...].astype(F32)).astype(BF16)

    return pl.pallas_call(
        body, name=name, grid=(S // tm, nj),
        in_specs=[pl.BlockSpec((tm, FF_BLOCK), lambda i, j: (i, j)),
                  pl.BlockSpec((tm, FF_BLOCK), lambda i, j: (i, j + nj))],
        out_specs=pl.BlockSpec((tm, FF_BLOCK), lambda i, j: (i, j)),
        out_shape=jax.ShapeDtypeStruct((S, D_FF), BF16),
        compiler_params=_cparams(("parallel", "parallel")),
    )(gu, gu)


def _swiglu_bwd(gu, d_act, name):
    S = gu.shape[0]
    tm = _blk(S, (256, 128))
    nj = D_FF // FF_BLOCK

    def body(g_ref, u_ref, da_ref, o_ref):
        part = pl.program_id(2)
        g = g_ref[...].astype(F32)
        u = u_ref[...].astype(F32)
        da = da_ref[...].astype(F32)
        sg = _sigmoid(g)
        dgate = da * u * (sg * (1.0 + g * (1.0 - sg)))
        dup = da * (g * sg)
        o_ref[...] = jnp.where(part == 0, dgate, dup).astype(BF16)

    return pl.pallas_call(
        body, name=name, grid=(S // tm, nj, 2),
        in_specs=[pl.BlockSpec((tm, FF_BLOCK), lambda i, j, p: (i, j)),
                  pl.BlockSpec((tm, FF_BLOCK), lambda i, j, p: (i, j + nj)),
                  pl.BlockSpec((tm, FF_BLOCK), lambda i, j, p: (i, j))],
        out_specs=pl.BlockSpec((tm, FF_BLOCK), lambda i, j, p: (i, j + p * nj)),
        out_shape=jax.ShapeDtypeStruct((S, 2 * D_FF), BF16),
        compiler_params=_cparams(("parallel", "parallel", "arbitrary")),
    )(gu, gu, d_act)


def _loss_head(y, target, name):
    S, D = y.shape
    tm = _blk(S, (256, 128, 64, 8))

    def body(y_ref, t_ref, dy_ref, l_ref):
        diff = y_ref[...] - t_ref[...]
        dy_ref[...] = diff * (1.0 / D)
        sq = (diff * diff) * (0.5 / D)
        part = jnp.sum(sq.reshape(tm // 8, 8, D), axis=0)

        @pl.when(pl.program_id(0) == 0)
        def _():
            l_ref[...] = part

        @pl.when(pl.program_id(0) > 0)
        def _():
            l_ref[...] += part

    row = pl.BlockSpec((tm, D), lambda i: (i, 0))
    return pl.pallas_call(
        body, name=name, grid=(S // tm,), in_specs=[row, row],
        out_specs=[row, pl.BlockSpec((8, D), lambda i: (0, 0))],
        out_shape=[jax.ShapeDtypeStruct((S, D), F32), jax.ShapeDtypeStruct((8, D), F32)],
        compiler_params=_cparams(("arbitrary",)),
    )(y, target)


def _layer_fwd(x0, mem, g, w, lb, hg_gain, bias, tag):
    S = x0.shape[0]
    h0 = _rms_fwd(x0, g[0], f"rms0_{tag}")
    proj = _mm(h0, w["w_in"], "nn", F32, f"mm_in_{tag}")
    state = None
    cat = lax.empty((S, D_MODEL), BF16)
    for bi, d in enumerate(DILATIONS):
        last = bi == len(DILATIONS) - 1
        out = _attn_fwd(proj, bias[bi], state, d, last, cat if last else None, f"attn_fwd{d}_{tag}")
        if last:
            cat, lse = out
        else:
            state = tuple(out)
    cat, states = _hgrn_fwd(proj, lb, hg_gain, cat, f"hgrn_fwd_{tag}")
    mix = _mm(cat, w["w_out"], "nn", F32, f"mm_out_{tag}")
    x1 = _norm_residual(x0, mix, g[1], f"res1_{tag}")
    hc = _rms_fwd(x1, g[2], f"rms2_{tag}")
    cq = _mm(hc, w["w_cq"], "nn", BF16, f"mm_cq_{tag}")
    mn = _rms_fwd(mem, g[3], f"rms3_{tag}")
    ckv = _mm(mn, w["w_ckv"], "nn", BF16, f"mm_ckv_{tag}")
    cop = _cross_fwd(cq, ckv, f"cross_fwd_{tag}")
    co = _mm(cop, w["w_co"], "nn", F32, f"mm_co_{tag}")
    x2 = _norm_residual(x1, co, g[4], f"res2_{tag}")
    hf = _rms_fwd(x2, g[5], f"rms5_{tag}")
    gu = _mm(hf, w["w_gate_up"], "nn", BF16, f"mm_gu_{tag}")
    act = _swiglu_fwd(gu, f"swiglu_fwd_{tag}")
    y = _mm(act, w["w_down"], "nn", F32, f"mm_down_{tag}")
    x3 = _norm_residual(x2, y, g[6], f"res3_{tag}")
    saved = dict(x0=x0, h0=h0, proj=proj, cat=cat, lse=lse, states=states, mix=mix, x1=x1, hc=hc, cq=cq, mn=mn,
                 ckv=ckv, cop=cop, co=co, x2=x2, hf=hf, gu=gu, act=act, y=y)
    return x3, saved


def _layer_bwd(dx3, sv, mem, g, w, lb, hg_gain, bias, tag):
    dg = [None] * N_NORMS
    gw = {}
    dy, dg[6] = _rms_bwd(sv["y"], g[6], dx3, None, BF16, f"rmsb6_{tag}")
    d_act = _mm(dy, w["w_down"], "nt", BF16, f"mmb_down_x_{tag}")
    gw["w_down"] = _mm(sv["act"], dy, "tn", BF16, f"mmb_down_w_{tag}")
    dgu = _swiglu_bwd(sv["gu"], d_act, f"swiglu_bwd_{tag}")
    gw["w_gate_up"] = _mm(sv["hf"], dgu, "tn", BF16, f"mmb_gu_w_{tag}")
    d_hf = _mm(dgu, w["w_gate_up"], "nt", F32, f"mmb_gu_x_{tag}")
    dx2, dg[5] = _rms_bwd(sv["x2"], g[5], d_hf, dx3, F32, f"rmsb5_{tag}")
    d_co, dg[4] = _rms_bwd(sv["co"], g[4], dx2, None, BF16, f"rmsb4_{tag}")
    d_cop = _mm(d_co, w["w_co"], "nt", BF16, f"mmb_co_x_{tag}")
    gw["w_co"] = _mm(sv["cop"], d_co, "tn", BF16, f"mmb_co_w_{tag}")
    d_cq, d_ckv = _cross_bwd(sv["cq"], sv["ckv"], d_cop, f"cross_bwd_{tag}")
    gw["w_cq"] = _mm(sv["hc"], d_cq, "tn", BF16, f"mmb_cq_w_{tag}")
    d_hc = _mm(d_cq, w["w_cq"], "nt", F32, f"mmb_cq_x_{tag}")
    gw["w_ckv"] = _mm(sv["mn"], d_ckv, "tn", BF16, f"mmb_ckv_w_{tag}")
    d_mn = _mm(d_ckv, w["w_ckv"], "nt", F32, f"mmb_ckv_x_{tag}")
    _, dg[3] = _rms_bwd(mem, g[3], d_mn, None, BF16, f"rmsb3_{tag}")
    dx1, dg[2] = _rms_bwd(sv["x1"], g[2], d_hc, dx2, F32, f"rmsb2_{tag}")
    d_mix, dg[1] = _rms_bwd(sv["mix"], g[1], dx1, None, BF16, f"rmsb1_{tag}")
    d_cat = _mm(d_mix, w["w_out"], "nt", BF16, f"mmb_out_x_{tag}")
    gw["w_out"] = _mm(sv["cat"], d_mix, "tn", BF16, f"mmb_out_w_{tag}")
    acc = None
    dbias = []
    for bi, d in enumerate(DILATIONS):
        last = bi == len(DILATIONS) - 1
        dq, dk, dv, db = _attn_bwd(sv["proj"], bias[bi], sv["cat"], d_cat, sv["lse"], acc, d, last,
                                   f"attn_bwd{d}_{tag}")
        acc = (dq, dk, dv)
        dbias.append(db)
    d_hz, dlb, dgain = _hgrn_bwd(sv["proj"], lb, hg_gain, sv["states"], d_cat, f"hgrn_bwd_{tag}")
    parts = [acc[0], acc[1], acc[2], d_hz]
    gw["w_in"] = _mm(sv["h0"], parts, "tn", BF16, f"mmb_in_w_{tag}")
    d_h0 = _mm(parts, w["w_in"], "nt", F32, f"mmb_in_x_{tag}")
    dx0, dg[0] = _rms_bwd(sv["x0"], g[0], d_h0, dx1, F32, f"rmsb0_{tag}")
    return dx0, gw, jnp.stack(dg), dlb, dgain, jnp.stack(dbias)


def _lb_all(lb_logits):
    p = jax.nn.softmax(lb_logits.astype(F32), axis=0)
    return jnp.cumsum(p, axis=0) - p


def _local_step(x, mem, target, rel_bias, lb_logits, norm_gains, hg_norm, weights_of_layer, on_layer_grads=None):
    L = lb_logits.shape[0]
    bias, bias_vjp = jax.vjp(_bias_tables, rel_bias)
    lb_all, lb_vjp = jax.vjp(_lb_all, lb_logits)
    gains = norm_gains.reshape(L, N_NORMS, 1, D_MODEL)
    saved = []
    h = x
    ws = []
    for l in range(L):
        w = weights_of_layer(l)
        ws.append(w)
        h, sv = _layer_fwd(h, mem, gains[l], w, lb_all[l:l + 1], hg_norm[l:l + 1], bias, f"l{l}")
        saved.append(sv)
    dy, lparts = _loss_head(h, target, "loss_head")
    loss = jnp.sum(lparts)
    d_gains, d_lb, d_hg, gws = [None] * L, [None] * L, [None] * L, [None] * L
    d_bias = jnp.zeros_like(bias)
    dh = dy
    for l in reversed(range(L)):
        dh, gw, dgl, dlbl, dhgl, dbl = _layer_bwd(dh, saved[l], mem, gains[l], ws[l], lb_all[l:l + 1],
                                                  hg_norm[l:l + 1], bias, f"l{l}")
        d_gains[l], d_lb[l], d_hg[l], gws[l] = dgl.reshape(N_NORMS, D_MODEL), dlbl[0], dhgl[0], gw
        d_bias = d_bias + dbl
        if on_layer_grads is not None:
            gws[l] = on_layer_grads(l, gw)
    (d_rel_bias,) = bias_vjp(d_bias)
    (d_lb_logits,) = lb_vjp(jnp.stack(d_lb))
    return loss, dh, gws, d_rel_bias, d_lb_logits, jnp.stack(d_gains), jnp.stack(d_hg)


def kernel(x, mem, rel_bias, lb_logits, norm_gains, w_in, hg_norm, w_out, w_cq, w_ckv, w_co, w_gate_up, w_down, loss_target, m_rel_bias, m_lb_logits, m_norm_gains, m_w_in, m_hg_norm, m_w_out, m_w_cq, m_w_ckv, m_w_co, m_w_gate_up, m_w_down, v_rel_bias, v_lb_logits, v_norm_gains, v_w_in, v_hg_norm, v_w_out, v_w_cq, v_w_ckv, v_w_co, v_w_gate_up, v_w_down):
    return _train_step(
        x, mem, loss_target,
        dict(rel_bias=rel_bias, lb_logits=lb_logits, norm_gains=norm_gains, hg_norm=hg_norm, w_in=w_in, w_out=w_out,
             w_cq=w_cq, w_ckv=w_ckv, w_co=w_co, w_gate_up=w_gate_up, w_down=w_down),
        dict(rel_bias=m_rel_bias, lb_logits=m_lb_logits, norm_gains=m_norm_gains, hg_norm=m_hg_norm, w_in=m_w_in,
             w_out=m_w_out, w_cq=m_w_cq, w_ckv=m_w_ckv, w_co=m_w_co, w_gate_up=m_w_gate_up, w_down=m_w_down),
        dict(rel_bias=v_rel_bias, lb_logits=v_lb_logits, norm_gains=v_norm_gains, hg_norm=v_hg_norm, w_in=v_w_in,
             w_out=v_w_out, w_cq=v_w_cq, w_ckv=v_w_ckv, w_co=v_w_co, w_gate_up=v_w_gate_up, w_down=v_w_down))


BIG = ("w_in", "w_out", "w_cq", "w_ckv", "w_co", "w_gate_up", "w_down")
SHARD_AXIS = dict(w_in=1, w_out=0, w_cq=0, w_ckv=1, w_co=0, w_gate_up=1, w_down=0)
NB = len(BIG)
OUT_ORDER = ("rel_bias", "lb_logits", "norm_gains", "w_in", "hg_norm", "w_out", "w_cq", "w_ckv", "w_co",
             "w_gate_up", "w_down")
ANY = pl.BlockSpec(memory_space=pl.ANY)


def _place():
    x, y, c = lax.axis_index("x"), lax.axis_index("y"), lax.axis_index("c")
    chips = [(1 - x, y), (x, 1 - y), (1 - x, 1 - y)]
    return x, y, c, chips


def _remote(src, dst, send_sem, recv_sem, to):
    return pltpu.make_async_remote_copy(src_ref=src, dst_ref=dst, send_sem=send_sem, recv_sem=recv_sem,
                                        device_id=to, device_id_type=MESH_ID)


def _half_region(ref, axis, chip, half, lead=()):
    R, C = ref.shape[-2:]
    if axis == 0:
        rs = R // N_CHIPS
        return ref.at[(*lead, pl.ds(chip * rs + half * (rs // 2), rs // 2), slice(None))]
    cs = C // N_CHIPS
    return ref.at[(*lead, pl.ds(half * (R // 2), R // 2), pl.ds(chip * cs, cs))]


def _cast_place(n, w, l, where, name):
    _, rs, cs = w.shape
    tr = _blk(rs, (256, 128, 64, 32, 16))
    nt = rs // tr
    if SHARD_AXIS[n] == 0:
        full = (rs * N_CHIPS, cs)
        o_spec = pl.BlockSpec((tr, cs), lambda i, wh: (wh[0] * nt + i, 0))
    else:
        full = (rs, cs * N_CHIPS)
        o_spec = pl.BlockSpec((tr, cs), lambda i, wh: (i, wh[0]))

    def body(wh, w_ref, o_ref):
        o_ref[...] = w_ref[...].astype(BF16)

    return pl.pallas_call(
        body, name=name,
        grid_spec=pltpu.PrefetchScalarGridSpec(
            num_scalar_prefetch=1, grid=(nt,),
            in_specs=[pl.BlockSpec((None, tr, cs), lambda i, wh: (l, i, 0))], out_specs=o_spec),
        out_shape=jax.ShapeDtypeStruct(full, BF16), compiler_params=_cparams(("parallel",)),
    )(where, w)


def _allgather_weights(placed, name):
    names = list(BIG)

    def body(*refs):
        outs = refs[NB:2 * NB]
        send, recv = refs[2 * NB:]
        x, y, c, chips = _place()
        me = 2 * x + y
        sib = (x, y, 1 - c)
        sends = []
        for j, chip in enumerate(chips):
            for wi, n in enumerate(names):
                reg = _half_region(outs[wi], SHARD_AXIS[n], me, c)
                cp = _remote(reg, reg, send.at[j * NB + wi], recv.at[j * NB + wi], (*chip, c))
                cp.start()
                sends.append(cp)
        for j, chip in enumerate(chips):
            them = 2 * chip[0] + chip[1]
            for wi, n in enumerate(names):
                reg = _half_region(outs[wi], SHARD_AXIS[n], them, c)
                _remote(reg, reg, send.at[j * NB + wi], recv.at[j * NB + wi], (*chip, c)).wait_recv()
                cp = _remote(reg, reg, send.at[(3 + j) * NB + wi], recv.at[(3 + j) * NB + wi], sib)
                cp.start()
                sends.append(cp)
        for j, chip in enumerate(chips):
            them = 2 * chip[0] + chip[1]
            for wi, n in enumerate(names):
                reg = _half_region(outs[wi], SHARD_AXIS[n], them, 1 - c)
                _remote(reg, reg, send.at[(3 + j) * NB + wi], recv.at[(3 + j) * NB + wi], sib).wait_recv()
        for cp in sends:
            cp.wait_send()

    outs = pl.pallas_call(
        body, name=name, in_specs=[ANY] * NB, out_specs=[ANY] * NB,
        out_shape=[jax.ShapeDtypeStruct(placed[n].shape, BF16) for n in names],
        scratch_shapes=[pltpu.SemaphoreType.DMA((6 * NB,)), pltpu.SemaphoreType.DMA((6 * NB,))],
        input_output_aliases={wi: wi for wi in range(NB)},
    )(*[placed[n] for n in names])
    return dict(zip(names, outs))


def _allgather_gains(g_shard, name):
    A, Cs = g_shard.shape

    def body(in_ref, out_ref, send, recv, local):
        x, y, c, chips = _place()
        me = 2 * x + y
        mine = pltpu.make_async_copy(in_ref, out_ref.at[me], local)
        mine.start()
        cps = [_remote(in_ref, out_ref.at[me], send.at[j], recv.at[j], (*chip, c)) for j, chip in enumerate(chips)]
        for cp in cps:
            cp.start()
        for j, chip in enumerate(chips):
            them = 2 * chip[0] + chip[1]
            _remote(in_ref, out_ref.at[them], send.at[j], recv.at[j], (*chip, c)).wait_recv()
        for cp in cps:
            cp.wait_send()
        mine.wait()

    return pl.pallas_call(
        body, name=name, in_specs=[ANY], out_specs=ANY,
        out_shape=jax.ShapeDtypeStruct((N_CHIPS, A, Cs), F32),
        scratch_shapes=[pltpu.SemaphoreType.DMA((3,)), pltpu.SemaphoreType.DMA((3,)), pltpu.SemaphoreType.DMA],
    )(g_shard)


def _half_shape(n, shape):
    R, C = shape
    if SHARD_AXIS[n] == 0:
        return (N_CHIPS, R // N_CHIPS // 2, C)
    return (R // 2, C)


def _as_halves(n, g):
    R, C = g.shape
    if SHARD_AXIS[n] == 0:
        return g.reshape(N_CHIPS, R // N_CHIPS, C)
    return g


def _my_half(n, ref, half):
    if SHARD_AXIS[n] == 0:
        hs = ref.shape[1] // 2
        return ref.at[:, pl.ds(half * hs, hs), :]
    hs = ref.shape[0] // 2
    return ref.at[pl.ds(half * hs, hs), :]


def _swap_sibling_halves(gw, name):
    names = list(BIG)

    def body(*refs):
        ins = refs[:NB]
        outs = refs[NB:2 * NB]
        send, recv = refs[2 * NB:]
        x, y, c, _ = _place()
        sib = (x, y, 1 - c)
        cps = []
        for wi, n in enumerate(names):
            cp = _remote(_my_half(n, ins[wi], 1 - c), outs[wi], send.at[wi], recv.at[wi], sib)
            cp.start()
            cps.append(cp)
        for cp in cps:
            cp.wait()

    shapes = []
    for n in names:
        g = gw[n]
        shapes.append((g.shape[0], g.shape[1] // 2, g.shape[2]) if g.ndim == 3 else (g.shape[0] // 2, g.shape[1]))
    outs = pl.pallas_call(
        body, name=name, in_specs=[ANY] * NB, out_specs=[ANY] * NB,
        out_shape=[jax.ShapeDtypeStruct(s, BF16) for s in shapes],
        scratch_shapes=[pltpu.SemaphoreType.DMA((NB,)), pltpu.SemaphoreType.DMA((NB,))],
    )(*[gw[n] for n in names])
    return dict(zip(names, outs))


def _pair_sum(n, g, other, c_arr, name):
    if g.ndim == 3:
        nc, rs, C = g.shape
        hs = rs // 2
        tr = _blk(hs, (256, 128, 64, 32, 16))
        nt = hs // tr
        grid = (nc, nt)
        g_spec = pl.BlockSpec((1, tr, C), lambda k, i, c_ref: (k, c_ref[0] * nt + i, 0))
        o_spec = pl.BlockSpec((1, tr, C), lambda k, i, c_ref: (k, i, 0))
    else:
        R, C = g.shape
        hs = R // 2
        cs = C // N_CHIPS
        tr = _blk(hs, (256, 128, 64, 32, 16))
        nt = hs // tr
        grid = (nt, N_CHIPS)
        g_spec = pl.BlockSpec((tr, cs), lambda i, k, c_ref: (c_ref[0] * nt + i, k))
        o_spec = pl.BlockSpec((tr, cs), lambda i, k, c_ref: (i, k))

    def body(c_ref, g_ref, o_ref, out_ref):
        out_ref[...] = (g_ref[...].astype(F32) + o_ref[...].astype(F32)).astype(BF16)

    return pl.pallas_call(
        body, name=name,
        grid_spec=pltpu.PrefetchScalarGridSpec(num_scalar_prefetch=1, grid=grid, in_specs=[g_spec, o_spec],
                                               out_specs=o_spec),
        out_shape=jax.ShapeDtypeStruct(other.shape, BF16),
        compiler_params=_cparams(("parallel", "parallel")),
    )(c_arr, g, other)


def _scatter_to_owners(pairs, name):
    names = list(BIG)

    def piece(n, ref, chip):
        if SHARD_AXIS[n] == 0:
            return ref.at[chip]
        cs = ref.shape[1] // N_CHIPS
        return ref.at[:, pl.ds(chip * cs, cs)]

    def body(*refs):
        ins = refs[:NB]
        outs = refs[NB:2 * NB]
        send, recv = refs[2 * NB:]
        x, y, c, chips = _place()
        me = 2 * x + y
        sends = []
        for j, chip in enumerate(chips):
            them = 2 * chip[0] + chip[1]
            for wi, n in enumerate(names):
                cp = _remote(piece(n, ins[wi], them), outs[wi].at[j], send.at[j * NB + wi], recv.at[j * NB + wi],
                             (*chip, c))
                cp.start()
                sends.append(cp)
        for j, chip in enumerate(chips):
            for wi, n in enumerate(names):
                _remote(piece(n, ins[wi], me), outs[wi].at[j], send.at[j * NB + wi], recv.at[j * NB + wi],
                        (*chip, c)).wait_recv()
        for cp in sends:
            cp.wait_send()

    shapes = []
    for n in names:
        p = pairs[n]
        shapes.append((3,) + (p.shape[1:] if p.ndim == 3 else (p.shape[0], p.shape[1] // N_CHIPS)))
    outs = pl.pallas_call(
        body, name=name, in_specs=[ANY] * NB, out_specs=[ANY] * NB,
        out_shape=[jax.ShapeDtypeStruct(s, BF16) for s in shapes],
        scratch_shapes=[pltpu.SemaphoreType.DMA((3 * NB,)), pltpu.SemaphoreType.DMA((3 * NB,))],
    )(*[pairs[n] for n in names])
    return dict(zip(names, outs))


def _sum_final(n, pair, q, stack, l, where, name):
    hs, cc = q.shape[1:]
    tr = _blk(hs, (256, 128, 64, 32, 16))
    nt = hs // tr
    if SHARD_AXIS[n] == 0:
        p_spec = pl.BlockSpec((None, tr, cc), lambda i, wh: (wh[0], i, 0))
    else:
        p_spec = pl.BlockSpec((tr, cc), lambda i, wh: (i, wh[0]))

    def body(wh, p_ref, q_ref, stack_any, o_ref):
        o_ref[...] = ((p_ref[...].astype(F32) + q_ref[0].astype(F32)) + q_ref[1].astype(F32)) + q_ref[2].astype(F32)

    return pl.pallas_call(
        body, name=name,
        grid_spec=pltpu.PrefetchScalarGridSpec(
            num_scalar_prefetch=1, grid=(nt,),
            in_specs=[p_spec, pl.BlockSpec((3, tr, cc), lambda i, wh: (0, i, 0)), ANY],
            out_specs=pl.BlockSpec((None, tr, cc), lambda i, wh: (l, wh[1] * nt + i, 0))),
        out_shape=jax.ShapeDtypeStruct(stack.shape, F32),
        input_output_aliases={3: 0},
        compiler_params=_cparams(("parallel",)),
    )(where, pair, q, stack)


def _sum_slots(q, name):
    K, A, B = q.shape
    tr = _blk(A, (256, 128, 64, 32, 16, 8))

    def body(q_ref, o_ref):
        s = q_ref[0].astype(F32)
        for k in range(1, K):
            s = s + q_ref[k].astype(F32)
        o_ref[...] = s

    return pl.pallas_call(
        body, name=name, grid=(A // tr,),
        in_specs=[pl.BlockSpec((K, tr, B), lambda i: (0, i, 0))], out_specs=pl.BlockSpec((tr, B), lambda i: (i, 0)),
        out_shape=jax.ShapeDtypeStruct((A, B), F32), compiler_params=_cparams(("parallel",)),
    )(q)


def _finish_grads(stacks, small, name):
    names = list(BIG)
    A = small.shape[0]
    nh = NB

    def body(*refs):
        small_ref = refs[nh]
        outs = refs[nh + 1:nh + 1 + NB]
        slots = refs[nh + 1 + NB]
        send, recv, local = refs[nh + 2 + NB:]
        x, y, c, _ = _place()
        sib = (x, y, 1 - c)
        me = 4 * x + 2 * y + c
        locals_, sends = [], []
        for wi, n in enumerate(names):
            hr = outs[wi].shape[1] // 2
            reg = outs[wi].at[:, pl.ds(c * hr, hr), :]
            cp = _remote(reg, reg, send.at[wi], recv.at[wi], sib)
            cp.start()
            sends.append(cp)
        cp = pltpu.make_async_copy(small_ref, slots.at[me], local)
        cp.start()
        locals_.append(cp)
        peers = []
        for dx in range(2):
            for dy in range(2):
                for dc in range(2):
                    if dx or dy or dc:
                        peers.append((dx, dy, dc))
        for j, (dx, dy, dc) in enumerate(peers):
            to = (jnp.bitwise_xor(x, dx), jnp.bitwise_xor(y, dy), jnp.bitwise_xor(c, dc))
            cp = _remote(small_ref, slots.at[me], send.at[nh + j], recv.at[nh + j], to)
            cp.start()
            sends.append(cp)
        for wi, n in enumerate(names):
            hr = outs[wi].shape[1] // 2
            reg = outs[wi].at[:, pl.ds((1 - c) * hr, hr), :]
            _remote(reg, reg, send.at[wi], recv.at[wi], sib).wait_recv()
        for j, (dx, dy, dc) in enumerate(peers):
            frm = 4 * jnp.bitwise_xor(x, dx) + 2 * jnp.bitwise_xor(y, dy) + jnp.bitwise_xor(c, dc)
            _remote(small_ref, slots.at[frm], send.at[nh + j], recv.at[nh + j], sib).wait_recv()
        for cp in sends:
            cp.wait_send()
        for cp in locals_:
            cp.wait()

    res = pl.pallas_call(
        body, name=name, in_specs=[ANY] * (nh + 1), out_specs=[ANY] * (NB + 1),
        out_shape=[jax.ShapeDtypeStruct(stacks[n].shape, F32) for n in names]
        + [jax.ShapeDtypeStruct((8, A, LANES), F32)],
        scratch_shapes=[pltpu.SemaphoreType.DMA((nh + 7,)), pltpu.SemaphoreType.DMA((nh + 7,)),
                        pltpu.SemaphoreType.DMA],
        input_output_aliases={wi: wi for wi in range(NB)},
    )(*[stacks[n] for n in names], small)
    return dict(zip(names, res[:NB])), res[NB]


def _adamw(w, g, m, v, name):
    shape = w.shape
    if w.ndim == 2:
        w, g, m, v = (t.reshape((1,) + shape) for t in (w, g, m, v))
    L, A, B = w.shape
    tr = _blk(A, (128, 64, 32, 16, 8))
    c1 = 1.0 - ADAM_B1 ** ADAM_STEP
    c2 = 1.0 - ADAM_B2 ** ADAM_STEP

    def body(w_ref, g_ref, m_ref, v_ref, d_ref, nm_ref, nv_ref):
        gv = g_ref[...]
        nm = ADAM_B1 * m_ref[...] + (1.0 - ADAM_B1) * gv
        nv = ADAM_B2 * v_ref[...] + (1.0 - ADAM_B2) * (gv * gv)
        m_hat = nm / c1
        v_hat = nv / c2
        d_ref[...] = -ADAM_LR * (m_hat / (jnp.sqrt(v_hat) + ADAM_EPS) + ADAM_WD * w_ref[...])
        nm_ref[...] = nm
        nv_ref[...] = nv

    spec = pl.BlockSpec((1, tr, B), lambda l, i: (l, i, 0))
    sds = jax.ShapeDtypeStruct((L, A, B), F32)
    outs = pl.pallas_call(
        body, name=name, grid=(L, A // tr), in_specs=[spec] * 4, out_specs=[spec] * 3, out_shape=[sds] * 3,
        compiler_params=_cparams(("parallel", "parallel")),
    )(w, g, m, v)
    return tuple(o.reshape(shape) for o in outs)


SMALL_ROWS = 520


def _train_step(x, mem, target, w, m, v):
    L = w["lb_logits"].shape[0]
    cx, cy, cc = lax.axis_index("x"), lax.axis_index("y"), lax.axis_index("c")
    me = 2 * cx + cy
    c_arr = jnp.reshape(cc, (1,)).astype(jnp.int32)
    where = jnp.stack([me, cc]).astype(jnp.int32)

    full = []
    for l in range(L):
        placed = {n: _cast_place(n, w[n], l, where, f"cast_{n}_l{l}") for n in BIG}
        full.append(_allgather_weights(placed, f"allgather_l{l}"))
    gs = _allgather_gains(w["norm_gains"].reshape(L * N_NORMS, -1), "allgather_gains")
    gains = jnp.transpose(gs, (1, 0, 2)).reshape(L, N_NORMS, D_MODEL)

    stacks = {n: lax.empty(w[n].shape, F32) for n in BIG}

    def reduce_layer(l, gw):
        views = {n: _as_halves(n, gw[n]) for n in BIG}
        theirs = _swap_sibling_halves(views, f"swap_halves_l{l}")
        pairs = {n: _pair_sum(n, views[n], theirs[n], c_arr, f"pair_sum_{n}_l{l}") for n in BIG}
        slots = _scatter_to_owners(pairs, f"scatter_l{l}")
        for n in BIG:
            stacks[n] = _sum_final(n, pairs[n], slots[n], stacks[n], l, where, f"sum_chips_{n}_l{l}")
        return None

    loss, dx, _, d_rb, d_lb, d_gains, d_hg = _local_step(
        x[0], mem[0], target[0], w["rel_bias"], w["lb_logits"], gains, w["hg_norm"], lambda l: full[l], reduce_layer)

    flat = jnp.concatenate([d_rb.reshape(-1), d_lb.reshape(-1), d_hg.reshape(-1), d_gains.reshape(-1)])
    small = jnp.pad(flat, (0, SMALL_ROWS * LANES - flat.shape[0])).reshape(SMALL_ROWS, LANES)
    grads, slots = _finish_grads(stacks, small, "finish_grads")
    tot = _sum_slots(slots, "sum_small").reshape(-1)
    n_rb, n_lb = d_rb.size, d_lb.size
    grads["rel_bias"] = tot[:n_rb].reshape(d_rb.shape)
    grads["lb_logits"] = tot[n_rb:n_rb + n_lb].reshape(d_lb.shape)
    grads["hg_norm"] = tot[n_rb + n_lb:n_rb + 2 * n_lb].reshape(d_hg.shape)
    g_full = tot[n_rb + 2 * n_lb:n_rb + 2 * n_lb + d_gains.size].reshape(d_gains.shape)
    cs = D_MODEL // N_CHIPS
    grads["norm_gains"] = lax.dynamic_slice_in_dim(g_full, me * cs, cs, axis=2)

    delta, new_m, new_v = {}, {}, {}
    for n in OUT_ORDER:
        delta[n], new_m[n], new_v[n] = _adamw(w[n], grads[n], m[n], v[n], f"adamw_{n}")
    loss = lax.psum(loss, ("x", "y", "c"))
    return (loss, dx[None], *[grads[n] for n in OUT_ORDER], *[delta[n] for n in OUT_ORDER],
            *[new_m[n] for n in OUT_ORDER], *[new_v[n] for n in OUT_ORDER])
```

```python
import functools
import math

import numpy as np
import jax
import jax.numpy as jnp
from jax import lax
from jax.experimental import pallas as pl
from jax.experimental.pallas import tpu as pltpu

F32 = jnp.float32
BF16 = jnp.bfloat16
MESH_ID = pl.DeviceIdType.MESH

D_MODEL = 2048
HEAD_DIM = 128
N_HEADS = 8
ATTN_W = 1024
HG_W = 1024
HG_CHUNK = 64
Q_BLOCK = 128
DILATIONS = (16, 4, 1)
REL_BUCKETS = 32
REL_MAX_DIST = 2048
CROSS_HEADS = 4
CROSS_DIM = 512
D_FF = 5632
RMS_EPS = 1e-6
NEG_INF = -1e30
N_NORMS = 7

ADAM_LR = 0.001
ADAM_B1 = 0.9
ADAM_B2 = 0.999
ADAM_EPS = 1e-08
ADAM_WD = 0.01
ADAM_STEP = 10

VMEM_LIMIT_V7X = 56 * 1024 * 1024
LANES = 128
N_CHIPS = 4


def _cparams(sem=None):
    if sem is None:
        return pltpu.CompilerParams(vmem_limit_bytes=VMEM_LIMIT_V7X)
    return pltpu.CompilerParams(dimension_semantics=sem, vmem_limit_bytes=VMEM_LIMIT_V7X)


def _blk(n, prefs):
    for p in prefs:
        if p <= n and n % p == 0:
            return p
    return n


def _dot(a, b, dims):
    return lax.dot_general(a.astype(BF16), b.astype(BF16), (dims, ((), ())), preferred_element_type=F32)


def _dot_nn(a, b):
    return _dot(a, b, ((1,), (0,)))


def _dot_nt(a, b):
    return _dot(a, b, ((1,), (1,)))


def _dot_tn(a, b):
    return _dot(a, b, ((0,), (0,)))


def _sigmoid(x):
    return 1.0 / (1.0 + jnp.exp(-x))


def _mm(a_list, b_list, mode, out_dtype, name):
    if not isinstance(a_list, (list, tuple)):
        a_list = [a_list]
    if not isinstance(b_list, (list, tuple)):
        b_list = [b_list]
    na, nb = len(a_list), len(b_list)
    deep = (2048, 2816, 1024, 1408, 512, 256, 128)
    if mode == "nn":
        M, K = a_list[0].shape
        N = b_list[0].shape[1]
        bm, bn, bk = _blk(M, (1024, 512, 256, 128)), _blk(N, (1024, 512, 256, 128)), _blk(K, deep)
    elif mode == "nt":
        M = a_list[0].shape[0]
        K = sum(a.shape[1] for a in a_list)
        N = b_list[0].shape[0]
        bm, bn = _blk(M, (1024, 512, 256, 128)), _blk(N, (1024, 512, 256, 128))
        bk = _blk(math.gcd(*[a.shape[1] for a in a_list]), deep)
    else:
        K, M = a_list[0].shape
        N = sum(b.shape[1] for b in b_list)
        bn = _blk(math.gcd(*[b.shape[1] for b in b_list]), (1024, 2816, 512, 256, 128))
        wide = bn > 1024
        bm = _blk(M, (512, 256, 128) if wide else (1024, 1408, 512, 256, 128))
        bk = _blk(K, (1024, 512, 256, 128) if wide else (2048, 1024, 512, 256, 128))
    nk = K // bk
    grid = (M // bm, N // bn, nk)

    if mode == "nn":
        a_specs = [pl.BlockSpec((bm, bk), lambda i, j, k: (i, k))]
        b_specs = [pl.BlockSpec((bk, bn), lambda i, j, k: (k, j))]
        dims = ((1,), (0,))
    elif mode == "nt":
        a_specs, off = [], 0
        for a in a_list:
            n_i = a.shape[1] // bk
            a_specs.append(pl.BlockSpec((bm, bk), functools.partial(
                lambda i, j, k, off, n_i: (i, jnp.clip(k - off, 0, n_i - 1)), off=off, n_i=n_i)))
            off += n_i
        b_specs = [pl.BlockSpec((bn, bk), lambda i, j, k: (j, k))]
        dims = ((1,), (1,))
    else:
        a_specs = [pl.BlockSpec((bk, bm), lambda i, j, k: (k, i))]
        b_specs, off = [], 0
        for b in b_list:
            n_j = b.shape[1] // bn
            b_specs.append(pl.BlockSpec((bk, bn), functools.partial(
                lambda i, j, k, off, n_j: (jnp.where((j >= off) & (j < off + n_j), k, 0),
                                           jnp.clip(j - off, 0, n_j - 1)), off=off, n_j=n_j)))
            off += n_j
        dims = ((0,), (0,))
    a_bounds = np.cumsum([0] + [a.shape[1] // bk for a in a_list]) if mode == "nt" else None
    b_bounds = np.cumsum([0] + [b.shape[1] // bn for b in b_list]) if mode == "tn" else None

    def body(*refs):
        a_refs = refs[:na]
        b_refs = refs[na:na + nb]
        o_ref = refs[na + nb]
        acc_ref = refs[na + nb + 1] if nk > 1 else None
        j = pl.program_id(1)
        k = pl.program_id(2)

        def accumulate(p):
            if nk == 1:
                o_ref[...] = p.astype(out_dtype)
                return

            @pl.when(k == 0)
            def _():
                acc_ref[...] = p

            @pl.when(k > 0)
            def _():
                acc_ref[...] += p

        if na > 1:
            for t in range(na):
                @pl.when((k >= int(a_bounds[t])) & (k < int(a_bounds[t + 1])))
                def _(t=t):
                    accumulate(_dot(a_refs[t][...], b_refs[0][...], dims))
        elif nb > 1:
            for t in range(nb):
                @pl.when((j >= int(b_bounds[t])) & (j < int(b_bounds[t + 1])))
                def _(t=t):
                    accumulate(_dot(a_refs[0][...], b_refs[t][...], dims))
        else:
            accumulate(_dot(a_refs[0][...], b_refs[0][...], dims))

        if nk > 1:
            @pl.when(k == nk - 1)
            def _():
                o_ref[...] = acc_ref[...].astype(out_dtype)

    return pl.pallas_call(
        body, name=name, grid=grid,
        in_specs=a_specs + b_specs,
        out_specs=pl.BlockSpec((bm, bn), lambda i, j, k: (i, j)),
        out_shape=jax.ShapeDtypeStruct((M, N), out_dtype),
        scratch_shapes=[pltpu.VMEM((bm, bn), F32)] if nk > 1 else [],
        compiler_params=_cparams(("parallel", "parallel", "arbitrary")),
    )(*a_list, *b_list)


def _rms_fwd(x, g, name):
    S, D = x.shape
    tm = _blk(S, (256, 128, 64, 8))

    def body(x_ref, g_ref, o_ref):
        xv = x_ref[...]
        r = lax.rsqrt(jnp.mean(xv * xv, axis=-1, keepdims=True) + RMS_EPS)
        o_ref[...] = ((xv * r) * g_ref[...]).astype(o_ref.dtype)

    return pl.pallas_call(
        body, name=name, grid=(S // tm,),
        in_specs=[pl.BlockSpec((tm, D), lambda i: (i, 0)), pl.BlockSpec((1, D), lambda i: (0, 0))],
        out_specs=pl.BlockSpec((tm, D), lambda i: (i, 0)),
        out_shape=jax.ShapeDtypeStruct((S, D), BF16),
        compiler_params=_cparams(("parallel",)),
    )(x, g)


def _norm_residual(x, t, g, g_next, name):
    S, D = x.shape
    tm = _blk(S, (256, 128, 64, 8))
    chain = g_next is not None

    def body(*refs):
        if chain:
            x_ref, t_ref, g_ref, gn_ref, o_ref, h_ref = refs
        else:
            x_ref, t_ref, g_ref, o_ref = refs
        tv = t_ref[...]
        r = lax.rsqrt(jnp.mean(tv * tv, axis=-1, keepdims=True) + RMS_EPS)
        xn = x_ref[...] + (tv * r) * g_ref[...]
        o_ref[...] = xn
        if chain:
            rn = lax.rsqrt(jnp.mean(xn * xn, axis=-1, keepdims=True) + RMS_EPS)
            h_ref[...] = ((xn * rn) * gn_ref[...]).astype(BF16)

    row = pl.BlockSpec((tm, D), lambda i: (i, 0))
    vec = pl.BlockSpec((1, D), lambda i: (0, 0))
    return pl.pallas_call(
        body, name=name, grid=(S // tm,),
        in_specs=[row, row, vec] + ([vec] if chain else []),
        out_specs=[row, row] if chain else row,
        out_shape=[jax.ShapeDtypeStruct((S, D), F32), jax.ShapeDtypeStruct((S, D), BF16)] if chain
        else jax.ShapeDtypeStruct((S, D), F32),
        compiler_params=_cparams(("parallel",)),
    )(*([x, t, g, g_next] if chain else [x, t, g]))


def _rms_bwd(x, g, dh, res, out_dtype, name):
    S, D = x.shape
    tm = _blk(S, (256, 128, 64, 8))
    has_res = res is not None

    def body(*refs):
        if has_res:
            x_ref, g_ref, dh_ref, res_ref, dx_ref, dg_ref = refs
        else:
            x_ref, g_ref, dh_ref, dx_ref, dg_ref = refs
        xv = x_ref[...]
        r = lax.rsqrt(jnp.mean(xv * xv, axis=-1, keepdims=True) + RMS_EPS)
        xh = xv * r
        dhv = dh_ref[...].astype(F32)
        gd = dhv * g_ref[...]
        dx = r * (gd - xh * jnp.mean(gd * xh, axis=-1, keepdims=True))
        if has_res:
            dx = dx + res_ref[...]
        dx_ref[...] = dx.astype(out_dtype)
        part = jnp.sum(dhv * xh, axis=0, keepdims=True)

        @pl.when(pl.program_id(0) == 0)
        def _():
            dg_ref[...] = part

        @pl.when(pl.program_id(0) > 0)
        def _():
            dg_ref[...] += part

    row = pl.BlockSpec((tm, D), lambda i: (i, 0))
    vec = pl.BlockSpec((1, D), lambda i: (0, 0))
    ins = [x, g, dh] + ([res] if has_res else [])
    return pl.pallas_call(
        body, name=name, grid=(S // tm,),
        in_specs=[row, vec, row] + ([row] if has_res else []),
        out_specs=[row, vec],
        out_shape=[jax.ShapeDtypeStruct((S, D), out_dtype), jax.ShapeDtypeStruct((1, D), F32)],
        compiler_params=_cparams(("arbitrary",)),
    )(*ins)


def _rel_bucket(dist):
    max_exact = REL_BUCKETS // 2
    d_f = jnp.maximum(dist, 1).astype(jnp.float32)
    large = max_exact + (jnp.log(d_f / max_exact) / math.log(REL_MAX_DIST / max_exact)
                         * (REL_BUCKETS - max_exact)).astype(jnp.int32)
    large = jnp.minimum(large, REL_BUCKETS - 1)
    return jnp.where(dist < max_exact, dist, large)


def _bias_tables(rel_bias):
    qi = jnp.arange(Q_BLOCK)[:, None]
    kj = jnp.arange(2 * Q_BLOCK)[None, :]
    m = qi - kj + Q_BLOCK
    band_ok = (m >= 0) & (m <= Q_BLOCK)
    tabs = []
    for d in DILATIONS:
        bucket = _rel_bucket(jnp.maximum(m, 0) * d)
        onehot = (bucket[:, :, None] == jnp.arange(REL_BUCKETS)[None, None, :]).astype(F32)
        bias = jnp.einsum("qkb,bh->hqk", onehot, rel_bias.astype(F32), precision=lax.Precision.HIGHEST)
        tabs.append(jnp.where(band_ok[None], bias, NEG_INF))
    return jnp.stack(tabs)


HEADS_PER_STEP = {16: 1, 4: 1, 1: 8}


def _rows(r, d):
    return pl.ds(r, Q_BLOCK, stride=d) if d > 1 else pl.ds(0, Q_BLOCK)


def _attn_fwd(proj, bias, state, d, last, cat_in, name):
    S = proj.shape[0]
    R = Q_BLOCK * d
    nsb = S // R
    first = state is None
    scale = HEAD_DIM ** -0.5
    hb = HEADS_PER_STEP[d]
    W = hb * HEAD_DIM

    def body(*refs):
        q_ref, kp_ref, kc_ref, vp_ref, vc_ref, b_ref = refs[:6]
        pos = 6
        if not first:
            m_in, l_in, a_in = refs[pos:pos + 3]
            pos += 3
        if last:
            pos += 1
            o_ref, lse_ref, o_tmp = refs[pos:pos + 3]
        else:
            m_out, l_out, a_out = refs[pos:pos + 3]
        n = pl.program_id(0)
        for hh in range(hb):
            ls = slice(hh * HEAD_DIM, (hh + 1) * HEAD_DIM)
            bp = b_ref[hh, :, :Q_BLOCK]
            bc = b_ref[hh, :, Q_BLOCK:]
            for r in range(d):
                rows = _rows(r, d)
                q = q_ref[rows, ls]
                sp = _dot_nt(q, kp_ref[rows, ls]) * scale + bp
                sp = jnp.where(n == 0, NEG_INF, sp)
                sc = _dot_nt(q, kc_ref[rows, ls]) * scale + bc
                mrow = jnp.maximum(jnp.max(sp, axis=-1, keepdims=True), jnp.max(sc, axis=-1, keepdims=True))
                if first:
                    m_new = mrow
                else:
                    m_old = m_in[rows, ls][:, :1]
                    m_new = jnp.maximum(m_old, mrow)
                pp = jnp.exp(sp - m_new)
                pc = jnp.exp(sc - m_new)
                lrow = jnp.sum(pp, axis=-1, keepdims=True) + jnp.sum(pc, axis=-1, keepdims=True)
                pv = _dot_nn(pp, vp_ref[rows, ls]) + _dot_nn(pc, vc_ref[rows, ls])
                if first:
                    l_new, a_new = lrow, pv
                else:
                    alpha = jnp.exp(m_old - m_new)
                    l_new = alpha * l_in[rows, ls][:, :1] + lrow
                    a_new = alpha * a_in[rows, ls] + pv
                if last:
                    o_tmp[rows, ls] = a_new / l_new
                    lse_ref[rows, ls] = jnp.broadcast_to(m_new + jnp.log(l_new), (Q_BLOCK, HEAD_DIM))
                else:
                    m_out[rows, ls] = jnp.broadcast_to(m_new, (Q_BLOCK, HEAD_DIM))
                    l_out[rows, ls] = jnp.broadcast_to(l_new, (Q_BLOCK, HEAD_DIM))
                    a_out[rows, ls] = a_new
        if last:
            o_ref[...] = o_tmp[...].astype(BF16)

    ng = N_HEADS // hb

    def col(g0):
        return pl.BlockSpec((R, W), lambda n, h: (n, g0 * ng + h))

    def col_prev(g0):
        return pl.BlockSpec((R, W), lambda n, h: (jnp.maximum(n - 1, 0), g0 * ng + h))

    in_specs = [col(0), col_prev(1), col(1), col_prev(2), col(2),
                pl.BlockSpec((hb, Q_BLOCK, 2 * Q_BLOCK), lambda n, h: (h, 0, 0))]
    ins = [proj, proj, proj, proj, proj, bias]
    if not first:
        in_specs += [col(0)] * 3
        ins += list(state)
    st = jax.ShapeDtypeStruct((S, ATTN_W), F32)
    if last:
        in_specs.append(pl.BlockSpec(memory_space=pl.ANY))
        ins.append(cat_in)
        out_specs = [col(0), col(0)]
        out_shape = [jax.ShapeDtypeStruct(cat_in.shape, BF16), st]
        scratch = [pltpu.VMEM((R, W), F32)]
        aliases = {len(ins) - 1: 0}
    else:
        out_specs = [col(0)] * 3
        out_shape = [st, st, st]
        scratch = []
        aliases = {}
    return pl.pallas_call(
        body, name=name, grid=(nsb, ng), in_specs=in_specs, out_specs=out_specs,
        out_shape=out_shape, scratch_shapes=scratch, input_output_aliases=aliases,
        compiler_params=_cparams(("arbitrary", "arbitrary")),
    )(*ins)


def _attn_bwd(proj, bias, cat, d_cat, lse, acc, d, last, name):
    S = proj.shape[0]
    R = Q_BLOCK * d
    nsb = S // R
    first = acc is None
    scale = HEAD_DIM ** -0.5
    odt = BF16 if last else F32
    hb = HEADS_PER_STEP[d]
    W = hb * HEAD_DIM

    def body(*refs):
        q_ref, kp_ref, kc_ref, vp_ref, vc_ref, b_ref, o_ref, do_ref, lse_ref = refs[:9]
        pos = 9
        if not first:
            dq_in, dk_in, dv_in = refs[pos:pos + 3]
            pos += 3
        dq_out, dk_out, dv_out, db_out = refs[pos:pos + 4]
        pos += 4
        ck, cv, o_f, do_f, dq_t, dk_t, dv_t, db_acc = refs[pos:pos + 8]
        n = pl.program_id(1)

        @pl.when(n == 0)
        def _():
            db_acc[...] = jnp.zeros_like(db_acc)
            ck[...] = jnp.zeros_like(ck)
            cv[...] = jnp.zeros_like(cv)

        @pl.when(n < nsb)
        def _():
            o_f[...] = o_ref[...].astype(F32)
            do_f[...] = do_ref[...].astype(F32)
            for hh in range(hb):
                ls = slice(hh * HEAD_DIM, (hh + 1) * HEAD_DIM)
                bp = b_ref[hh, :, :Q_BLOCK]
                bc = b_ref[hh, :, Q_BLOCK:]
                for r in range(d):
                    rows = _rows(r, d)
                    q = q_ref[rows, ls]
                    kp = kp_ref[rows, ls]
                    kc = kc_ref[rows, ls]
                    vp = vp_ref[rows, ls]
                    vc = vc_ref[rows, ls]
                    do = do_f[rows, ls]
                    lse_r = lse_ref[rows, ls][:, :1]
                    sp = _dot_nt(q, kp) * scale + bp
                    sp = jnp.where(n == 0, NEG_INF, sp)
                    sc = _dot_nt(q, kc) * scale + bc
                    pp = jnp.exp(sp - lse_r)
                    pc = jnp.exp(sc - lse_r)
                    dd = jnp.sum(do * o_f[rows, ls], axis=-1, keepdims=True)
                    dsp = pp * (_dot_nt(do, vp) - dd)
                    dsc = pc * (_dot_nt(do, vc) - dd)
                    db_acc[hh, :, :Q_BLOCK] += dsp
                    db_acc[hh, :, Q_BLOCK:] += dsc
                    dq = (_dot_nn(dsp, kp) + _dot_nn(dsc, kc)) * scale
                    dk_prev = ck[rows, ls] + _dot_tn(dsp, q) * scale
                    dv_prev = cv[rows, ls] + _dot_tn(pp, do)
                    ck[rows, ls] = _dot_tn(dsc, q) * scale
                    cv[rows, ls] = _dot_tn(pc, do)
                    if not first:
                        dq = dq + dq_in[rows, ls]
                        dk_prev = dk_prev + dk_in[rows, ls]
                        dv_prev = dv_prev + dv_in[rows, ls]
                    dq_t[rows, ls] = dq
                    dk_t[rows, ls] = dk_prev
                    dv_t[rows, ls] = dv_prev
            dq_out[...] = dq_t[...].astype(odt)

            @pl.when(n > 0)
            def _():
                dk_out[...] = dk_t[...].astype(odt)
                dv_out[...] = dv_t[...].astype(odt)

        @pl.when(n == nsb)
        def _():
            if first:
                dk_out[...] = ck[...].astype(odt)
                dv_out[...] = cv[...].astype(odt)
            else:
                dk_out[...] = (ck[...] + dk_in[...]).astype(odt)
                dv_out[...] = (cv[...] + dv_in[...]).astype(odt)
            db_out[...] = db_acc[...]

    last_n = nsb - 1
    ng = N_HEADS // hb

    def cur(g0):
        return pl.BlockSpec((R, W), lambda h, n: (jnp.minimum(n, last_n), g0 * ng + h))

    def prev(g0):
        return pl.BlockSpec((R, W), lambda h, n: (jnp.maximum(jnp.minimum(n, last_n) - 1, 0), g0 * ng + h))

    delayed = pl.BlockSpec((R, W), lambda h, n: (jnp.maximum(n - 1, 0), h))
    in_specs = [cur(0), prev(1), cur(1), prev(2), cur(2),
                pl.BlockSpec((hb, Q_BLOCK, 2 * Q_BLOCK), lambda h, n: (h, 0, 0)),
                cur(0), cur(0), cur(0)]
    ins = [proj, proj, proj, proj, proj, bias, cat, d_cat, lse]
    if not first:
        in_specs += [cur(0), delayed, delayed]
        ins += list(acc)
    st = jax.ShapeDtypeStruct((S, ATTN_W), odt)
    tile = pltpu.VMEM((R, W), F32)
    return pl.pallas_call(
        body, name=name, grid=(ng, nsb + 1), in_specs=in_specs,
        out_specs=[cur(0), delayed, delayed,
                   pl.BlockSpec((hb, Q_BLOCK, 2 * Q_BLOCK), lambda h, n: (h, 0, 0))],
        out_shape=[st, st, st, jax.ShapeDtypeStruct((N_HEADS, Q_BLOCK, 2 * Q_BLOCK), F32)],
        scratch_shapes=[tile] * 7 + [pltpu.VMEM((hb, Q_BLOCK, 2 * Q_BLOCK), F32)],
        compiler_params=_cparams(("arbitrary", "arbitrary")),
    )(*ins)


HG_LEVELS = (32, 16, 8, 4, 2, 1)
N_LEV = len(HG_LEVELS)


def _hg_consts():
    C = HG_CHUNK
    t = np.arange(C)
    mq, mk, masks = [], [], []
    for B in HG_LEVELS:
        up = (t // B) % 2 == 1
        bs = (t // B) * B
        be = bs + B - 1
        mq.append(up[:, None] & (t[None, :] >= bs[:, None]) & (t[None, :] <= t[:, None]))
        mk.append((~up)[:, None] & (t[None, :] > t[:, None]) & (t[None, :] <= be[:, None]))
        masks.append(up[:, None] & (~up)[None, :] & ((t[:, None] // (2 * B)) == (t[None, :] // (2 * B))))
    masks.append(np.eye(C, dtype=bool))
    mb = t[None, :] <= t[:, None]
    mw = t[None, :] > t[:, None]
    m_all = np.concatenate(mq + mk + [mb, mw], axis=0).astype(np.float32)
    return jnp.asarray(m_all, BF16), jnp.asarray(np.stack(masks).astype(np.float32))


def _split3(v):
    hi = v.astype(BF16)
    r1 = v - hi.astype(F32)
    mid = r1.astype(BF16)
    lo = (r1 - mid.astype(F32)).astype(BF16)
    return jnp.concatenate([hi, mid, lo], axis=1)


def _hg_chunk_fwd(fz, iv, qz, lbh, m_all, mask_ref, st_t):
    C = HG_CHUNK
    sig = _sigmoid(fz)
    f = lbh + (1.0 - lbh) * sig
    lf = jnp.log(f)
    kk = 1.0 - f
    sq = _sigmoid(qz)
    qq = qz * sq
    a3 = lax.dot_general(m_all, _split3(lf), (((1,), (0,)), ((), ())), preferred_element_type=F32)
    args = a3[:, :HEAD_DIM] + a3[:, HEAD_DIM:2 * HEAD_DIM] + a3[:, 2 * HEAD_DIM:]
    e = jnp.exp(args)
    qs = [qq * e[j * C:(j + 1) * C] for j in range(N_LEV)]
    ks = [kk * e[(N_LEV + j) * C:(N_LEV + j + 1) * C] for j in range(N_LEV)]
    a = mask_ref[N_LEV] * _dot_nt(qq, kk)
    for j in range(N_LEV):
        a = a + mask_ref[j] * _dot_nt(qs[j], ks[j])
    eb = e[2 * N_LEV * C:(2 * N_LEV + 1) * C]
    ew = e[(2 * N_LEV + 1) * C:]
    qe = qq * eb
    w = kk * ew
    o = _dot_nt(qe, st_t) + _dot_nn(a, iv)
    eb_last = eb[C - 1:C, :]
    new_st = st_t * eb_last + _dot_tn(iv, w)
    return dict(sig=sig, f=f, kk=kk, sq=sq, qq=qq, e=e, qs=qs, ks=ks, a=a, eb=eb, ew=ew, qe=qe, w=w, o=o,
                eb_last=eb_last, new_st=new_st)


def _hg_rows(S):
    return _blk(S, (256, 128, 64))


def _hgrn_fwd(proj, lb, gain, cat_in, name):
    S = proj.shape[0]
    rows = _hg_rows(S)
    cb = rows // HG_CHUNK
    m_all, masks = _hg_consts()

    def body(fz_ref, iv_ref, qz_ref, gz_ref, lb_ref, gain_ref, m_ref, mask_ref, cat_any, o_ref, st_ref, st_scr):
        @pl.when(pl.program_id(0) == 0)
        def _():
            st_scr[...] = jnp.zeros_like(st_scr)

        m_all_v = m_ref[...]

        def chunk(c, carry):
            rs = pl.ds(pl.multiple_of(c * HG_CHUNK, HG_CHUNK), HG_CHUNK)
            for h in range(N_HEADS):
                sl = slice(h * HEAD_DIM, (h + 1) * HEAD_DIM)
                st_t = st_scr[h]
                st_ref[c, h] = st_t
                gz = gz_ref[rs, sl]
                iv = iv_ref[rs, sl]
                q = _hg_chunk_fwd(fz_ref[rs, sl], iv, qz_ref[rs, sl], lb_ref[:, sl], m_all_v, mask_ref, st_t)
                st_scr[h] = q["new_st"]
                o = q["o"]
                r = lax.rsqrt(jnp.mean(o * o, axis=-1, keepdims=True) + RMS_EPS)
                y = ((o * r) * gain_ref[:, sl]) * (gz * _sigmoid(gz))
                o_ref[rs, sl] = y.astype(BF16)
            return carry

        lax.fori_loop(0, cb, chunk, 0)

    def col(c):
        return pl.BlockSpec((rows, HG_W), lambda i: (i, c))

    vec = pl.BlockSpec((1, HG_W), lambda i: (0, 0))
    return pl.pallas_call(
        body, name=name, grid=(S // rows,),
        in_specs=[col(3), col(4), col(5), col(6), vec, vec,
                  pl.BlockSpec(m_all.shape, lambda i: (0, 0)), pl.BlockSpec(masks.shape, lambda i: (0, 0, 0)),
                  pl.BlockSpec(memory_space=pl.ANY)],
        out_specs=[col(1), pl.BlockSpec((cb, N_HEADS, HEAD_DIM, HEAD_DIM), lambda i: (i, 0, 0, 0))],
        out_shape=[jax.ShapeDtypeStruct(cat_in.shape, BF16),
                   jax.ShapeDtypeStruct((S // HG_CHUNK, N_HEADS, HEAD_DIM, HEAD_DIM), F32)],
        scratch_shapes=[pltpu.VMEM((N_HEADS, HEAD_DIM, HEAD_DIM), F32)],
        input_output_aliases={8: 0},
        compiler_params=_cparams(("arbitrary",)),
    )(proj, proj, proj, proj, lb, gain, m_all, masks, cat_in)


def _hgrn_bwd(proj, lb, gain, states, d_cat, name):
    S = proj.shape[0]
    rows = _hg_rows(S)
    cb = rows // HG_CHUNK
    nblk = S // rows
    C = HG_CHUNK
    m_all, masks = _hg_consts()

    def body(fz_ref, iv_ref, qz_ref, gz_ref, lb_ref, gain_ref, m_ref, mask_ref, st_ref, dy_ref,
             dz_ref, dlb_ref, dgain_ref, dst_scr):
        @pl.when(pl.program_id(0) == 0)
        def _():
            dst_scr[...] = jnp.zeros_like(dst_scr)
            dlb_ref[...] = jnp.zeros_like(dlb_ref)
            dgain_ref[...] = jnp.zeros_like(dgain_ref)

        m_all_v = m_ref[...]
        last_row = lax.broadcasted_iota(jnp.int32, (C, HEAD_DIM), 0) == C - 1

        def chunk(ci, carry):
            c = cb - 1 - ci
            rs = pl.ds(pl.multiple_of(c * C, C), C)
            for h in range(N_HEADS):
                sl = slice(h * HEAD_DIM, (h + 1) * HEAD_DIM)
                lbh = lb_ref[:, sl]
                gh = gain_ref[:, sl]
                st_t = st_ref[c, h]
                fz = fz_ref[rs, sl]
                iv = iv_ref[rs, sl]
                qz = qz_ref[rs, sl]
                gz = gz_ref[rs, sl]
                q = _hg_chunk_fwd(fz, iv, qz, lbh, m_all_v, mask_ref, st_t)
                o = q["o"]
                r = lax.rsqrt(jnp.mean(o * o, axis=-1, keepdims=True) + RMS_EPS)
                on = o * r
                sg = _sigmoid(gz)
                gate = gz * sg
                dy = dy_ref[rs, sl].astype(F32)
                dgain_ref[:, sl] += jnp.sum(dy * on * gate, axis=0, keepdims=True)
                dgz = (dy * on * gh) * (sg * (1.0 + gz * (1.0 - sg)))
                don = dy * gh * gate
                do = r * (don - on * jnp.mean(don * on, axis=-1, keepdims=True))
                da = _dot_nt(do, iv)
                dv = _dot_tn(q["a"], do)
                dqe = _dot_nn(do, st_t)
                dst_new = _dot_tn(do, q["qe"])
                dsp = dst_scr[h]
                dw = _dot_nn(iv, dsp)
                dv = dv + _dot_nt(q["w"], dsp)
                d_eb_last = jnp.sum(dsp * st_t, axis=0, keepdims=True)
                dst_scr[h] = dsp * q["eb_last"] + dst_new
                dad = da * mask_ref[N_LEV]
                dq = _dot_nn(dad, q["kk"])
                dk = _dot_tn(dad, q["qq"])
                dargs_q = []
                dargs_k = []
                for j in range(N_LEV):
                    daj = da * mask_ref[j]
                    dqj = _dot_nn(daj, q["ks"][j])
                    dkj = _dot_tn(daj, q["qs"][j])
                    dq = dq + dqj * q["e"][j * C:(j + 1) * C]
                    dk = dk + dkj * q["e"][(N_LEV + j) * C:(N_LEV + j + 1) * C]
                    dargs_q.append(dqj * q["qs"][j])
                    dargs_k.append(dkj * q["ks"][j])
                dq = dq + dqe * q["eb"]
                darg_b = dqe * q["qe"] + jnp.where(last_row, d_eb_last * q["eb_last"], 0.0)
                dk = dk + dw * q["ew"]
                darg_w = dw * q["w"]
                dall = jnp.concatenate(dargs_q + dargs_k + [darg_b, darg_w], axis=0)
                hi = dall.astype(BF16)
                lo = (dall - hi.astype(F32)).astype(BF16)
                dl2 = lax.dot_general(m_all_v, jnp.concatenate([hi, lo], axis=1), (((0,), (0,)), ((), ())),
                                      preferred_element_type=F32)
                dlf = dl2[:, :HEAD_DIM] + dl2[:, HEAD_DIM:]
                df = dlf / q["f"] - dk
                sig = q["sig"]
                dlb_ref[:, sl] += jnp.sum(df * (1.0 - sig), axis=0, keepdims=True)
                dfz = df * (1.0 - lbh) * (sig * (1.0 - sig))
                sq = q["sq"]
                dqz = dq * (sq * (1.0 + qz * (1.0 - sq)))
                dz_ref[rs, h * HEAD_DIM:(h + 1) * HEAD_DIM] = dfz.astype(BF16)
                dz_ref[rs, HG_W + h * HEAD_DIM:HG_W + (h + 1) * HEAD_DIM] = dv.astype(BF16)
                dz_ref[rs, 2 * HG_W + h * HEAD_DIM:2 * HG_W + (h + 1) * HEAD_DIM] = dqz.astype(BF16)
                dz_ref[rs, 3 * HG_W + h * HEAD_DIM:3 * HG_W + (h + 1) * HEAD_DIM] = dgz.astype(BF16)
            return carry

        lax.fori_loop(0, cb, chunk, 0)

    def col(c):
        return pl.BlockSpec((rows, HG_W), lambda i: (nblk - 1 - i, c))

    vec = pl.BlockSpec((1, HG_W), lambda i: (0, 0))
    return pl.pallas_call(
        body, name=name, grid=(nblk,),
        in_specs=[col(3), col(4), col(5), col(6), vec, vec,
                  pl.BlockSpec(m_all.shape, lambda i: (0, 0)), pl.BlockSpec(masks.shape, lambda i: (0, 0, 0)),
                  pl.BlockSpec((cb, N_HEADS, HEAD_DIM, HEAD_DIM), lambda i: (nblk - 1 - i, 0, 0, 0)),
                  col(1)],
        out_specs=[pl.BlockSpec((rows, 4 * HG_W), lambda i: (nblk - 1 - i, 0)), vec, vec],
        out_shape=[jax.ShapeDtypeStruct((S, 4 * HG_W), BF16), jax.ShapeDtypeStruct((1, HG_W), F32),
                   jax.ShapeDtypeStruct((1, HG_W), F32)],
        scratch_shapes=[pltpu.VMEM((N_HEADS, HEAD_DIM, HEAD_DIM), F32)],
        compiler_params=_cparams(("arbitrary",)),
    )(proj, proj, proj, proj, lb, gain, m_all, masks, states, d_cat)


def _cross_fwd(cq, ckv, name):
    S = cq.shape[0]
    n_mem = ckv.shape[0]
    tq = _blk(S, (256, 128))
    scale = CROSS_DIM ** -0.5

    def body(q_ref, kv_ref, o_ref):
        for h in range(CROSS_HEADS):
            sl = slice(h * CROSS_DIM, (h + 1) * CROSS_DIM)
            k = kv_ref[:, sl]
            v = kv_ref[:, D_MODEL + h * CROSS_DIM:D_MODEL + (h + 1) * CROSS_DIM]
            s = _dot_nt(q_ref[:, sl], k) * scale
            m = jnp.max(s, axis=-1, keepdims=True)
            p = jnp.exp(s - m)
            p = p / jnp.sum(p, axis=-1, keepdims=True)
            o_ref[:, sl] = _dot_nn(p, v).astype(BF16)

    return pl.pallas_call(
        body, name=name, grid=(S // tq,),
        in_specs=[pl.BlockSpec((tq, D_MODEL), lambda i: (i, 0)), pl.BlockSpec((n_mem, 2 * D_MODEL), lambda i: (0, 0))],
        out_specs=pl.BlockSpec((tq, D_MODEL), lambda i: (i, 0)),
        out_shape=jax.ShapeDtypeStruct((S, D_MODEL), BF16),
        compiler_params=_cparams(("parallel",)),
    )(cq, ckv)


def _cross_bwd(cq, ckv, d_o, name):
    S = cq.shape[0]
    n_mem = ckv.shape[0]
    tq = _blk(S, (256, 128))
    scale = CROSS_DIM ** -0.5

    def body(q_ref, kv_ref, do_ref, dq_ref, dkv_ref):
        @pl.when(pl.program_id(0) == 0)
        def _():
            dkv_ref[...] = jnp.zeros_like(dkv_ref)

        for h in range(CROSS_HEADS):
            sl = slice(h * CROSS_DIM, (h + 1) * CROSS_DIM)
            slv = slice(D_MODEL + h * CROSS_DIM, D_MODEL + (h + 1) * CROSS_DIM)
            q = q_ref[:, sl]
            k = kv_ref[:, sl]
            v = kv_ref[:, slv]
            do = do_ref[:, sl]
            s = _dot_nt(q, k) * scale
            m = jnp.max(s, axis=-1, keepdims=True)
            p = jnp.exp(s - m)
            p = p / jnp.sum(p, axis=-1, keepdims=True)
            dp = _dot_nt(do, v)
            ds = p * (dp - jnp.sum(dp * p, axis=-1, keepdims=True)) * scale
            dq_ref[:, sl] = _dot_nn(ds, k).astype(BF16)
            dkv_ref[:, sl] += _dot_tn(ds, q)
            dkv_ref[:, slv] += _dot_tn(p, do)

    row = pl.BlockSpec((tq, D_MODEL), lambda i: (i, 0))
    kv = pl.BlockSpec((n_mem, 2 * D_MODEL), lambda i: (0, 0))
    return pl.pallas_call(
        body, name=name, grid=(S // tq,),
        in_specs=[row, kv, row], out_specs=[row, kv],
        out_shape=[jax.ShapeDtypeStruct((S, D_MODEL), BF16), jax.ShapeDtypeStruct((n_mem, 2 * D_MODEL), F32)],
        compiler_params=_cparams(("arbitrary",)),
    )(cq, ckv, d_o)


FF_BLOCK = 512


def _mm_gate_up(hf, w_gu, name):
    S, K = hf.shape
    bm = _blk(S, (1024, 512, 256, 128))
    nj = D_FF // FF_BLOCK

    def body(a_ref, bg_ref, bu_ref, g_ref, u_ref, act_ref):
        a = a_ref[...]
        g = _dot_nn(a, bg_ref[...])
        u = _dot_nn(a, bu_ref[...])
        g_ref[...] = g.astype(BF16)
        u_ref[...] = u.astype(BF16)
        act_ref[...] = ((g * _sigmoid(g)) * u).astype(BF16)

    out = pl.BlockSpec((bm, FF_BLOCK), lambda i, j: (i, j))
    sds = jax.ShapeDtypeStruct((S, D_FF), BF16)
    return pl.pallas_call(
        body, name=name, grid=(S // bm, nj),
        in_specs=[pl.BlockSpec((bm, K), lambda i, j: (i, 0)), pl.BlockSpec((K, FF_BLOCK), lambda i, j: (0, j)),
                  pl.BlockSpec((K, FF_BLOCK), lambda i, j: (0, j + nj))],
        out_specs=[out, out, out], out_shape=[sds, sds, sds],
        compiler_params=_cparams(("parallel", "arbitrary")),
    )(hf, w_gu, w_gu)


def _mm_down_x(dy, w_down, g, u, name):
    S, K = dy.shape
    bm = _blk(S, (1024, 512, 256, 128))
    nj = D_FF // FF_BLOCK

    def body(a_ref, b_ref, g_ref, u_ref, dg_ref, du_ref):
        da = _dot_nt(a_ref[...], b_ref[...])
        gv = g_ref[...].astype(F32)
        sg = _sigmoid(gv)
        dg_ref[...] = (da * u_ref[...].astype(F32) * (sg * (1.0 + gv * (1.0 - sg)))).astype(BF16)
        du_ref[...] = (da * (gv * sg)).astype(BF16)

    tile = pl.BlockSpec((bm, FF_BLOCK), lambda i, j: (i, j))
    sds = jax.ShapeDtypeStruct((S, D_FF), BF16)
    return pl.pallas_call(
        body, name=name, grid=(S // bm, nj),
        in_specs=[pl.BlockSpec((bm, K), lambda i, j: (i, 0)), pl.BlockSpec((FF_BLOCK, K), lambda i, j: (j, 0)),
                  tile, tile],
        out_specs=[tile, tile], out_shape=[sds, sds],
        compiler_params=_cparams(("parallel", "arbitrary")),
    )(dy, w_down, g, u)


def _loss_head(y, target, name):
    S, D = y.shape
    tm = _blk(S, (256, 128, 64, 8))

    def body(y_ref, t_ref, dy_ref, l_ref):
        diff = y_ref[...] - t_ref[...]
        dy_ref[...] = diff * (1.0 / D)
        sq = (diff * diff) * (0.5 / D)
        part = jnp.sum(sq.reshape(tm // 8, 8, D), axis=0)

        @pl.when(pl.program_id(0) == 0)
        def _():
            l_ref[...] = part

        @pl.when(pl.program_id(0) > 0)
        def _():
            l_ref[...] += part

    row = pl.BlockSpec((tm, D), lambda i: (i, 0))
    return pl.pallas_call(
        body, name=name, grid=(S // tm,), in_specs=[row, row],
        out_specs=[row, pl.BlockSpec((8, D), lambda i: (0, 0))],
        out_shape=[jax.ShapeDtypeStruct((S, D), F32), jax.ShapeDtypeStruct((8, D), F32)],
        compiler_params=_cparams(("arbitrary",)),
    )(y, target)


def _layer_fwd(x0, mem, g, w, lb, hg_gain, bias, tag):
    S = x0.shape[0]
    h0 = _rms_fwd(x0, g[0], f"rms0_{tag}")
    proj = _mm(h0, w["w_in"], "nn", F32, f"mm_in_{tag}")
    state = None
    cat = lax.empty((S, D_MODEL), BF16)
    for bi, d in enumerate(DILATIONS):
        last = bi == len(DILATIONS) - 1
        out = _attn_fwd(proj, bias[bi], state, d, last, cat if last else None, f"attn_fwd{d}_{tag}")
        if last:
            cat, lse = out
        else:
            state = tuple(out)
    cat, states = _hgrn_fwd(proj, lb, hg_gain, cat, f"hgrn_fwd_{tag}")
    mix = _mm(cat, w["w_out"], "nn", F32, f"mm_out_{tag}")
    x1, hc = _norm_residual(x0, mix, g[1], g[2], f"res1_{tag}")
    cq = _mm(hc, w["w_cq"], "nn", BF16, f"mm_cq_{tag}")
    mn = _rms_fwd(mem, g[3], f"rms3_{tag}")
    ckv = _mm(mn, w["w_ckv"], "nn", BF16, f"mm_ckv_{tag}")
    cop = _cross_fwd(cq, ckv, f"cross_fwd_{tag}")
    co = _mm(cop, w["w_co"], "nn", F32, f"mm_co_{tag}")
    x2, hf = _norm_residual(x1, co, g[4], g[5], f"res2_{tag}")
    gate, up, act = _mm_gate_up(hf, w["w_gate_up"], f"mm_gu_{tag}")
    y = _mm(act, w["w_down"], "nn", F32, f"mm_down_{tag}")
    x3 = _norm_residual(x2, y, g[6], None, f"res3_{tag}")
    saved = dict(x0=x0, h0=h0, proj=proj, cat=cat, lse=lse, states=states, mix=mix, x1=x1, hc=hc, cq=cq, mn=mn,
                 ckv=ckv, cop=cop, co=co, x2=x2, hf=hf, gate=gate, up=up, act=act, y=y)
    return x3, saved


def _layer_bwd(dx3, sv, mem, g, w, lb, hg_gain, bias, tag):
    dg = [None] * N_NORMS
    gw = {}
    dy, dg[6] = _rms_bwd(sv["y"], g[6], dx3, None, BF16, f"rmsb6_{tag}")
    dgu = list(_mm_down_x(dy, w["w_down"], sv["gate"], sv["up"], f"mmb_down_x_{tag}"))
    gw["w_down"] = _mm(sv["act"], dy, "tn", BF16, f"mmb_down_w_{tag}")
    gw["w_gate_up"] = _mm(sv["hf"], dgu, "tn", BF16, f"mmb_gu_w_{tag}")
    d_hf = _mm(dgu, w["w_gate_up"], "nt", F32, f"mmb_gu_x_{tag}")
    dx2, dg[5] = _rms_bwd(sv["x2"], g[5], d_hf, dx3, F32, f"rmsb5_{tag}")
    d_co, dg[4] = _rms_bwd(sv["co"], g[4], dx2, None, BF16, f"rmsb4_{tag}")
    d_cop = _mm(d_co, w["w_co"], "nt", BF16, f"mmb_co_x_{tag}")
    gw["w_co"] = _mm(sv["cop"], d_co, "tn", BF16, f"mmb_co_w_{tag}")
    d_cq, d_ckv = _cross_bwd(sv["cq"], sv["ckv"], d_cop, f"cross_bwd_{tag}")
    gw["w_cq"] = _mm(sv["hc"], d_cq, "tn", BF16, f"mmb_cq_w_{tag}")
    d_hc = _mm(d_cq, w["w_cq"], "nt", F32, f"mmb_cq_x_{tag}")
    gw["w_ckv"] = _mm(sv["mn"], d_ckv, "tn", BF16, f"mmb_ckv_w_{tag}")
    d_mn = _mm(d_ckv, w["w_ckv"], "nt", F32, f"mmb_ckv_x_{tag}")
    _, dg[3] = _rms_bwd(mem, g[3], d_mn, None, BF16, f"rmsb3_{tag}")
    dx1, dg[2] = _rms_bwd(sv["x1"], g[2], d_hc, dx2, F32, f"rmsb2_{tag}")
    d_mix, dg[1] = _rms_bwd(sv["mix"], g[1], dx1, None, BF16, f"rmsb1_{tag}")
    d_cat = _mm(d_mix, w["w_out"], "nt", BF16, f"mmb_out_x_{tag}")
    gw["w_out"] = _mm(sv["cat"], d_mix, "tn", BF16, f"mmb_out_w_{tag}")
    acc = None
    dbias = []
    for bi, d in enumerate(DILATIONS):
        last = bi == len(DILATIONS) - 1
        dq, dk, dv, db = _attn_bwd(sv["proj"], bias[bi], sv["cat"], d_cat, sv["lse"], acc, d, last,
                                   f"attn_bwd{d}_{tag}")
        acc = (dq, dk, dv)
        dbias.append(db)
    d_hz, dlb, dgain = _hgrn_bwd(sv["proj"], lb, hg_gain, sv["states"], d_cat, f"hgrn_bwd_{tag}")
    parts = [acc[0], acc[1], acc[2], d_hz]
    gw["w_in"] = _mm(sv["h0"], parts, "tn", BF16, f"mmb_in_w_{tag}")
    d_h0 = _mm(parts, w["w_in"], "nt", F32, f"mmb_in_x_{tag}")
    dx0, dg[0] = _rms_bwd(sv["x0"], g[0], d_h0, dx1, F32, f"rmsb0_{tag}")
    return dx0, gw, jnp.stack(dg), dlb, dgain, jnp.stack(dbias)


def _lb_all(lb_logits):
    p = jax.nn.softmax(lb_logits.astype(F32), axis=0)
    return jnp.cumsum(p, axis=0) - p


def _local_step(x, mem, target, rel_bias, lb_logits, norm_gains, hg_norm, weights_of_layer, on_layer_grads=None):
    L = lb_logits.shape[0]
    bias, bias_vjp = jax.vjp(_bias_tables, rel_bias)
    lb_all, lb_vjp = jax.vjp(_lb_all, lb_logits)
    gains = norm_gains.reshape(L, N_NORMS, 1, D_MODEL)
    saved = []
    h = x
    ws = []
    for l in range(L):
        w = weights_of_layer(l)
        ws.append(w)
        h, sv = _layer_fwd(h, mem, gains[l], w, lb_all[l:l + 1], hg_norm[l:l + 1], bias, f"l{l}")
        saved.append(sv)
    dy, lparts = _loss_head(h, target, "loss_head")
    loss = jnp.sum(lparts)
    d_gains, d_lb, d_hg, gws = [None] * L, [None] * L, [None] * L, [None] * L
    d_bias = jnp.zeros_like(bias)
    dh = dy
    for l in reversed(range(L)):
        dh, gw, dgl, dlbl, dhgl, dbl = _layer_bwd(dh, saved[l], mem, gains[l], ws[l], lb_all[l:l + 1],
                                                  hg_norm[l:l + 1], bias, f"l{l}")
        d_gains[l], d_lb[l], d_hg[l], gws[l] = dgl.reshape(N_NORMS, D_MODEL), dlbl[0], dhgl[0], gw
        d_bias = d_bias + dbl
        if on_layer_grads is not None:
            gws[l] = on_layer_grads(l, gw)
    (d_rel_bias,) = bias_vjp(d_bias)
    (d_lb_logits,) = lb_vjp(jnp.stack(d_lb))
    return loss, dh, gws, d_rel_bias, d_lb_logits, jnp.stack(d_gains), jnp.stack(d_hg)


def kernel(x, mem, rel_bias, lb_logits, norm_gains, w_in, hg_norm, w_out, w_cq, w_ckv, w_co, w_gate_up, w_down, loss_target, m_rel_bias, m_lb_logits, m_norm_gains, m_w_in, m_hg_norm, m_w_out, m_w_cq, m_w_ckv, m_w_co, m_w_gate_up, m_w_down, v_rel_bias, v_lb_logits, v_norm_gains, v_w_in, v_hg_norm, v_w_out, v_w_cq, v_w_ckv, v_w_co, v_w_gate_up, v_w_down):
    return _train_step(
        x, mem, loss_target,
        dict(rel_bias=rel_bias, lb_logits=lb_logits, norm_gains=norm_gains, hg_norm=hg_norm, w_in=w_in, w_out=w_out,
             w_cq=w_cq, w_ckv=w_ckv, w_co=w_co, w_gate_up=w_gate_up, w_down=w_down),
        dict(rel_bias=m_rel_bias, lb_logits=m_lb_logits, norm_gains=m_norm_gains, hg_norm=m_hg_norm, w_in=m_w_in,
             w_out=m_w_out, w_cq=m_w_cq, w_ckv=m_w_ckv, w_co=m_w_co, w_gate_up=m_w_gate_up, w_down=m_w_down),
        dict(rel_bias=v_rel_bias, lb_logits=v_lb_logits, norm_gains=v_norm_gains, hg_norm=v_hg_norm, w_in=v_w_in,
             w_out=v_w_out, w_cq=v_w_cq, w_ckv=v_w_ckv, w_co=v_w_co, w_gate_up=v_w_gate_up, w_down=v_w_down))


BIG = ("w_in", "w_out", "w_cq", "w_ckv", "w_co", "w_gate_up", "w_down")
SHARD_AXIS = dict(w_in=1, w_out=0, w_cq=0, w_ckv=1, w_co=0, w_gate_up=1, w_down=0)
NB = len(BIG)
OUT_ORDER = ("rel_bias", "lb_logits", "norm_gains", "w_in", "hg_norm", "w_out", "w_cq", "w_ckv", "w_co",
             "w_gate_up", "w_down")
ANY = pl.BlockSpec(memory_space=pl.ANY)


def _place():
    x, y, c = lax.axis_index("x"), lax.axis_index("y"), lax.axis_index("c")
    chips = [(1 - x, y), (x, 1 - y), (1 - x, 1 - y)]
    return x, y, c, chips


def _remote(src, dst, send_sem, recv_sem, to):
    return pltpu.make_async_remote_copy(src_ref=src, dst_ref=dst, send_sem=send_sem, recv_sem=recv_sem,
                                        device_id=to, device_id_type=MESH_ID)


def _half_region(ref, axis, chip, half, lead=()):
    R, C = ref.shape[-2:]
    if axis == 0:
        rs = R // N_CHIPS
        return ref.at[(*lead, pl.ds(chip * rs + half * (rs // 2), rs // 2), slice(None))]
    cs = C // N_CHIPS
    return ref.at[(*lead, pl.ds(half * (R // 2), R // 2), pl.ds(chip * cs, cs))]


def _cast_place(n, w, l, where, name):
    _, rs, cs = w.shape
    tr = _blk(rs, (256, 128, 64, 32, 16))
    nt = rs // tr
    if SHARD_AXIS[n] == 0:
        full = (rs * N_CHIPS, cs)
        o_spec = pl.BlockSpec((tr, cs), lambda i, wh: (wh[0] * nt + i, 0))
    else:
        full = (rs, cs * N_CHIPS)
        o_spec = pl.BlockSpec((tr, cs), lambda i, wh: (i, wh[0]))

    def body(wh, w_ref, o_ref):
        o_ref[...] = w_ref[...].astype(BF16)

    return pl.pallas_call(
        body, name=name,
        grid_spec=pltpu.PrefetchScalarGridSpec(
            num_scalar_prefetch=1, grid=(nt,),
            in_specs=[pl.BlockSpec((None, tr, cs), lambda i, wh: (l, i, 0))], out_specs=o_spec),
        out_shape=jax.ShapeDtypeStruct(full, BF16), compiler_params=_cparams(("parallel",)),
    )(where, w)


def _allgather_weights(placed, name):
    names = list(BIG)

    def body(*refs):
        outs = refs[NB:2 * NB]
        send, recv = refs[2 * NB:]
        x, y, c, chips = _place()
        me = 2 * x + y
        sib = (x, y, 1 - c)
        sends = []
        for j, chip in enumerate(chips):
            for wi, n in enumerate(names):
                reg = _half_region(outs[wi], SHARD_AXIS[n], me, c)
                cp = _remote(reg, reg, send.at[j * NB + wi], recv.at[j * NB + wi], (*chip, c))
                cp.start()
                sends.append(cp)
        for j, chip in enumerate(chips):
            them = 2 * chip[0] + chip[1]
            for wi, n in enumerate(names):
                reg = _half_region(outs[wi], SHARD_AXIS[n], them, c)
                _remote(reg, reg, send.at[j * NB + wi], recv.at[j * NB + wi], (*chip, c)).wait_recv()
                cp = _remote(reg, reg, send.at[(3 + j) * NB + wi], recv.at[(3 + j) * NB + wi], sib)
                cp.start()
                sends.append(cp)
        for j, chip in enumerate(chips):
            them = 2 * chip[0] + chip[1]
            for wi, n in enumerate(names):
                reg = _half_region(outs[wi], SHARD_AXIS[n], them, 1 - c)
                _remote(reg, reg, send.at[(3 + j) * NB + wi], recv.at[(3 + j) * NB + wi], sib).wait_recv()
        for cp in sends:
            cp.wait_send()

    outs = pl.pallas_call(
        body, name=name, in_specs=[ANY] * NB, out_specs=[ANY] * NB,
        out_shape=[jax.ShapeDtypeStruct(placed[n].shape, BF16) for n in names],
        scratch_shapes=[pltpu.SemaphoreType.DMA((6 * NB,)), pltpu.SemaphoreType.DMA((6 * NB,))],
        input_output_aliases={wi: wi for wi in range(NB)},
    )(*[placed[n] for n in names])
    return dict(zip(names, outs))


def _allgather_gains(g_shard, name):
    A, Cs = g_shard.shape

    def body(in_ref, out_ref, send, recv, local):
        x, y, c, chips = _place()
        me = 2 * x + y
        mine = pltpu.make_async_copy(in_ref, out_ref.at[me], local)
        mine.start()
        cps = [_remote(in_ref, out_ref.at[me], send.at[j], recv.at[j], (*chip, c)) for j, chip in enumerate(chips)]
        for cp in cps:
            cp.start()
        for j, chip in enumerate(chips):
            them = 2 * chip[0] + chip[1]
            _remote(in_ref, out_ref.at[them], send.at[j], recv.at[j], (*chip, c)).wait_recv()
        for cp in cps:
            cp.wait_send()
        mine.wait()

    return pl.pallas_call(
        body, name=name, in_specs=[ANY], out_specs=ANY,
        out_shape=jax.ShapeDtypeStruct((N_CHIPS, A, Cs), F32),
        scratch_shapes=[pltpu.SemaphoreType.DMA((3,)), pltpu.SemaphoreType.DMA((3,)), pltpu.SemaphoreType.DMA],
    )(g_shard)


def _half_shape(n, shape):
    R, C = shape
    if SHARD_AXIS[n] == 0:
        return (N_CHIPS, R // N_CHIPS // 2, C)
    return (R // 2, C)


def _as_halves(n, g):
    R, C = g.shape
    if SHARD_AXIS[n] == 0:
        return g.reshape(N_CHIPS, R // N_CHIPS, C)
    return g


def _my_half(n, ref, half):
    if SHARD_AXIS[n] == 0:
        hs = ref.shape[1] // 2
        return ref.at[:, pl.ds(half * hs, hs), :]
    hs = ref.shape[0] // 2
    return ref.at[pl.ds(half * hs, hs), :]


def _swap_sibling_halves(gw, name):
    names = list(BIG)

    def body(*refs):
        ins = refs[:NB]
        outs = refs[NB:2 * NB]
        send, recv = refs[2 * NB:]
        x, y, c, _ = _place()
        sib = (x, y, 1 - c)
        cps = []
        for wi, n in enumerate(names):
            cp = _remote(_my_half(n, ins[wi], 1 - c), outs[wi], send.at[wi], recv.at[wi], sib)
            cp.start()
            cps.append(cp)
        for cp in cps:
            cp.wait()

    shapes = []
    for n in names:
        g = gw[n]
        shapes.append((g.shape[0], g.shape[1] // 2, g.shape[2]) if g.ndim == 3 else (g.shape[0] // 2, g.shape[1]))
    outs = pl.pallas_call(
        body, name=name, in_specs=[ANY] * NB, out_specs=[ANY] * NB,
        out_shape=[jax.ShapeDtypeStruct(s, BF16) for s in shapes],
        scratch_shapes=[pltpu.SemaphoreType.DMA((NB,)), pltpu.SemaphoreType.DMA((NB,))],
    )(*[gw[n] for n in names])
    return dict(zip(names, outs))


def _pair_sum(n, g, other, c_arr, name):
    if g.ndim == 3:
        nc, rs, C = g.shape
        hs = rs // 2
        tr = _blk(hs, (256, 128, 64, 32, 16))
        nt = hs // tr
        grid = (nc, nt)
        g_spec = pl.BlockSpec((1, tr, C), lambda k, i, c_ref: (k, c_ref[0] * nt + i, 0))
        o_spec = pl.BlockSpec((1, tr, C), lambda k, i, c_ref: (k, i, 0))
    else:
        R, C = g.shape
        hs = R // 2
        cs = C // N_CHIPS
        tr = _blk(hs, (256, 128, 64, 32, 16))
        nt = hs // tr
        grid = (nt, N_CHIPS)
        g_spec = pl.BlockSpec((tr, cs), lambda i, k, c_ref: (c_ref[0] * nt + i, k))
        o_spec = pl.BlockSpec((tr, cs), lambda i, k, c_ref: (i, k))

    def body(c_ref, g_ref, o_ref, out_ref):
        out_ref[...] = (g_ref[...].astype(F32) + o_ref[...].astype(F32)).astype(BF16)

    return pl.pallas_call(
        body, name=name,
        grid_spec=pltpu.PrefetchScalarGridSpec(num_scalar_prefetch=1, grid=grid, in_specs=[g_spec, o_spec],
                                               out_specs=o_spec),
        out_shape=jax.ShapeDtypeStruct(other.shape, BF16),
        compiler_params=_cparams(("parallel", "parallel")),
    )(c_arr, g, other)


def _scatter_to_owners(pairs, name):
    names = list(BIG)

    def piece(n, ref, chip):
        if SHARD_AXIS[n] == 0:
            return ref.at[chip]
        cs = ref.shape[1] // N_CHIPS
        return ref.at[:, pl.ds(chip * cs, cs)]

    def body(*refs):
        ins = refs[:NB]
        outs = refs[NB:2 * NB]
        send, recv = refs[2 * NB:]
        x, y, c, chips = _place()
        me = 2 * x + y
        sends = []
        for j, chip in enumerate(chips):
            them = 2 * chip[0] + chip[1]
            for wi, n in enumerate(names):
                cp = _remote(piece(n, ins[wi], them), outs[wi].at[j], send.at[j * NB + wi], recv.at[j * NB + wi],
                             (*chip, c))
                cp.start()
                sends.append(cp)
        for j, chip in enumerate(chips):
            for wi, n in enumerate(names):
                _remote(piece(n, ins[wi], me), outs[wi].at[j], send.at[j * NB + wi], recv.at[j * NB + wi],
                        (*chip, c)).wait_recv()
        for cp in sends:
            cp.wait_send()

    shapes = []
    for n in names:
        p = pairs[n]
        shapes.append((3,) + (p.shape[1:] if p.ndim == 3 else (p.shape[0], p.shape[1] // N_CHIPS)))
    outs = pl.pallas_call(
        body, name=name, in_specs=[ANY] * NB, out_specs=[ANY] * NB,
        out_shape=[jax.ShapeDtypeStruct(s, BF16) for s in shapes],
        scratch_shapes=[pltpu.SemaphoreType.DMA((3 * NB,)), pltpu.SemaphoreType.DMA((3 * NB,))],
    )(*[pairs[n] for n in names])
    return dict(zip(names, outs))


def _sum_final(n, pair, q, stack, l, where, name):
    hs, cc = q.shape[1:]
    tr = _blk(hs, (256, 128, 64, 32, 16))
    nt = hs // tr
    if SHARD_AXIS[n] == 0:
        p_spec = pl.BlockSpec((None, tr, cc), lambda i, wh: (wh[0], i, 0))
    else:
        p_spec = pl.BlockSpec((tr, cc), lambda i, wh: (i, wh[0]))

    def body(wh, p_ref, q_ref, stack_any, o_ref):
        o_ref[...] = ((p_ref[...].astype(F32) + q_ref[0].astype(F32)) + q_ref[1].astype(F32)) + q_ref[2].astype(F32)

    return pl.pallas_call(
        body, name=name,
        grid_spec=pltpu.PrefetchScalarGridSpec(
            num_scalar_prefetch=1, grid=(nt,),
            in_specs=[p_spec, pl.BlockSpec((3, tr, cc), lambda i, wh: (0, i, 0)), ANY],
            out_specs=pl.BlockSpec((None, tr, cc), lambda i, wh: (l, wh[1] * nt + i, 0))),
        out_shape=jax.ShapeDtypeStruct(stack.shape, F32),
        input_output_aliases={3: 0},
        compiler_params=_cparams(("parallel",)),
    )(where, pair, q, stack)


def _sum_slots(q, name):
    K, A, B = q.shape
    tr = _blk(A, (256, 128, 64, 32, 16, 8))

    def body(q_ref, o_ref):
        s = q_ref[0].astype(F32)
        for k in range(1, K):
            s = s + q_ref[k].astype(F32)
        o_ref[...] = s

    return pl.pallas_call(
        body, name=name, grid=(A // tr,),
        in_specs=[pl.BlockSpec((K, tr, B), lambda i: (0, i, 0))], out_specs=pl.BlockSpec((tr, B), lambda i: (i, 0)),
        out_shape=jax.ShapeDtypeStruct((A, B), F32), compiler_params=_cparams(("parallel",)),
    )(q)


def _finish_grads(stacks, small, name):
    names = list(BIG)
    A = small.shape[0]
    nh = NB

    def body(*refs):
        small_ref = refs[nh]
        outs = refs[nh + 1:nh + 1 + NB]
        slots = refs[nh + 1 + NB]
        send, recv, local = refs[nh + 2 + NB:]
        x, y, c, _ = _place()
        sib = (x, y, 1 - c)
        me = 4 * x + 2 * y + c
        locals_, sends = [], []
        for wi, n in enumerate(names):
            hr = outs[wi].shape[1] // 2
            reg = outs[wi].at[:, pl.ds(c * hr, hr), :]
            cp = _remote(reg, reg, send.at[wi], recv.at[wi], sib)
            cp.start()
            sends.append(cp)
        cp = pltpu.make_async_copy(small_ref, slots.at[me], local)
        cp.start()
        locals_.append(cp)
        peers = []
        for dx in range(2):
            for dy in range(2):
                for dc in range(2):
                    if dx or dy or dc:
                        peers.append((dx, dy, dc))
        for j, (dx, dy, dc) in enumerate(peers):
            to = (jnp.bitwise_xor(x, dx), jnp.bitwise_xor(y, dy), jnp.bitwise_xor(c, dc))
            cp = _remote(small_ref, slots.at[me], send.at[nh + j], recv.at[nh + j], to)
            cp.start()
            sends.append(cp)
        for wi, n in enumerate(names):
            hr = outs[wi].shape[1] // 2
            reg = outs[wi].at[:, pl.ds((1 - c) * hr, hr), :]
            _remote(reg, reg, send.at[wi], recv.at[wi], sib).wait_recv()
        for j, (dx, dy, dc) in enumerate(peers):
            frm = 4 * jnp.bitwise_xor(x, dx) + 2 * jnp.bitwise_xor(y, dy) + jnp.bitwise_xor(c, dc)
            _remote(small_ref, slots.at[frm], send.at[nh + j], recv.at[nh + j], sib).wait_recv()
        for cp in sends:
            cp.wait_send()
        for cp in locals_:
            cp.wait()

    res = pl.pallas_call(
        body, name=name, in_specs=[ANY] * (nh + 1), out_specs=[ANY] * (NB + 1),
        out_shape=[jax.ShapeDtypeStruct(stacks[n].shape, F32) for n in names]
        + [jax.ShapeDtypeStruct((8, A, LANES), F32)],
        scratch_shapes=[pltpu.SemaphoreType.DMA((nh + 7,)), pltpu.SemaphoreType.DMA((nh + 7,)),
                        pltpu.SemaphoreType.DMA],
        input_output_aliases={wi: wi for wi in range(NB)},
    )(*[stacks[n] for n in names], small)
    return dict(zip(names, res[:NB])), res[NB]


def _adamw(w, g, m, v, name):
    shape = w.shape
    if w.ndim == 2:
        w, g, m, v = (t.reshape((1,) + shape) for t in (w, g, m, v))
    L, A, B = w.shape
    tr = _blk(A, (128, 64, 32, 16, 8))
    c1 = 1.0 - ADAM_B1 ** ADAM_STEP
    c2 = 1.0 - ADAM_B2 ** ADAM_STEP

    def body(w_ref, g_ref, m_ref, v_ref, d_ref, nm_ref, nv_ref):
        gv = g_ref[...]
        nm = ADAM_B1 * m_ref[...] + (1.0 - ADAM_B1) * gv
        nv = ADAM_B2 * v_ref[...] + (1.0 - ADAM_B2) * (gv * gv)
        m_hat = nm / c1
        v_hat = nv / c2
        d_ref[...] = -ADAM_LR * (m_hat / (jnp.sqrt(v_hat) + ADAM_EPS) + ADAM_WD * w_ref[...])
        nm_ref[...] = nm
        nv_ref[...] = nv

    spec = pl.BlockSpec((1, tr, B), lambda l, i: (l, i, 0))
    sds = jax.ShapeDtypeStruct((L, A, B), F32)
    outs = pl.pallas_call(
        body, name=name, grid=(L, A // tr), in_specs=[spec] * 4, out_specs=[spec] * 3, out_shape=[sds] * 3,
        compiler_params=_cparams(("parallel", "parallel")),
    )(w, g, m, v)
    return tuple(o.reshape(shape) for o in outs)


SMALL_ROWS = 520


def _train_step(x, mem, target, w, m, v):
    L = w["lb_logits"].shape[0]
    cx, cy, cc = lax.axis_index("x"), lax.axis_index("y"), lax.axis_index("c")
    me = 2 * cx + cy
    c_arr = jnp.reshape(cc, (1,)).astype(jnp.int32)
    where = jnp.stack([me, cc]).astype(jnp.int32)

    full = []
    for l in range(L):
        placed = {n: _cast_place(n, w[n], l, where, f"cast_{n}_l{l}") for n in BIG}
        full.append(_allgather_weights(placed, f"allgather_l{l}"))
    gs = _allgather_gains(w["norm_gains"].reshape(L * N_NORMS, -1), "allgather_gains")
    gains = jnp.transpose(gs, (1, 0, 2)).reshape(L, N_NORMS, D_MODEL)

    stacks = {n: lax.empty(w[n].shape, F32) for n in BIG}

    def reduce_layer(l, gw):
        views = {n: _as_halves(n, gw[n]) for n in BIG}
        theirs = _swap_sibling_halves(views, f"swap_halves_l{l}")
        pairs = {n: _pair_sum(n, views[n], theirs[n], c_arr, f"pair_sum_{n}_l{l}") for n in BIG}
        slots = _scatter_to_owners(pairs, f"scatter_l{l}")
        for n in BIG:
            stacks[n] = _sum_final(n, pairs[n], slots[n], stacks[n], l, where, f"sum_chips_{n}_l{l}")
        return None

    loss, dx, _, d_rb, d_lb, d_gains, d_hg = _local_step(
        x[0], mem[0], target[0], w["rel_bias"], w["lb_logits"], gains, w["hg_norm"], lambda l: full[l], reduce_layer)

    flat = jnp.concatenate([d_rb.reshape(-1), d_lb.reshape(-1), d_hg.reshape(-1), d_gains.reshape(-1)])
    small = jnp.pad(flat, (0, SMALL_ROWS * LANES - flat.shape[0])).reshape(SMALL_ROWS, LANES)
    grads, slots = _finish_grads(stacks, small, "finish_grads")
    tot = _sum_slots(slots, "sum_small").reshape(-1)
    n_rb, n_lb = d_rb.size, d_lb.size
    grads["rel_bias"] = tot[:n_rb].reshape(d_rb.shape)
    grads["lb_logits"] = tot[n_rb:n_rb + n_lb].reshape(d_lb.shape)
    grads["hg_norm"] = tot[n_rb + n_lb:n_rb + 2 * n_lb].reshape(d_hg.shape)
    g_full = tot[n_rb + 2 * n_lb:n_rb + 2 * n_lb + d_gains.size].reshape(d_gains.shape)
    cs = D_MODEL // N_CHIPS
    grads["norm_gains"] = lax.dynamic_slice_in_dim(g_full, me * cs, cs, axis=2)

    delta, new_m, new_v = {}, {}, {}
    for n in OUT_ORDER:
        delta[n], new_m[n], new_v[n] = _adamw(w[n], grads[n], m[n], v[n], f"adamw_{n}")
    loss = lax.psum(loss, ("x", "y", "c"))
    return (loss, dx[None], *[grads[n] for n in OUT_ORDER], *[delta[n] for n in OUT_ORDER],
            *[new_m[n] for n in OUT_ORDER], *[new_v[n] for n in OUT_ORDER])
```

```python
import functools
import math

import numpy as np
import jax
import jax.numpy as jnp
from jax import lax
from jax.experimental import pallas as pl
from jax.experimental.pallas import tpu as pltpu

F32 = jnp.float32
BF16 = jnp.bfloat16
MESH_ID = pl.DeviceIdType.MESH

D_MODEL = 2048
HEAD_DIM = 128
N_HEADS = 8
ATTN_W = 1024
HG_W = 1024
HG_CHUNK = 64
Q_BLOCK = 128
DILATIONS = (16, 4, 1)
REL_BUCKETS = 32
REL_MAX_DIST = 2048
CROSS_HEADS = 4
CROSS_DIM = 512
D_FF = 5632
RMS_EPS = 1e-6
NEG_INF = -1e30
N_NORMS = 7

ADAM_LR = 0.001
ADAM_B1 = 0.9
ADAM_B2 = 0.999
ADAM_EPS = 1e-08
ADAM_WD = 0.01
ADAM_STEP = 10

VMEM_LIMIT_V7X = 56 * 1024 * 1024
LANES = 128
N_CHIPS = 4


def _cparams(sem=None):
    if sem is None:
        return pltpu.CompilerParams(vmem_limit_bytes=VMEM_LIMIT_V7X)
    return pltpu.CompilerParams(dimension_semantics=sem, vmem_limit_bytes=VMEM_LIMIT_V7X)


def _blk(n, prefs):
    for p in prefs:
        if p <= n and n % p == 0:
            return p
    return n


def _dot(a, b, dims):
    return lax.dot_general(a.astype(BF16), b.astype(BF16), (dims, ((), ())), preferred_element_type=F32)


def _dot_nn(a, b):
    return _dot(a, b, ((1,), (0,)))


def _dot_nt(a, b):
    return _dot(a, b, ((1,), (1,)))


def _dot_tn(a, b):
    return _dot(a, b, ((0,), (0,)))


def _sigmoid(x):
    return 1.0 / (1.0 + jnp.exp(-x))


def _mm(a_list, b_list, mode, out_dtype, name):
    if not isinstance(a_list, (list, tuple)):
        a_list = [a_list]
    if not isinstance(b_list, (list, tuple)):
        b_list = [b_list]
    na, nb = len(a_list), len(b_list)
    deep = (2048, 2816, 1024, 1408, 512, 256, 128)
    if mode == "nn":
        M, K = a_list[0].shape
        N = b_list[0].shape[1]
        bm, bn, bk = _blk(M, (1024, 512, 256, 128)), _blk(N, (1024, 512, 256, 128)), _blk(K, deep)
    elif mode == "nt":
        M = a_list[0].shape[0]
        K = sum(a.shape[1] for a in a_list)
        N = b_list[0].shape[0]
        bm, bn = _blk(M, (1024, 512, 256, 128)), _blk(N, (1024, 512, 256, 128))
        bk = _blk(math.gcd(*[a.shape[1] for a in a_list]), deep)
    else:
        K, M = a_list[0].shape
        N = sum(b.shape[1] for b in b_list)
        bn = _blk(math.gcd(*[b.shape[1] for b in b_list]), (1024, 2816, 512, 256, 128))
        wide = bn > 1024
        bm = _blk(M, (512, 256, 128) if wide else (1024, 1408, 512, 256, 128))
        bk = _blk(K, (1024, 512, 256, 128) if wide else (2048, 1024, 512, 256, 128))
    nk = K // bk
    grid = (M // bm, N // bn, nk)

    if mode == "nn":
        a_specs = [pl.BlockSpec((bm, bk), lambda i, j, k: (i, k))]
        b_specs = [pl.BlockSpec((bk, bn), lambda i, j, k: (k, j))]
        dims = ((1,), (0,))
    elif mode == "nt":
        a_specs, off = [], 0
        for a in a_list:
            n_i = a.shape[1] // bk
            a_specs.append(pl.BlockSpec((bm, bk), functools.partial(
                lambda i, j, k, off, n_i: (i, jnp.clip(k - off, 0, n_i - 1)), off=off, n_i=n_i)))
            off += n_i
        b_specs = [pl.BlockSpec((bn, bk), lambda i, j, k: (j, k))]
        dims = ((1,), (1,))
    else:
        a_specs = [pl.BlockSpec((bk, bm), lambda i, j, k: (k, i))]
        b_specs, off = [], 0
        for b in b_list:
            n_j = b.shape[1] // bn
            b_specs.append(pl.BlockSpec((bk, bn), functools.partial(
                lambda i, j, k, off, n_j: (jnp.where((j >= off) & (j < off + n_j), k, 0),
                                           jnp.clip(j - off, 0, n_j - 1)), off=off, n_j=n_j)))
            off += n_j
        dims = ((0,), (0,))
    a_bounds = np.cumsum([0] + [a.shape[1] // bk for a in a_list]) if mode == "nt" else None
    b_bounds = np.cumsum([0] + [b.shape[1] // bn for b in b_list]) if mode == "tn" else None

    def body(*refs):
        a_refs = refs[:na]
        b_refs = refs[na:na + nb]
        o_ref = refs[na + nb]
        acc_ref = refs[na + nb + 1] if nk > 1 else None
        j = pl.program_id(1)
        k = pl.program_id(2)

        def accumulate(p):
            if nk == 1:
                o_ref[...] = p.astype(out_dtype)
                return

            @pl.when(k == 0)
            def _():
                acc_ref[...] = p

            @pl.when(k > 0)
            def _():
                acc_ref[...] += p

        if na > 1:
            for t in range(na):
                @pl.when((k >= int(a_bounds[t])) & (k < int(a_bounds[t + 1])))
                def _(t=t):
                    accumulate(_dot(a_refs[t][...], b_refs[0][...], dims))
        elif nb > 1:
            for t in range(nb):
                @pl.when((j >= int(b_bounds[t])) & (j < int(b_bounds[t + 1])))
                def _(t=t):
                    accumulate(_dot(a_refs[0][...], b_refs[t][...], dims))
        else:
            accumulate(_dot(a_refs[0][...], b_refs[0][...], dims))

        if nk > 1:
            @pl.when(k == nk - 1)
            def _():
                o_ref[...] = acc_ref[...].astype(out_dtype)

    return pl.pallas_call(
        body, name=name, grid=grid,
        in_specs=a_specs + b_specs,
        out_specs=pl.BlockSpec((bm, bn), lambda i, j, k: (i, j)),
        out_shape=jax.ShapeDtypeStruct((M, N), out_dtype),
        scratch_shapes=[pltpu.VMEM((bm, bn), F32)] if nk > 1 else [],
        compiler_params=_cparams(("parallel", "parallel", "arbitrary")),
    )(*a_list, *b_list)


def _rms_fwd(x, g, name):
    S, D = x.shape
    tm = _blk(S, (256, 128, 64, 8))

    def body(x_ref, g_ref, o_ref):
        xv = x_ref[...]
        r = lax.rsqrt(jnp.mean(xv * xv, axis=-1, keepdims=True) + RMS_EPS)
        o_ref[...] = ((xv * r) * g_ref[...]).astype(o_ref.dtype)

    return pl.pallas_call(
        body, name=name, grid=(S // tm,),
        in_specs=[pl.BlockSpec((tm, D), lambda i: (i, 0)), pl.BlockSpec((1, D), lambda i: (0, 0))],
        out_specs=pl.BlockSpec((tm, D), lambda i: (i, 0)),
        out_shape=jax.ShapeDtypeStruct((S, D), BF16),
        compiler_params=_cparams(("parallel",)),
    )(x, g)


def _norm_residual(x, t, g, g_next, name):
    S, D = x.shape
    tm = _blk(S, (256, 128, 64, 8))
    chain = g_next is not None

    def body(*refs):
        if chain:
            x_ref, t_ref, g_ref, gn_ref, o_ref, h_ref = refs
        else:
            x_ref, t_ref, g_ref, o_ref = refs
        tv = t_ref[...]
        r = lax.rsqrt(jnp.mean(tv * tv, axis=-1, keepdims=True) + RMS_EPS)
        xn = x_ref[...] + (tv * r) * g_ref[...]
        o_ref[...] = xn
        if chain:
            rn = lax.rsqrt(jnp.mean(xn * xn, axis=-1, keepdims=True) + RMS_EPS)
            h_ref[...] = ((xn * rn) * gn_ref[...]).astype(BF16)

    row = pl.BlockSpec((tm, D), lambda i: (i, 0))
    vec = pl.BlockSpec((1, D), lambda i: (0, 0))
    return pl.pallas_call(
        body, name=name, grid=(S // tm,),
        in_specs=[row, row, vec] + ([vec] if chain else []),
        out_specs=[row, row] if chain else row,
        out_shape=[jax.ShapeDtypeStruct((S, D), F32), jax.ShapeDtypeStruct((S, D), BF16)] if chain
        else jax.ShapeDtypeStruct((S, D), F32),
        compiler_params=_cparams(("parallel",)),
    )(*([x, t, g, g_next] if chain else [x, t, g]))


def _rms_bwd(x, g, dh, res, out_dtype, name):
    S, D = x.shape
    tm = _blk(S, (256, 128, 64, 8))
    has_res = res is not None

    def body(*refs):
        if has_res:
            x_ref, g_ref, dh_ref, res_ref, dx_ref, dg_ref = refs
        else:
            x_ref, g_ref, dh_ref, dx_ref, dg_ref = refs
        xv = x_ref[...]
        r = lax.rsqrt(jnp.mean(xv * xv, axis=-1, keepdims=True) + RMS_EPS)
        xh = xv * r
        dhv = dh_ref[...].astype(F32)
        gd = dhv * g_ref[...]
        dx = r * (gd - xh * jnp.mean(gd * xh, axis=-1, keepdims=True))
        if has_res:
            dx = dx + res_ref[...]
        dx_ref[...] = dx.astype(out_dtype)
        part = jnp.sum(dhv * xh, axis=0, keepdims=True)

        @pl.when(pl.program_id(0) == 0)
        def _():
            dg_ref[...] = part

        @pl.when(pl.program_id(0) > 0)
        def _():
            dg_ref[...] += part

    row = pl.BlockSpec((tm, D), lambda i: (i, 0))
    vec = pl.BlockSpec((1, D), lambda i: (0, 0))
    ins = [x, g, dh] + ([res] if has_res else [])
    return pl.pallas_call(
        body, name=name, grid=(S // tm,),
        in_specs=[row, vec, row] + ([row] if has_res else []),
        out_specs=[row, vec],
        out_shape=[jax.ShapeDtypeStruct((S, D), out_dtype), jax.ShapeDtypeStruct((1, D), F32)],
        compiler_params=_cparams(("arbitrary",)),
    )(*ins)


def _rel_bucket(dist):
    max_exact = REL_BUCKETS // 2
    d_f = jnp.maximum(dist, 1).astype(jnp.float32)
    large = max_exact + (jnp.log(d_f / max_exact) / math.log(REL_MAX_DIST / max_exact)
                         * (REL_BUCKETS - max_exact)).astype(jnp.int32)
    large = jnp.minimum(large, REL_BUCKETS - 1)
    return jnp.where(dist < max_exact, dist, large)


def _bias_tables(rel_bias):
    qi = jnp.arange(Q_BLOCK)[:, None]
    kj = jnp.arange(2 * Q_BLOCK)[None, :]
    m = qi - kj + Q_BLOCK
    band_ok = (m >= 0) & (m <= Q_BLOCK)
    tabs = []
    for d in DILATIONS:
        bucket = _rel_bucket(jnp.maximum(m, 0) * d)
        onehot = (bucket[:, :, None] == jnp.arange(REL_BUCKETS)[None, None, :]).astype(F32)
        bias = jnp.einsum("qkb,bh->hqk", onehot, rel_bias.astype(F32), precision=lax.Precision.HIGHEST)
        tabs.append(jnp.where(band_ok[None], bias, NEG_INF))
    return jnp.stack(tabs)


HEADS_PER_STEP = {16: 1, 4: 1, 1: 8}


def _rows(r, d):
    return pl.ds(r, Q_BLOCK, stride=d) if d > 1 else pl.ds(0, Q_BLOCK)


def _attn_fwd(proj, bias, state, d, last, cat_in, name):
    S = proj.shape[0]
    R = Q_BLOCK * d
    nsb = S // R
    first = state is None
    scale = HEAD_DIM ** -0.5
    hb = HEADS_PER_STEP[d]
    W = hb * HEAD_DIM

    def body(*refs):
        q_ref, kp_ref, kc_ref, vp_ref, vc_ref, b_ref = refs[:6]
        pos = 6
        if not first:
            m_in, l_in, a_in = refs[pos:pos + 3]
            pos += 3
        if last:
            pos += 1
            o_ref, lse_ref, o_tmp = refs[pos:pos + 3]
        else:
            m_out, l_out, a_out = refs[pos:pos + 3]
        n = pl.program_id(0)
        for hh in range(hb):
            ls = slice(hh * HEAD_DIM, (hh + 1) * HEAD_DIM)
            bp = b_ref[hh, :, :Q_BLOCK]
            bc = b_ref[hh, :, Q_BLOCK:]
            for r in range(d):
                rows = _rows(r, d)
                q = q_ref[rows, ls]
                sp = _dot_nt(q, kp_ref[rows, ls]) * scale + bp
                sp = jnp.where(n == 0, NEG_INF, sp)
                sc = _dot_nt(q, kc_ref[rows, ls]) * scale + bc
                mrow = jnp.maximum(jnp.max(sp, axis=-1, keepdims=True), jnp.max(sc, axis=-1, keepdims=True))
                if first:
                    m_new = mrow
                else:
                    m_old = m_in[rows, ls][:, :1]
                    m_new = jnp.maximum(m_old, mrow)
                pp = jnp.exp(sp - m_new)
                pc = jnp.exp(sc - m_new)
                lrow = jnp.sum(pp, axis=-1, keepdims=True) + jnp.sum(pc, axis=-1, keepdims=True)
                pv = _dot_nn(pp, vp_ref[rows, ls]) + _dot_nn(pc, vc_ref[rows, ls])
                if first:
                    l_new, a_new = lrow, pv
                else:
                    alpha = jnp.exp(m_old - m_new)
                    l_new = alpha * l_in[rows, ls][:, :1] + lrow
                    a_new = alpha * a_in[rows, ls] + pv
                if last:
                    o_tmp[rows, ls] = a_new / l_new
                    lse_ref[rows, ls] = jnp.broadcast_to(m_new + jnp.log(l_new), (Q_BLOCK, HEAD_DIM))
                else:
                    m_out[rows, ls] = jnp.broadcast_to(m_new, (Q_BLOCK, HEAD_DIM))
                    l_out[rows, ls] = jnp.broadcast_to(l_new, (Q_BLOCK, HEAD_DIM))
                    a_out[rows, ls] = a_new
        if last:
            o_ref[...] = o_tmp[...].astype(BF16)

    ng = N_HEADS // hb

    def col(g0):
        return pl.BlockSpec((R, W), lambda n, h: (n, g0 * ng + h))

    def col_prev(g0):
        return pl.BlockSpec((R, W), lambda n, h: (jnp.maximum(n - 1, 0), g0 * ng + h))

    in_specs = [col(0), col_prev(1), col(1), col_prev(2), col(2),
                pl.BlockSpec((hb, Q_BLOCK, 2 * Q_BLOCK), lambda n, h: (h, 0, 0))]
    ins = [proj, proj, proj, proj, proj, bias]
    if not first:
        in_specs += [col(0)] * 3
        ins += list(state)
    st = jax.ShapeDtypeStruct((S, ATTN_W), F32)
    if last:
        in_specs.append(pl.BlockSpec(memory_space=pl.ANY))
        ins.append(cat_in)
        out_specs = [col(0), col(0)]
        out_shape = [jax.ShapeDtypeStruct(cat_in.shape, BF16), st]
        scratch = [pltpu.VMEM((R, W), F32)]
        aliases = {len(ins) - 1: 0}
    else:
        out_specs = [col(0)] * 3
        out_shape = [st, st, st]
        scratch = []
        aliases = {}
    return pl.pallas_call(
        body, name=name, grid=(nsb, ng), in_specs=in_specs, out_specs=out_specs,
        out_shape=out_shape, scratch_shapes=scratch, input_output_aliases=aliases,
        compiler_params=_cparams(("arbitrary", "arbitrary")),
    )(*ins)


def _attn_bwd(proj, bias, cat, d_cat, lse, acc, d, last, name):
    S = proj.shape[0]
    R = Q_BLOCK * d
    nsb = S // R
    first = acc is None
    scale = HEAD_DIM ** -0.5
    odt = BF16 if last else F32
    hb = HEADS_PER_STEP[d]
    W = hb * HEAD_DIM

    def body(*refs):
        q_ref, kp_ref, kc_ref, vp_ref, vc_ref, b_ref, o_ref, do_ref, lse_ref = refs[:9]
        pos = 9
        if not first:
            dq_in, dk_in, dv_in = refs[pos:pos + 3]
            pos += 3
        dq_out, dk_out, dv_out, db_out = refs[pos:pos + 4]
        pos += 4
        ck, cv, o_f, do_f, dq_t, dk_t, dv_t, db_acc = refs[pos:pos + 8]
        n = pl.program_id(1)

        @pl.when(n == 0)
        def _():
            db_acc[...] = jnp.zeros_like(db_acc)
            ck[...] = jnp.zeros_like(ck)
            cv[...] = jnp.zeros_like(cv)

        @pl.when(n < nsb)
        def _():
            o_f[...] = o_ref[...].astype(F32)
            do_f[...] = do_ref[...].astype(F32)
            for hh in range(hb):
                ls = slice(hh * HEAD_DIM, (hh + 1) * HEAD_DIM)
                bp = b_ref[hh, :, :Q_BLOCK]
                bc = b_ref[hh, :, Q_BLOCK:]
                for r in range(d):
                    rows = _rows(r, d)
                    q = q_ref[rows, ls]
                    kp = kp_ref[rows, ls]
                    kc = kc_ref[rows, ls]
                    vp = vp_ref[rows, ls]
                    vc = vc_ref[rows, ls]
                    do = do_f[rows, ls]
                    lse_r = lse_ref[rows, ls][:, :1]
                    sp = _dot_nt(q, kp) * scale + bp
                    sp = jnp.where(n == 0, NEG_INF, sp)
                    sc = _dot_nt(q, kc) * scale + bc
                    pp = jnp.exp(sp - lse_r)
                    pc = jnp.exp(sc - lse_r)
                    dd = jnp.sum(do * o_f[rows, ls], axis=-1, keepdims=True)
                    dsp = pp * (_dot_nt(do, vp) - dd)
                    dsc = pc * (_dot_nt(do, vc) - dd)
                    db_acc[hh, :, :Q_BLOCK] += dsp
                    db_acc[hh, :, Q_BLOCK:] += dsc
                    dq = (_dot_nn(dsp, kp) + _dot_nn(dsc, kc)) * scale
                    dk_prev = ck[rows, ls] + _dot_tn(dsp, q) * scale
                    dv_prev = cv[rows, ls] + _dot_tn(pp, do)
                    ck[rows, ls] = _dot_tn(dsc, q) * scale
                    cv[rows, ls] = _dot_tn(pc, do)
                    if not first:
                        dq = dq + dq_in[rows, ls]
                        dk_prev = dk_prev + dk_in[rows, ls]
                        dv_prev = dv_prev + dv_in[rows, ls]
                    dq_t[rows, ls] = dq
                    dk_t[rows, ls] = dk_prev
                    dv_t[rows, ls] = dv_prev
            dq_out[...] = dq_t[...].astype(odt)

            @pl.when(n > 0)
            def _():
                dk_out[...] = dk_t[...].astype(odt)
                dv_out[...] = dv_t[...].astype(odt)

        @pl.when(n == nsb)
        def _():
            if first:
                dk_out[...] = ck[...].astype(odt)
                dv_out[...] = cv[...].astype(odt)
            else:
                dk_out[...] = (ck[...] + dk_in[...]).astype(odt)
                dv_out[...] = (cv[...] + dv_in[...]).astype(odt)
            db_out[...] = db_acc[...]

    last_n = nsb - 1
    ng = N_HEADS // hb

    def cur(g0):
        return pl.BlockSpec((R, W), lambda h, n: (jnp.minimum(n, last_n), g0 * ng + h))

    def prev(g0):
        return pl.BlockSpec((R, W), lambda h, n: (jnp.maximum(jnp.minimum(n, last_n) - 1, 0), g0 * ng + h))

    delayed = pl.BlockSpec((R, W), lambda h, n: (jnp.maximum(n - 1, 0), h))
    in_specs = [cur(0), prev(1), cur(1), prev(2), cur(2),
                pl.BlockSpec((hb, Q_BLOCK, 2 * Q_BLOCK), lambda h, n: (h, 0, 0)),
                cur(0), cur(0), cur(0)]
    ins = [proj, proj, proj, proj, proj, bias, cat, d_cat, lse]
    if not first:
        in_specs += [cur(0), delayed, delayed]
        ins += list(acc)
    st = jax.ShapeDtypeStruct((S, ATTN_W), odt)
    tile = pltpu.VMEM((R, W), F32)
    return pl.pallas_call(
        body, name=name, grid=(ng, nsb + 1), in_specs=in_specs,
        out_specs=[cur(0), delayed, delayed,
                   pl.BlockSpec((hb, Q_BLOCK, 2 * Q_BLOCK), lambda h, n: (h, 0, 0))],
        out_shape=[st, st, st, jax.ShapeDtypeStruct((N_HEADS, Q_BLOCK, 2 * Q_BLOCK), F32)],
        scratch_shapes=[tile] * 7 + [pltpu.VMEM((hb, Q_BLOCK, 2 * Q_BLOCK), F32)],
        compiler_params=_cparams(("arbitrary", "arbitrary")),
    )(*ins)


HG_LEVELS = (32, 16, 8, 4, 2, 1)
N_LEV = len(HG_LEVELS)


def _hg_consts():
    C = HG_CHUNK
    t = np.arange(C)
    mq, mk, masks = [], [], []
    for B in HG_LEVELS:
        up = (t // B) % 2 == 1
        bs = (t // B) * B
        be = bs + B - 1
        mq.append(up[:, None] & (t[None, :] >= bs[:, None]) & (t[None, :] <= t[:, None]))
        mk.append((~up)[:, None] & (t[None, :] > t[:, None]) & (t[None, :] <= be[:, None]))
        masks.append(up[:, None] & (~up)[None, :] & ((t[:, None] // (2 * B)) == (t[None, :] // (2 * B))))
    masks.append(np.eye(C, dtype=bool))
    mb = t[None, :] <= t[:, None]
    mw = t[None, :] > t[:, None]
    m_all = np.concatenate(mq + mk + [mb, mw], axis=0).astype(np.float32)
    return jnp.asarray(m_all, BF16), jnp.asarray(np.stack(masks).astype(np.float32))


def _split3(v):
    hi = v.astype(BF16)
    r1 = v - hi.astype(F32)
    mid = r1.astype(BF16)
    lo = (r1 - mid.astype(F32)).astype(BF16)
    return jnp.concatenate([hi, mid, lo], axis=1)


def _hg_chunk_fwd(fz, iv, qz, lbh, m_all, mask_ref, st_t):
    C = HG_CHUNK
    sig = _sigmoid(fz)
    f = lbh + (1.0 - lbh) * sig
    lf = jnp.log(f)
    kk = 1.0 - f
    sq = _sigmoid(qz)
    qq = qz * sq
    a3 = lax.dot_general(m_all, _split3(lf), (((1,), (0,)), ((), ())), preferred_element_type=F32)
    args = a3[:, :HEAD_DIM] + a3[:, HEAD_DIM:2 * HEAD_DIM] + a3[:, 2 * HEAD_DIM:]
    e = jnp.exp(args)
    qs = [qq * e[j * C:(j + 1) * C] for j in range(N_LEV)]
    ks = [kk * e[(N_LEV + j) * C:(N_LEV + j + 1) * C] for j in range(N_LEV)]
    a = mask_ref[N_LEV] * _dot_nt(qq, kk)
    for j in range(N_LEV):
        a = a + mask_ref[j] * _dot_nt(qs[j], ks[j])
    eb = e[2 * N_LEV * C:(2 * N_LEV + 1) * C]
    ew = e[(2 * N_LEV + 1) * C:]
    qe = qq * eb
    w = kk * ew
    o = _dot_nt(qe, st_t) + _dot_nn(a, iv)
    eb_last = eb[C - 1:C, :]
    new_st = st_t * eb_last + _dot_tn(iv, w)
    return dict(sig=sig, f=f, kk=kk, sq=sq, qq=qq, e=e, qs=qs, ks=ks, a=a, eb=eb, ew=ew, qe=qe, w=w, o=o,
                eb_last=eb_last, new_st=new_st)


def _hg_rows(S):
    return _blk(S, (256, 128, 64))


def _host_refs(refs, n_in, n_out, n_scratch, comm):
    nci = len(comm["ins"]) if comm else 0
    nco = len(comm["out_shapes"]) if comm else 0
    a = n_in
    b = a + nci
    c = b + n_out
    d = c + nco
    e = d + n_scratch
    return refs[:a], refs[a:b], refs[b:c], refs[c:d], refs[d:e], refs[e:]


def _host_call(comm, n_in, n_out, in_specs, out_specs, out_shape, scratch, aliases):
    if comm:
        in_specs = in_specs + [ANY] * len(comm["ins"])
        out_specs = out_specs + [ANY] * len(comm["out_shapes"])
        out_shape = out_shape + comm["out_shapes"]
        scratch = scratch + [pltpu.SemaphoreType.DMA((comm["nsem"],)), pltpu.SemaphoreType.DMA((comm["nsem"],))]
        aliases = dict(aliases)
        aliases.update({n_in + i: n_out + o for i, o in comm["aliases"].items()})
    return dict(in_specs=in_specs, out_specs=out_specs, out_shape=out_shape, scratch_shapes=scratch,
                input_output_aliases=aliases)


def _hgrn_fwd(proj, lb, gain, cat_in, name, comm=None):
    S = proj.shape[0]
    rows = _hg_rows(S)
    cb = rows // HG_CHUNK
    nsteps = S // rows
    m_all, masks = _hg_consts()

    def body(*refs):
        own_in, c_in, own_out, c_out, own_scr, sems = _host_refs(refs, 9, 2, 1, comm)
        fz_ref, iv_ref, qz_ref, gz_ref, lb_ref, gain_ref, m_ref, mask_ref, _ = own_in
        o_ref, st_ref = own_out
        (st_scr,) = own_scr

        @pl.when(pl.program_id(0) == 0)
        def _():
            st_scr[...] = jnp.zeros_like(st_scr)
            if comm:
                comm["start"](c_in, c_out, *sems)

        m_all_v = m_ref[...]

        def chunk(c, carry):
            rs = pl.ds(pl.multiple_of(c * HG_CHUNK, HG_CHUNK), HG_CHUNK)
            for h in range(N_HEADS):
                sl = slice(h * HEAD_DIM, (h + 1) * HEAD_DIM)
                st_t = st_scr[h]
                st_ref[c, h] = st_t
                gz = gz_ref[rs, sl]
                iv = iv_ref[rs, sl]
                q = _hg_chunk_fwd(fz_ref[rs, sl], iv, qz_ref[rs, sl], lb_ref[:, sl], m_all_v, mask_ref, st_t)
                st_scr[h] = q["new_st"]
                o = q["o"]
                r = lax.rsqrt(jnp.mean(o * o, axis=-1, keepdims=True) + RMS_EPS)
                y = ((o * r) * gain_ref[:, sl]) * (gz * _sigmoid(gz))
                o_ref[rs, sl] = y.astype(BF16)
            return carry

        lax.fori_loop(0, cb, chunk, 0)
        if comm:
            @pl.when(pl.program_id(0) == nsteps - 1)
            def _():
                comm["wait"](c_in, c_out, *sems)

    def col(c):
        return pl.BlockSpec((rows, HG_W), lambda i: (i, c))

    vec = pl.BlockSpec((1, HG_W), lambda i: (0, 0))
    args = _host_call(
        comm, 9, 2,
        in_specs=[col(3), col(4), col(5), col(6), vec, vec,
                  pl.BlockSpec(m_all.shape, lambda i: (0, 0)), pl.BlockSpec(masks.shape, lambda i: (0, 0, 0)),
                  pl.BlockSpec(memory_space=pl.ANY)],
        out_specs=[col(1), pl.BlockSpec((cb, N_HEADS, HEAD_DIM, HEAD_DIM), lambda i: (i, 0, 0, 0))],
        out_shape=[jax.ShapeDtypeStruct(cat_in.shape, BF16),
                   jax.ShapeDtypeStruct((S // HG_CHUNK, N_HEADS, HEAD_DIM, HEAD_DIM), F32)],
        scratch=[pltpu.VMEM((N_HEADS, HEAD_DIM, HEAD_DIM), F32)],
        aliases={8: 0})
    return pl.pallas_call(
        body, name=name, grid=(nsteps,), compiler_params=_cparams(("arbitrary",)), **args,
    )(proj, proj, proj, proj, lb, gain, m_all, masks, cat_in, *(comm["ins"] if comm else []))


def _hgrn_bwd(proj, lb, gain, states, d_cat, name, comm=None):
    S = proj.shape[0]
    rows = _hg_rows(S)
    cb = rows // HG_CHUNK
    nblk = S // rows
    C = HG_CHUNK
    m_all, masks = _hg_consts()

    def body(*refs):
        own_in, c_in, own_out, c_out, own_scr, sems = _host_refs(refs, 10, 3, 1, comm)
        fz_ref, iv_ref, qz_ref, gz_ref, lb_ref, gain_ref, m_ref, mask_ref, st_ref, dy_ref = own_in
        dz_ref, dlb_ref, dgain_ref = own_out
        (dst_scr,) = own_scr

        @pl.when(pl.program_id(0) == 0)
        def _():
            dst_scr[...] = jnp.zeros_like(dst_scr)
            dlb_ref[...] = jnp.zeros_like(dlb_ref)
            dgain_ref[...] = jnp.zeros_like(dgain_ref)
            if comm:
                comm["start"](c_in, c_out, *sems)

        m_all_v = m_ref[...]
        last_row = lax.broadcasted_iota(jnp.int32, (C, HEAD_DIM), 0) == C - 1

        def chunk(ci, carry):
            c = cb - 1 - ci
            rs = pl.ds(pl.multiple_of(c * C, C), C)
            for h in range(N_HEADS):
                sl = slice(h * HEAD_DIM, (h + 1) * HEAD_DIM)
                lbh = lb_ref[:, sl]
                gh = gain_ref[:, sl]
                st_t = st_ref[c, h]
                fz = fz_ref[rs, sl]
                iv = iv_ref[rs, sl]
                qz = qz_ref[rs, sl]
                gz = gz_ref[rs, sl]
                q = _hg_chunk_fwd(fz, iv, qz, lbh, m_all_v, mask_ref, st_t)
                o = q["o"]
                r = lax.rsqrt(jnp.mean(o * o, axis=-1, keepdims=True) + RMS_EPS)
                on = o * r
                sg = _sigmoid(gz)
                gate = gz * sg
                dy = dy_ref[rs, sl].astype(F32)
                dgain_ref[:, sl] += jnp.sum(dy * on * gate, axis=0, keepdims=True)
                dgz = (dy * on * gh) * (sg * (1.0 + gz * (1.0 - sg)))
                don = dy * gh * gate
                do = r * (don - on * jnp.mean(don * on, axis=-1, keepdims=True))
                da = _dot_nt(do, iv)
                dv = _dot_tn(q["a"], do)
                dqe = _dot_nn(do, st_t)
                dst_new = _dot_tn(do, q["qe"])
                dsp = dst_scr[h]
                dw = _dot_nn(iv, dsp)
                dv = dv + _dot_nt(q["w"], dsp)
                d_eb_last = jnp.sum(dsp * st_t, axis=0, keepdims=True)
                dst_scr[h] = dsp * q["eb_last"] + dst_new
                dad = da * mask_ref[N_LEV]
                dq = _dot_nn(dad, q["kk"])
                dk = _dot_tn(dad, q["qq"])
                dargs_q = []
                dargs_k = []
                for j in range(N_LEV):
                    daj = da * mask_ref[j]
                    dqj = _dot_nn(daj, q["ks"][j])
                    dkj = _dot_tn(daj, q["qs"][j])
                    dq = dq + dqj * q["e"][j * C:(j + 1) * C]
                    dk = dk + dkj * q["e"][(N_LEV + j) * C:(N_LEV + j + 1) * C]
                    dargs_q.append(dqj * q["qs"][j])
                    dargs_k.append(dkj * q["ks"][j])
                dq = dq + dqe * q["eb"]
                darg_b = dqe * q["qe"] + jnp.where(last_row, d_eb_last * q["eb_last"], 0.0)
                dk = dk + dw * q["ew"]
                darg_w = dw * q["w"]
                dall = jnp.concatenate(dargs_q + dargs_k + [darg_b, darg_w], axis=0)
                hi = dall.astype(BF16)
                lo = (dall - hi.astype(F32)).astype(BF16)
                dl2 = lax.dot_general(m_all_v, jnp.concatenate([hi, lo], axis=1), (((0,), (0,)), ((), ())),
                                      preferred_element_type=F32)
                dlf = dl2[:, :HEAD_DIM] + dl2[:, HEAD_DIM:]
                df = dlf / q["f"] - dk
                sig = q["sig"]
                dlb_ref[:, sl] += jnp.sum(df * (1.0 - sig), axis=0, keepdims=True)
                dfz = df * (1.0 - lbh) * (sig * (1.0 - sig))
                sq = q["sq"]
                dqz = dq * (sq * (1.0 + qz * (1.0 - sq)))
                dz_ref[rs, h * HEAD_DIM:(h + 1) * HEAD_DIM] = dfz.astype(BF16)
                dz_ref[rs, HG_W + h * HEAD_DIM:HG_W + (h + 1) * HEAD_DIM] = dv.astype(BF16)
                dz_ref[rs, 2 * HG_W + h * HEAD_DIM:2 * HG_W + (h + 1) * HEAD_DIM] = dqz.astype(BF16)
                dz_ref[rs, 3 * HG_W + h * HEAD_DIM:3 * HG_W + (h + 1) * HEAD_DIM] = dgz.astype(BF16)
            return carry

        lax.fori_loop(0, cb, chunk, 0)
        if comm:
            @pl.when(pl.program_id(0) == nblk - 1)
            def _():
                comm["wait"](c_in, c_out, *sems)

    def col(c):
        return pl.BlockSpec((rows, HG_W), lambda i: (nblk - 1 - i, c))

    vec = pl.BlockSpec((1, HG_W), lambda i: (0, 0))
    args = _host_call(
        comm, 10, 3,
        in_specs=[col(3), col(4), col(5), col(6), vec, vec,
                  pl.BlockSpec(m_all.shape, lambda i: (0, 0)), pl.BlockSpec(masks.shape, lambda i: (0, 0, 0)),
                  pl.BlockSpec((cb, N_HEADS, HEAD_DIM, HEAD_DIM), lambda i: (nblk - 1 - i, 0, 0, 0)),
                  col(1)],
        out_specs=[pl.BlockSpec((rows, 4 * HG_W), lambda i: (nblk - 1 - i, 0)), vec, vec],
        out_shape=[jax.ShapeDtypeStruct((S, 4 * HG_W), BF16), jax.ShapeDtypeStruct((1, HG_W), F32),
                   jax.ShapeDtypeStruct((1, HG_W), F32)],
        scratch=[pltpu.VMEM((N_HEADS, HEAD_DIM, HEAD_DIM), F32)],
        aliases={})
    return pl.pallas_call(
        body, name=name, grid=(nblk,), compiler_params=_cparams(("arbitrary",)), **args,
    )(proj, proj, proj, proj, lb, gain, m_all, masks, states, d_cat, *(comm["ins"] if comm else []))


def _cross_fwd(cq, ckv, name):
    S = cq.shape[0]
    n_mem = ckv.shape[0]
    tq = _blk(S, (256, 128))
    scale = CROSS_DIM ** -0.5

    def body(q_ref, kv_ref, o_ref):
        for h in range(CROSS_HEADS):
            sl = slice(h * CROSS_DIM, (h + 1) * CROSS_DIM)
            k = kv_ref[:, sl]
            v = kv_ref[:, D_MODEL + h * CROSS_DIM:D_MODEL + (h + 1) * CROSS_DIM]
            s = _dot_nt(q_ref[:, sl], k) * scale
            m = jnp.max(s, axis=-1, keepdims=True)
            p = jnp.exp(s - m)
            p = p / jnp.sum(p, axis=-1, keepdims=True)
            o_ref[:, sl] = _dot_nn(p, v).astype(BF16)

    return pl.pallas_call(
        body, name=name, grid=(S // tq,),
        in_specs=[pl.BlockSpec((tq, D_MODEL), lambda i: (i, 0)), pl.BlockSpec((n_mem, 2 * D_MODEL), lambda i: (0, 0))],
        out_specs=pl.BlockSpec((tq, D_MODEL), lambda i: (i, 0)),
        out_shape=jax.ShapeDtypeStruct((S, D_MODEL), BF16),
        compiler_params=_cparams(("parallel",)),
    )(cq, ckv)


def _cross_bwd(cq, ckv, d_o, name):
    S = cq.shape[0]
    n_mem = ckv.shape[0]
    tq = _blk(S, (256, 128))
    scale = CROSS_DIM ** -0.5

    def body(q_ref, kv_ref, do_ref, dq_ref, dkv_ref):
        @pl.when(pl.program_id(0) == 0)
        def _():
            dkv_ref[...] = jnp.zeros_like(dkv_ref)

        for h in range(CROSS_HEADS):
            sl = slice(h * CROSS_DIM, (h + 1) * CROSS_DIM)
            slv = slice(D_MODEL + h * CROSS_DIM, D_MODEL + (h + 1) * CROSS_DIM)
            q = q_ref[:, sl]
            k = kv_ref[:, sl]
            v = kv_ref[:, slv]
            do = do_ref[:, sl]
            s = _dot_nt(q, k) * scale
            m = jnp.max(s, axis=-1, keepdims=True)
            p = jnp.exp(s - m)
            p = p / jnp.sum(p, axis=-1, keepdims=True)
            dp = _dot_nt(do, v)
            ds = p * (dp - jnp.sum(dp * p, axis=-1, keepdims=True)) * scale
            dq_ref[:, sl] = _dot_nn(ds, k).astype(BF16)
            dkv_ref[:, sl] += _dot_tn(ds, q)
            dkv_ref[:, slv] += _dot_tn(p, do)

    row = pl.BlockSpec((tq, D_MODEL), lambda i: (i, 0))
    kv = pl.BlockSpec((n_mem, 2 * D_MODEL), lambda i: (0, 0))
    return pl.pallas_call(
        body, name=name, grid=(S // tq,),
        in_specs=[row, kv, row], out_specs=[row, kv],
        out_shape=[jax.ShapeDtypeStruct((S, D_MODEL), BF16), jax.ShapeDtypeStruct((n_mem, 2 * D_MODEL), F32)],
        compiler_params=_cparams(("arbitrary",)),
    )(cq, ckv, d_o)


FF_BLOCK = 512


def _mm_gate_up(hf, w_gu, name):
    S, K = hf.shape
    bm = _blk(S, (1024, 512, 256, 128))
    nj = D_FF // FF_BLOCK

    def body(a_ref, bg_ref, bu_ref, g_ref, u_ref, act_ref):
        a = a_ref[...]
        g = _dot_nn(a, bg_ref[...])
        u = _dot_nn(a, bu_ref[...])
        g_ref[...] = g.astype(BF16)
        u_ref[...] = u.astype(BF16)
        act_ref[...] = ((g * _sigmoid(g)) * u).astype(BF16)

    out = pl.BlockSpec((bm, FF_BLOCK), lambda i, j: (i, j))
    sds = jax.ShapeDtypeStruct((S, D_FF), BF16)
    return pl.pallas_call(
        body, name=name, grid=(S // bm, nj),
        in_specs=[pl.BlockSpec((bm, K), lambda i, j: (i, 0)), pl.BlockSpec((K, FF_BLOCK), lambda i, j: (0, j)),
                  pl.BlockSpec((K, FF_BLOCK), lambda i, j: (0, j + nj))],
        out_specs=[out, out, out], out_shape=[sds, sds, sds],
        compiler_params=_cparams(("parallel", "arbitrary")),
    )(hf, w_gu, w_gu)


def _mm_down_x(dy, w_down, g, u, name):
    S, K = dy.shape
    bm = _blk(S, (1024, 512, 256, 128))
    nj = D_FF // FF_BLOCK

    def body(a_ref, b_ref, g_ref, u_ref, dg_ref, du_ref):
        da = _dot_nt(a_ref[...], b_ref[...])
        gv = g_ref[...].astype(F32)
        sg = _sigmoid(gv)
        dg_ref[...] = (da * u_ref[...].astype(F32) * (sg * (1.0 + gv * (1.0 - sg)))).astype(BF16)
        du_ref[...] = (da * (gv * sg)).astype(BF16)

    tile = pl.BlockSpec((bm, FF_BLOCK), lambda i, j: (i, j))
    sds = jax.ShapeDtypeStruct((S, D_FF), BF16)
    return pl.pallas_call(
        body, name=name, grid=(S // bm, nj),
        in_specs=[pl.BlockSpec((bm, K), lambda i, j: (i, 0)), pl.BlockSpec((FF_BLOCK, K), lambda i, j: (j, 0)),
                  tile, tile],
        out_specs=[tile, tile], out_shape=[sds, sds],
        compiler_params=_cparams(("parallel", "arbitrary")),
    )(dy, w_down, g, u)


def _loss_head(y, target, name):
    S, D = y.shape
    tm = _blk(S, (256, 128, 64, 8))

    def body(y_ref, t_ref, dy_ref, l_ref):
        diff = y_ref[...] - t_ref[...]
        dy_ref[...] = diff * (1.0 / D)
        sq = (diff * diff) * (0.5 / D)
        part = jnp.sum(sq.reshape(tm // 8, 8, D), axis=0)

        @pl.when(pl.program_id(0) == 0)
        def _():
            l_ref[...] = part

        @pl.when(pl.program_id(0) > 0)
        def _():
            l_ref[...] += part

    row = pl.BlockSpec((tm, D), lambda i: (i, 0))
    return pl.pallas_call(
        body, name=name, grid=(S // tm,), in_specs=[row, row],
        out_specs=[row, pl.BlockSpec((8, D), lambda i: (0, 0))],
        out_shape=[jax.ShapeDtypeStruct((S, D), F32), jax.ShapeDtypeStruct((8, D), F32)],
        compiler_params=_cparams(("arbitrary",)),
    )(y, target)


def _layer_fwd(x0, mem, g, w, lb, hg_gain, bias, tag, comm=None):
    S = x0.shape[0]
    h0 = _rms_fwd(x0, g[0], f"rms0_{tag}")
    proj = _mm(h0, w["w_in"], "nn", F32, f"mm_in_{tag}")
    state = None
    cat = lax.empty((S, D_MODEL), BF16)
    for bi, d in enumerate(DILATIONS):
        last = bi == len(DILATIONS) - 1
        out = _attn_fwd(proj, bias[bi], state, d, last, cat if last else None, f"attn_fwd{d}_{tag}")
        if last:
            cat, lse = out
        else:
            state = tuple(out)
    cat, states, *carried = _hgrn_fwd(proj, lb, hg_gain, cat, f"hgrn_fwd_{tag}", comm)
    mix = _mm(cat, w["w_out"], "nn", F32, f"mm_out_{tag}")
    x1, hc = _norm_residual(x0, mix, g[1], g[2], f"res1_{tag}")
    cq = _mm(hc, w["w_cq"], "nn", BF16, f"mm_cq_{tag}")
    mn = _rms_fwd(mem, g[3], f"rms3_{tag}")
    ckv = _mm(mn, w["w_ckv"], "nn", BF16, f"mm_ckv_{tag}")
    cop = _cross_fwd(cq, ckv, f"cross_fwd_{tag}")
    co = _mm(cop, w["w_co"], "nn", F32, f"mm_co_{tag}")
    x2, hf = _norm_residual(x1, co, g[4], g[5], f"res2_{tag}")
    gate, up, act = _mm_gate_up(hf, w["w_gate_up"], f"mm_gu_{tag}")
    y = _mm(act, w["w_down"], "nn", F32, f"mm_down_{tag}")
    x3 = _norm_residual(x2, y, g[6], None, f"res3_{tag}")
    saved = dict(x0=x0, h0=h0, proj=proj, cat=cat, lse=lse, states=states, mix=mix, x1=x1, hc=hc, cq=cq, mn=mn,
                 ckv=ckv, cop=cop, co=co, x2=x2, hf=hf, gate=gate, up=up, act=act, y=y)
    return x3, saved, carried


def _layer_bwd(dx3, sv, mem, g, w, lb, hg_gain, bias, tag, comm=None):
    dg = [None] * N_NORMS
    gw = {}
    dy, dg[6] = _rms_bwd(sv["y"], g[6], dx3, None, BF16, f"rmsb6_{tag}")
    dgu = list(_mm_down_x(dy, w["w_down"], sv["gate"], sv["up"], f"mmb_down_x_{tag}"))
    gw["w_down"] = _mm(sv["act"], dy, "tn", BF16, f"mmb_down_w_{tag}")
    gw["w_gate_up"] = _mm(sv["hf"], dgu, "tn", BF16, f"mmb_gu_w_{tag}")
    d_hf = _mm(dgu, w["w_gate_up"], "nt", F32, f"mmb_gu_x_{tag}")
    dx2, dg[5] = _rms_bwd(sv["x2"], g[5], d_hf, dx3, F32, f"rmsb5_{tag}")
    d_co, dg[4] = _rms_bwd(sv["co"], g[4], dx2, None, BF16, f"rmsb4_{tag}")
    d_cop = _mm(d_co, w["w_co"], "nt", BF16, f"mmb_co_x_{tag}")
    gw["w_co"] = _mm(sv["cop"], d_co, "tn", BF16, f"mmb_co_w_{tag}")
    d_cq, d_ckv = _cross_bwd(sv["cq"], sv["ckv"], d_cop, f"cross_bwd_{tag}")
    gw["w_cq"] = _mm(sv["hc"], d_cq, "tn", BF16, f"mmb_cq_w_{tag}")
    d_hc = _mm(d_cq, w["w_cq"], "nt", F32, f"mmb_cq_x_{tag}")
    gw["w_ckv"] = _mm(sv["mn"], d_ckv, "tn", BF16, f"mmb_ckv_w_{tag}")
    d_mn = _mm(d_ckv, w["w_ckv"], "nt", F32, f"mmb_ckv_x_{tag}")
    _, dg[3] = _rms_bwd(mem, g[3], d_mn, None, BF16, f"rmsb3_{tag}")
    dx1, dg[2] = _rms_bwd(sv["x1"], g[2], d_hc, dx2, F32, f"rmsb2_{tag}")
    d_mix, dg[1] = _rms_bwd(sv["mix"], g[1], dx1, None, BF16, f"rmsb1_{tag}")
    d_cat = _mm(d_mix, w["w_out"], "nt", BF16, f"mmb_out_x_{tag}")
    gw["w_out"] = _mm(sv["cat"], d_mix, "tn", BF16, f"mmb_out_w_{tag}")
    acc = None
    dbias = []
    for bi, d in enumerate(DILATIONS):
        last = bi == len(DILATIONS) - 1
        dq, dk, dv, db = _attn_bwd(sv["proj"], bias[bi], sv["cat"], d_cat, sv["lse"], acc, d, last,
                                   f"attn_bwd{d}_{tag}")
        acc = (dq, dk, dv)
        dbias.append(db)
    d_hz, dlb, dgain, *carried = _hgrn_bwd(sv["proj"], lb, hg_gain, sv["states"], d_cat, f"hgrn_bwd_{tag}", comm)
    parts = [acc[0], acc[1], acc[2], d_hz]
    gw["w_in"] = _mm(sv["h0"], parts, "tn", BF16, f"mmb_in_w_{tag}")
    d_h0 = _mm(parts, w["w_in"], "nt", F32, f"mmb_in_x_{tag}")
    dx0, dg[0] = _rms_bwd(sv["x0"], g[0], d_h0, dx1, F32, f"rmsb0_{tag}")
    return dx0, gw, jnp.stack(dg), dlb, dgain, jnp.stack(dbias), carried


def _lb_all(lb_logits):
    p = jax.nn.softmax(lb_logits.astype(F32), axis=0)
    return jnp.cumsum(p, axis=0) - p


def _local_step(x, mem, target, rel_bias, lb_logits, norm_gains, hg_norm, weights_of_layer, hooks=None):
    hooks = hooks or {}
    L = lb_logits.shape[0]
    bias, bias_vjp = jax.vjp(_bias_tables, rel_bias)
    lb_all, lb_vjp = jax.vjp(_lb_all, lb_logits)
    gains = norm_gains.reshape(L, N_NORMS, 1, D_MODEL)
    saved = []
    h = x
    ws = []
    for l in range(L):
        w = weights_of_layer(l)
        ws.append(w)
        comm = hooks["fwd_comm"](l) if "fwd_comm" in hooks else None
        h, sv, carried = _layer_fwd(h, mem, gains[l], w, lb_all[l:l + 1], hg_norm[l:l + 1], bias, f"l{l}", comm)
        if comm:
            hooks["fwd_done"](l, carried)
        saved.append(sv)
    dy, lparts = _loss_head(h, target, "loss_head")
    loss = jnp.sum(lparts)
    d_gains, d_lb, d_hg, gws = [None] * L, [None] * L, [None] * L, [None] * L
    d_bias = jnp.zeros_like(bias)
    dh = dy
    for l in reversed(range(L)):
        comm = hooks["bwd_comm"](l) if "bwd_comm" in hooks else None
        dh, gw, dgl, dlbl, dhgl, dbl, carried = _layer_bwd(dh, saved[l], mem, gains[l], ws[l], lb_all[l:l + 1],
                                                           hg_norm[l:l + 1], bias, f"l{l}", comm)
        if comm:
            hooks["bwd_done"](l, carried)
        d_gains[l], d_lb[l], d_hg[l], gws[l] = dgl.reshape(N_NORMS, D_MODEL), dlbl[0], dhgl[0], gw
        d_bias = d_bias + dbl
        if "layer_grads" in hooks:
            gws[l] = hooks["layer_grads"](l, gw)
    (d_rel_bias,) = bias_vjp(d_bias)
    (d_lb_logits,) = lb_vjp(jnp.stack(d_lb))
    return loss, dh, gws, d_rel_bias, d_lb_logits, jnp.stack(d_gains), jnp.stack(d_hg)


def kernel(x, mem, rel_bias, lb_logits, norm_gains, w_in, hg_norm, w_out, w_cq, w_ckv, w_co, w_gate_up, w_down, loss_target, m_rel_bias, m_lb_logits, m_norm_gains, m_w_in, m_hg_norm, m_w_out, m_w_cq, m_w_ckv, m_w_co, m_w_gate_up, m_w_down, v_rel_bias, v_lb_logits, v_norm_gains, v_w_in, v_hg_norm, v_w_out, v_w_cq, v_w_ckv, v_w_co, v_w_gate_up, v_w_down):
    return _train_step(
        x, mem, loss_target,
        dict(rel_bias=rel_bias, lb_logits=lb_logits, norm_gains=norm_gains, hg_norm=hg_norm, w_in=w_in, w_out=w_out,
             w_cq=w_cq, w_ckv=w_ckv, w_co=w_co, w_gate_up=w_gate_up, w_down=w_down),
        dict(rel_bias=m_rel_bias, lb_logits=m_lb_logits, norm_gains=m_norm_gains, hg_norm=m_hg_norm, w_in=m_w_in,
             w_out=m_w_out, w_cq=m_w_cq, w_ckv=m_w_ckv, w_co=m_w_co, w_gate_up=m_w_gate_up, w_down=m_w_down),
        dict(rel_bias=v_rel_bias, lb_logits=v_lb_logits, norm_gains=v_norm_gains, hg_norm=v_hg_norm, w_in=v_w_in,
             w_out=v_w_out, w_cq=v_w_cq, w_ckv=v_w_ckv, w_co=v_w_co, w_gate_up=v_w_gate_up, w_down=v_w_down))


BIG = ("w_in", "w_out", "w_cq", "w_ckv", "w_co", "w_gate_up", "w_down")
SHARD_AXIS = dict(w_in=1, w_out=0, w_cq=0, w_ckv=1, w_co=0, w_gate_up=1, w_down=0)
NB = len(BIG)
OUT_ORDER = ("rel_bias", "lb_logits", "norm_gains", "w_in", "hg_norm", "w_out", "w_cq", "w_ckv", "w_co",
             "w_gate_up", "w_down")
ANY = pl.BlockSpec(memory_space=pl.ANY)


def _place():
    x, y, c = lax.axis_index("x"), lax.axis_index("y"), lax.axis_index("c")
    chips = [(1 - x, y), (x, 1 - y), (1 - x, 1 - y)]
    return x, y, c, chips


def _remote(src, dst, send_sem, recv_sem, to):
    return pltpu.make_async_remote_copy(src_ref=src, dst_ref=dst, send_sem=send_sem, recv_sem=recv_sem,
                                        device_id=to, device_id_type=MESH_ID)


def _half_region(ref, axis, chip, half, lead=()):
    R, C = ref.shape[-2:]
    if axis == 0:
        rs = R // N_CHIPS
        return ref.at[(*lead, pl.ds(chip * rs + half * (rs // 2), rs // 2), slice(None))]
    cs = C // N_CHIPS
    return ref.at[(*lead, pl.ds(half * (R // 2), R // 2), pl.ds(chip * cs, cs))]


def _cast_place(n, w, l, where, name):
    _, rs, cs = w.shape
    tr = _blk(rs, (256, 128, 64, 32, 16))
    nt = rs // tr
    if SHARD_AXIS[n] == 0:
        full = (rs * N_CHIPS, cs)
        o_spec = pl.BlockSpec((tr, cs), lambda i, wh: (wh[0] * nt + i, 0))
    else:
        full = (rs, cs * N_CHIPS)
        o_spec = pl.BlockSpec((tr, cs), lambda i, wh: (i, wh[0]))

    def body(wh, w_ref, o_ref):
        o_ref[...] = w_ref[...].astype(BF16)

    return pl.pallas_call(
        body, name=name,
        grid_spec=pltpu.PrefetchScalarGridSpec(
            num_scalar_prefetch=1, grid=(nt,),
            in_specs=[pl.BlockSpec((None, tr, cs), lambda i, wh: (l, i, 0))], out_specs=o_spec),
        out_shape=jax.ShapeDtypeStruct(full, BF16), compiler_params=_cparams(("parallel",)),
    )(where, w)


def _allgather_weights(placed, name):
    names = list(BIG)

    def body(*refs):
        outs = refs[NB:2 * NB]
        send, recv = refs[2 * NB:]
        x, y, c, chips = _place()
        me = 2 * x + y
        sib = (x, y, 1 - c)
        sends = []
        for j, chip in enumerate(chips):
            for wi, n in enumerate(names):
                reg = _half_region(outs[wi], SHARD_AXIS[n], me, c)
                cp = _remote(reg, reg, send.at[j * NB + wi], recv.at[j * NB + wi], (*chip, c))
                cp.start()
                sends.append(cp)
        for j, chip in enumerate(chips):
            them = 2 * chip[0] + chip[1]
            for wi, n in enumerate(names):
                reg = _half_region(outs[wi], SHARD_AXIS[n], them, c)
                _remote(reg, reg, send.at[j * NB + wi], recv.at[j * NB + wi], (*chip, c)).wait_recv()
                cp = _remote(reg, reg, send.at[(3 + j) * NB + wi], recv.at[(3 + j) * NB + wi], sib)
                cp.start()
                sends.append(cp)
        for j, chip in enumerate(chips):
            them = 2 * chip[0] + chip[1]
            for wi, n in enumerate(names):
                reg = _half_region(outs[wi], SHARD_AXIS[n], them, 1 - c)
                _remote(reg, reg, send.at[(3 + j) * NB + wi], recv.at[(3 + j) * NB + wi], sib).wait_recv()
        for cp in sends:
            cp.wait_send()

    outs = pl.pallas_call(
        body, name=name, in_specs=[ANY] * NB, out_specs=[ANY] * NB,
        out_shape=[jax.ShapeDtypeStruct(placed[n].shape, BF16) for n in names],
        scratch_shapes=[pltpu.SemaphoreType.DMA((6 * NB,)), pltpu.SemaphoreType.DMA((6 * NB,))],
        input_output_aliases={wi: wi for wi in range(NB)},
    )(*[placed[n] for n in names])
    return dict(zip(names, outs))


def _comm_spec(ins, out_shapes, aliases, nsem, start, wait):
    return dict(ins=list(ins), out_shapes=list(out_shapes), aliases=dict(aliases), nsem=nsem, start=start, wait=wait)


def _run_comm(spec, name):
    ni, no = len(spec["ins"]), len(spec["out_shapes"])

    def body(*refs):
        ins, outs = refs[:ni], refs[ni:ni + no]
        send, recv = refs[ni + no:]
        spec["start"](ins, outs, send, recv)
        spec["wait"](ins, outs, send, recv)

    return pl.pallas_call(
        body, name=name, in_specs=[ANY] * ni, out_specs=[ANY] * no, out_shape=spec["out_shapes"],
        scratch_shapes=[pltpu.SemaphoreType.DMA((spec["nsem"],)), pltpu.SemaphoreType.DMA((spec["nsem"],))],
        input_output_aliases=spec["aliases"],
    )(*spec["ins"])


def _ag_ici_spec(placed):
    names = list(BIG)

    def start(ins, outs, send, recv):
        x, y, c, chips = _place()
        me = 2 * x + y
        for j, chip in enumerate(chips):
            for wi, n in enumerate(names):
                reg = _half_region(outs[wi], SHARD_AXIS[n], me, c)
                _remote(reg, reg, send.at[j * NB + wi], recv.at[j * NB + wi], (*chip, c)).start()

    def wait(ins, outs, send, recv):
        x, y, c, chips = _place()
        me = 2 * x + y
        for j, chip in enumerate(chips):
            them = 2 * chip[0] + chip[1]
            for wi, n in enumerate(names):
                reg = _half_region(outs[wi], SHARD_AXIS[n], them, c)
                _remote(reg, reg, send.at[j * NB + wi], recv.at[j * NB + wi], (*chip, c)).wait_recv()
        for j, chip in enumerate(chips):
            for wi, n in enumerate(names):
                reg = _half_region(outs[wi], SHARD_AXIS[n], me, c)
                _remote(reg, reg, send.at[j * NB + wi], recv.at[j * NB + wi], (*chip, c)).wait_send()

    return _comm_spec([placed[n] for n in names], [jax.ShapeDtypeStruct(placed[n].shape, BF16) for n in names],
                      {wi: wi for wi in range(NB)}, 3 * NB, start, wait)


def _ag_sibling_spec(arrays):
    names = list(BIG)

    def start(ins, outs, send, recv):
        x, y, c, chips = _place()
        for j, chip in enumerate(chips):
            them = 2 * chip[0] + chip[1]
            for wi, n in enumerate(names):
                reg = _half_region(outs[wi], SHARD_AXIS[n], them, c)
                _remote(reg, reg, send.at[j * NB + wi], recv.at[j * NB + wi], (x, y, 1 - c)).start()

    def wait(ins, outs, send, recv):
        x, y, c, chips = _place()
        for j, chip in enumerate(chips):
            them = 2 * chip[0] + chip[1]
            for wi, n in enumerate(names):
                reg = _half_region(outs[wi], SHARD_AXIS[n], them, 1 - c)
                _remote(reg, reg, send.at[j * NB + wi], recv.at[j * NB + wi], (x, y, 1 - c)).wait_recv()
        for j, chip in enumerate(chips):
            them = 2 * chip[0] + chip[1]
            for wi, n in enumerate(names):
                reg = _half_region(outs[wi], SHARD_AXIS[n], them, c)
                _remote(reg, reg, send.at[j * NB + wi], recv.at[j * NB + wi], (x, y, 1 - c)).wait_send()

    return _comm_spec([arrays[n] for n in names], [jax.ShapeDtypeStruct(arrays[n].shape, BF16) for n in names],
                      {wi: wi for wi in range(NB)}, 3 * NB, start, wait)


def _allgather_gains(g_shard, name):
    A, Cs = g_shard.shape

    def body(in_ref, out_ref, send, recv, local):
        x, y, c, chips = _place()
        me = 2 * x + y
        mine = pltpu.make_async_copy(in_ref, out_ref.at[me], local)
        mine.start()
        cps = [_remote(in_ref, out_ref.at[me], send.at[j], recv.at[j], (*chip, c)) for j, chip in enumerate(chips)]
        for cp in cps:
            cp.start()
        for j, chip in enumerate(chips):
            them = 2 * chip[0] + chip[1]
            _remote(in_ref, out_ref.at[them], send.at[j], recv.at[j], (*chip, c)).wait_recv()
        for cp in cps:
            cp.wait_send()
        mine.wait()

    return pl.pallas_call(
        body, name=name, in_specs=[ANY], out_specs=ANY,
        out_shape=jax.ShapeDtypeStruct((N_CHIPS, A, Cs), F32),
        scratch_shapes=[pltpu.SemaphoreType.DMA((3,)), pltpu.SemaphoreType.DMA((3,)), pltpu.SemaphoreType.DMA],
    )(g_shard)


def _half_shape(n, shape):
    R, C = shape
    if SHARD_AXIS[n] == 0:
        return (N_CHIPS, R // N_CHIPS // 2, C)
    return (R // 2, C)


def _as_halves(n, g):
    R, C = g.shape
    if SHARD_AXIS[n] == 0:
        return g.reshape(N_CHIPS, R // N_CHIPS, C)
    return g


def _my_half(n, ref, half):
    if SHARD_AXIS[n] == 0:
        hs = ref.shape[1] // 2
        return ref.at[:, pl.ds(half * hs, hs), :]
    hs = ref.shape[0] // 2
    return ref.at[pl.ds(half * hs, hs), :]


def _swap_sibling_halves(gw, name):
    names = list(BIG)

    def body(*refs):
        ins = refs[:NB]
        outs = refs[NB:2 * NB]
        send, recv = refs[2 * NB:]
        x, y, c, _ = _place()
        sib = (x, y, 1 - c)
        cps = []
        for wi, n in enumerate(names):
            cp = _remote(_my_half(n, ins[wi], 1 - c), outs[wi], send.at[wi], recv.at[wi], sib)
            cp.start()
            cps.append(cp)
        for cp in cps:
            cp.wait()

    shapes = []
    for n in names:
        g = gw[n]
        shapes.append((g.shape[0], g.shape[1] // 2, g.shape[2]) if g.ndim == 3 else (g.shape[0] // 2, g.shape[1]))
    outs = pl.pallas_call(
        body, name=name, in_specs=[ANY] * NB, out_specs=[ANY] * NB,
        out_shape=[jax.ShapeDtypeStruct(s, BF16) for s in shapes],
        scratch_shapes=[pltpu.SemaphoreType.DMA((NB,)), pltpu.SemaphoreType.DMA((NB,))],
    )(*[gw[n] for n in names])
    return dict(zip(names, outs))


def _pair_sum(n, g, other, c_arr, name):
    if g.ndim == 3:
        nc, rs, C = g.shape
        hs = rs // 2
        tr = _blk(hs, (256, 128, 64, 32, 16))
        nt = hs // tr
        grid = (nc, nt)
        g_spec = pl.BlockSpec((1, tr, C), lambda k, i, c_ref: (k, c_ref[0] * nt + i, 0))
        o_spec = pl.BlockSpec((1, tr, C), lambda k, i, c_ref: (k, i, 0))
    else:
        R, C = g.shape
        hs = R // 2
        cs = C // N_CHIPS
        tr = _blk(hs, (256, 128, 64, 32, 16))
        nt = hs // tr
        grid = (nt, N_CHIPS)
        g_spec = pl.BlockSpec((tr, cs), lambda i, k, c_ref: (c_ref[0] * nt + i, k))
        o_spec = pl.BlockSpec((tr, cs), lambda i, k, c_ref: (i, k))

    def body(c_ref, g_ref, o_ref, out_ref):
        out_ref[...] = (g_ref[...].astype(F32) + o_ref[...].astype(F32)).astype(BF16)

    return pl.pallas_call(
        body, name=name,
        grid_spec=pltpu.PrefetchScalarGridSpec(num_scalar_prefetch=1, grid=grid, in_specs=[g_spec, o_spec],
                                               out_specs=o_spec),
        out_shape=jax.ShapeDtypeStruct(other.shape, BF16),
        compiler_params=_cparams(("parallel", "parallel")),
    )(c_arr, g, other)


def _scatter_spec(pairs):
    names = list(BIG)

    def piece(n, ref, chip):
        if SHARD_AXIS[n] == 0:
            return ref.at[chip]
        cs = ref.shape[1] // N_CHIPS
        return ref.at[:, pl.ds(chip * cs, cs)]

    def start(ins, outs, send, recv):
        x, y, c, chips = _place()
        for j, chip in enumerate(chips):
            them = 2 * chip[0] + chip[1]
            for wi, n in enumerate(names):
                _remote(piece(n, ins[wi], them), outs[wi].at[j], send.at[j * NB + wi], recv.at[j * NB + wi],
                        (*chip, c)).start()

    def wait(ins, outs, send, recv):
        x, y, c, chips = _place()
        me = 2 * x + y
        for j, chip in enumerate(chips):
            for wi, n in enumerate(names):
                _remote(piece(n, ins[wi], me), outs[wi].at[j], send.at[j * NB + wi], recv.at[j * NB + wi],
                        (*chip, c)).wait_recv()
        for j, chip in enumerate(chips):
            them = 2 * chip[0] + chip[1]
            for wi, n in enumerate(names):
                _remote(piece(n, ins[wi], them), outs[wi].at[j], send.at[j * NB + wi], recv.at[j * NB + wi],
                        (*chip, c)).wait_send()

    shapes = []
    for n in names:
        p = pairs[n]
        shapes.append((3,) + (p.shape[1:] if p.ndim == 3 else (p.shape[0], p.shape[1] // N_CHIPS)))
    return _comm_spec([pairs[n] for n in names], [jax.ShapeDtypeStruct(s, BF16) for s in shapes], {}, 3 * NB,
                      start, wait)


def _sum_final(n, pair, q, stack, l, where, name):
    hs, cc = q.shape[1:]
    tr = _blk(hs, (256, 128, 64, 32, 16))
    nt = hs // tr
    if SHARD_AXIS[n] == 0:
        p_spec = pl.BlockSpec((None, tr, cc), lambda i, wh: (wh[0], i, 0))
    else:
        p_spec = pl.BlockSpec((tr, cc), lambda i, wh: (i, wh[0]))

    def body(wh, p_ref, q_ref, stack_any, o_ref):
        o_ref[...] = ((p_ref[...].astype(F32) + q_ref[0].astype(F32)) + q_ref[1].astype(F32)) + q_ref[2].astype(F32)

    return pl.pallas_call(
        body, name=name,
        grid_spec=pltpu.PrefetchScalarGridSpec(
            num_scalar_prefetch=1, grid=(nt,),
            in_specs=[p_spec, pl.BlockSpec((3, tr, cc), lambda i, wh: (0, i, 0)), ANY],
            out_specs=pl.BlockSpec((None, tr, cc), lambda i, wh: (l, wh[1] * nt + i, 0))),
        out_shape=jax.ShapeDtypeStruct(stack.shape, F32),
        input_output_aliases={3: 0},
        compiler_params=_cparams(("parallel",)),
    )(where, pair, q, stack)


def _sum_slots(q, name):
    K, A, B = q.shape
    tr = _blk(A, (256, 128, 64, 32, 16, 8))

    def body(q_ref, o_ref):
        s = q_ref[0].astype(F32)
        for k in range(1, K):
            s = s + q_ref[k].astype(F32)
        o_ref[...] = s

    return pl.pallas_call(
        body, name=name, grid=(A // tr,),
        in_specs=[pl.BlockSpec((K, tr, B), lambda i: (0, i, 0))], out_specs=pl.BlockSpec((tr, B), lambda i: (i, 0)),
        out_shape=jax.ShapeDtypeStruct((A, B), F32), compiler_params=_cparams(("parallel",)),
    )(q)


def _finish_grads(stacks, small, name):
    names = list(BIG)
    A = small.shape[0]
    nh = NB

    def body(*refs):
        small_ref = refs[nh]
        outs = refs[nh + 1:nh + 1 + NB]
        slots = refs[nh + 1 + NB]
        send, recv, local = refs[nh + 2 + NB:]
        x, y, c, _ = _place()
        sib = (x, y, 1 - c)
        me = 4 * x + 2 * y + c
        locals_, sends = [], []
        for wi, n in enumerate(names):
            hr = outs[wi].shape[1] // 2
            reg = outs[wi].at[:, pl.ds(c * hr, hr), :]
            cp = _remote(reg, reg, send.at[wi], recv.at[wi], sib)
            cp.start()
            sends.append(cp)
        cp = pltpu.make_async_copy(small_ref, slots.at[me], local)
        cp.start()
        locals_.append(cp)
        peers = []
        for dx in range(2):
            for dy in range(2):
                for dc in range(2):
                    if dx or dy or dc:
                        peers.append((dx, dy, dc))
        for j, (dx, dy, dc) in enumerate(peers):
            to = (jnp.bitwise_xor(x, dx), jnp.bitwise_xor(y, dy), jnp.bitwise_xor(c, dc))
            cp = _remote(small_ref, slots.at[me], send.at[nh + j], recv.at[nh + j], to)
            cp.start()
            sends.append(cp)
        for wi, n in enumerate(names):
            hr = outs[wi].shape[1] // 2
            reg = outs[wi].at[:, pl.ds((1 - c) * hr, hr), :]
            _remote(reg, reg, send.at[wi], recv.at[wi], sib).wait_recv()
        for j, (dx, dy, dc) in enumerate(peers):
            frm = 4 * jnp.bitwise_xor(x, dx) + 2 * jnp.bitwise_xor(y, dy) + jnp.bitwise_xor(c, dc)
            _remote(small_ref, slots.at[frm], send.at[nh + j], recv.at[nh + j], sib).wait_recv()
        for cp in sends:
            cp.wait_send()
        for cp in locals_:
            cp.wait()

    res = pl.pallas_call(
        body, name=name, in_specs=[ANY] * (nh + 1), out_specs=[ANY] * (NB + 1),
        out_shape=[jax.ShapeDtypeStruct(stacks[n].shape, F32) for n in names]
        + [jax.ShapeDtypeStruct((8, A, LANES), F32)],
        scratch_shapes=[pltpu.SemaphoreType.DMA((nh + 7,)), pltpu.SemaphoreType.DMA((nh + 7,)),
                        pltpu.SemaphoreType.DMA],
        input_output_aliases={wi: wi for wi in range(NB)},
    )(*[stacks[n] for n in names], small)
    return dict(zip(names, res[:NB])), res[NB]


def _adamw(w, g, m, v, name):
    shape = w.shape
    if w.ndim == 2:
        w, g, m, v = (t.reshape((1,) + shape) for t in (w, g, m, v))
    L, A, B = w.shape
    tr = _blk(A, (128, 64, 32, 16, 8))
    c1 = 1.0 - ADAM_B1 ** ADAM_STEP
    c2 = 1.0 - ADAM_B2 ** ADAM_STEP

    def body(w_ref, g_ref, m_ref, v_ref, d_ref, nm_ref, nv_ref):
        gv = g_ref[...]
        nm = ADAM_B1 * m_ref[...] + (1.0 - ADAM_B1) * gv
        nv = ADAM_B2 * v_ref[...] + (1.0 - ADAM_B2) * (gv * gv)
        m_hat = nm / c1
        v_hat = nv / c2
        d_ref[...] = -ADAM_LR * (m_hat / (jnp.sqrt(v_hat) + ADAM_EPS) + ADAM_WD * w_ref[...])
        nm_ref[...] = nm
        nv_ref[...] = nv

    spec = pl.BlockSpec((1, tr, B), lambda l, i: (l, i, 0))
    sds = jax.ShapeDtypeStruct((L, A, B), F32)
    outs = pl.pallas_call(
        body, name=name, grid=(L, A // tr), in_specs=[spec] * 4, out_specs=[spec] * 3, out_shape=[sds] * 3,
        compiler_params=_cparams(("parallel", "parallel")),
    )(w, g, m, v)
    return tuple(o.reshape(shape) for o in outs)


SMALL_ROWS = 520


def _train_step(x, mem, target, w, m, v):
    L = w["lb_logits"].shape[0]
    cx, cy, cc = lax.axis_index("x"), lax.axis_index("y"), lax.axis_index("c")
    me = 2 * cx + cy
    c_arr = jnp.reshape(cc, (1,)).astype(jnp.int32)
    where = jnp.stack([me, cc]).astype(jnp.int32)

    placed = [{n: _cast_place(n, w[n], l, where, f"cast_{n}_l{l}") for n in BIG} for l in range(L)]
    full = [None] * L
    full[0] = _allgather_weights(placed[0], "allgather_l0")
    gs = _allgather_gains(w["norm_gains"].reshape(L * N_NORMS, -1), "allgather_gains")
    gains = jnp.transpose(gs, (1, 0, 2)).reshape(L, N_NORMS, D_MODEL)

    def fwd_comm(l):
        return _ag_ici_spec(placed[l + 1]) if l + 1 < L else None

    def fwd_done(l, outs):
        full[l + 1] = dict(zip(BIG, _run_comm(_ag_sibling_spec(dict(zip(BIG, outs))), f"allgather_sibling_l{l + 1}")))

    stacks = {n: lax.empty(w[n].shape, F32) for n in BIG}
    pending = {}

    def finish_layer(l, pairs, slots):
        for n in BIG:
            stacks[n] = _sum_final(n, pairs[n], slots[n], stacks[n], l, where, f"sum_chips_{n}_l{l}")

    def layer_grads(l, gw):
        views = {n: _as_halves(n, gw[n]) for n in BIG}
        theirs = _swap_sibling_halves(views, f"swap_halves_l{l}")
        pairs = {n: _pair_sum(n, views[n], theirs[n], c_arr, f"pair_sum_{n}_l{l}") for n in BIG}
        if l == 0:
            finish_layer(0, pairs, dict(zip(BIG, _run_comm(_scatter_spec(pairs), "scatter_l0"))))
        else:
            pending[l] = pairs
        return None

    def bwd_comm(l):
        return _scatter_spec(pending[l + 1]) if l + 1 in pending else None

    def bwd_done(l, outs):
        finish_layer(l + 1, pending.pop(l + 1), dict(zip(BIG, outs)))

    loss, dx, _, d_rb, d_lb, d_gains, d_hg = _local_step(
        x[0], mem[0], target[0], w["rel_bias"], w["lb_logits"], gains, w["hg_norm"], lambda l: full[l],
        dict(fwd_comm=fwd_comm, fwd_done=fwd_done, bwd_comm=bwd_comm, bwd_done=bwd_done, layer_grads=layer_grads))

    flat = jnp.concatenate([d_rb.reshape(-1), d_lb.reshape(-1), d_hg.reshape(-1), d_gains.reshape(-1)])
    small = jnp.pad(flat, (0, SMALL_ROWS * LANES - flat.shape[0])).reshape(SMALL_ROWS, LANES)
    grads, slots = _finish_grads(stacks, small, "finish_grads")
    tot = _sum_slots(slots, "sum_small").reshape(-1)
    n_rb, n_lb = d_rb.size, d_lb.size
    grads["rel_bias"] = tot[:n_rb].reshape(d_rb.shape)
    grads["lb_logits"] = tot[n_rb:n_rb + n_lb].reshape(d_lb.shape)
    grads["hg_norm"] = tot[n_rb + n_lb:n_rb + 2 * n_lb].reshape(d_hg.shape)
    g_full = tot[n_rb + 2 * n_lb:n_rb + 2 * n_lb + d_gains.size].reshape(d_gains.shape)
    cs = D_MODEL // N_CHIPS
    grads["norm_gains"] = lax.dynamic_slice_in_dim(g_full, me * cs, cs, axis=2)

    delta, new_m, new_v = {}, {}, {}
    for n in OUT_ORDER:
        delta[n], new_m[n], new_v[n] = _adamw(w[n], grads[n], m[n], v[n], f"adamw_{n}")
    loss = lax.psum(loss, ("x", "y", "c"))
    return (loss, dx[None], *[grads[n] for n in OUT_ORDER], *[delta[n] for n in OUT_ORDER],
            *[new_m[n] for n in OUT_ORDER], *[new_v[n] for n in OUT_ORDER])
```

```python
import functools
import math

import numpy as np
import jax
import jax.numpy as jnp
from jax import lax
from jax.experimental import pallas as pl
from jax.experimental.pallas import tpu as pltpu

F32 = jnp.float32
BF16 = jnp.bfloat16
MESH_ID = pl.DeviceIdType.MESH

D_MODEL = 2048
HEAD_DIM = 128
N_HEADS = 8
ATTN_W = 1024
HG_W = 1024
HG_CHUNK = 64
Q_BLOCK = 128
DILATIONS = (16, 4, 1)
REL_BUCKETS = 32
REL_MAX_DIST = 2048
CROSS_HEADS = 4
CROSS_DIM = 512
D_FF = 5632
RMS_EPS = 1e-6
NEG_INF = -1e30
N_NORMS = 7

ADAM_LR = 0.001
ADAM_B1 = 0.9
ADAM_B2 = 0.999
ADAM_EPS = 1e-08
ADAM_WD = 0.01
ADAM_STEP = 10

VMEM_LIMIT_V7X = 56 * 1024 * 1024
LANES = 128
N_CHIPS = 4


def _cparams(sem=None):
    if sem is None:
        return pltpu.CompilerParams(vmem_limit_bytes=VMEM_LIMIT_V7X)
    return pltpu.CompilerParams(dimension_semantics=sem, vmem_limit_bytes=VMEM_LIMIT_V7X)


def _blk(n, prefs):
    for p in prefs:
        if p <= n and n % p == 0:
            return p
    return n


def _dot(a, b, dims):
    return lax.dot_general(a.astype(BF16), b.astype(BF16), (dims, ((), ())), preferred_element_type=F32)


def _dot_nn(a, b):
    return _dot(a, b, ((1,), (0,)))


def _dot_nt(a, b):
    return _dot(a, b, ((1,), (1,)))


def _dot_tn(a, b):
    return _dot(a, b, ((0,), (0,)))


def _sigmoid(x):
    return 1.0 / (1.0 + jnp.exp(-x))


def _mm(a_list, b_list, mode, out_dtype, name):
    if not isinstance(a_list, (list, tuple)):
        a_list = [a_list]
    if not isinstance(b_list, (list, tuple)):
        b_list = [b_list]
    na, nb = len(a_list), len(b_list)
    deep = (2048, 2816, 1024, 1408, 512, 256, 128)
    if mode == "nn":
        M, K = a_list[0].shape
        N = b_list[0].shape[1]
        bm, bn, bk = _blk(M, (1024, 512, 256, 128)), _blk(N, (1024, 512, 256, 128)), _blk(K, deep)
    elif mode == "nt":
        M = a_list[0].shape[0]
        K = sum(a.shape[1] for a in a_list)
        N = b_list[0].shape[0]
        bm, bn = _blk(M, (1024, 512, 256, 128)), _blk(N, (1024, 512, 256, 128))
        bk = _blk(math.gcd(*[a.shape[1] for a in a_list]), deep)
    else:
        K, M = a_list[0].shape
        N = sum(b.shape[1] for b in b_list)
        bn = _blk(math.gcd(*[b.shape[1] for b in b_list]), (1024, 2816, 512, 256, 128))
        wide = bn > 1024
        bm = _blk(M, (512, 256, 128) if wide else (1024, 1408, 512, 256, 128))
        bk = _blk(K, (1024, 512, 256, 128) if wide else (2048, 1024, 512, 256, 128))
    nk = K // bk
    grid = (M // bm, N // bn, nk)

    if mode == "nn":
        a_specs = [pl.BlockSpec((bm, bk), lambda i, j, k: (i, k))]
        b_specs = [pl.BlockSpec((bk, bn), lambda i, j, k: (k, j))]
        dims = ((1,), (0,))
    elif mode == "nt":
        a_specs, off = [], 0
        for a in a_list:
            n_i = a.shape[1] // bk
            a_specs.append(pl.BlockSpec((bm, bk), functools.partial(
                lambda i, j, k, off, n_i: (i, jnp.clip(k - off, 0, n_i - 1)), off=off, n_i=n_i)))
            off += n_i
        b_specs = [pl.BlockSpec((bn, bk), lambda i, j, k: (j, k))]
        dims = ((1,), (1,))
    else:
        a_specs = [pl.BlockSpec((bk, bm), lambda i, j, k: (k, i))]
        b_specs, off = [], 0
        for b in b_list:
            n_j = b.shape[1] // bn
            b_specs.append(pl.BlockSpec((bk, bn), functools.partial(
                lambda i, j, k, off, n_j: (jnp.where((j >= off) & (j < off + n_j), k, 0),
                                           jnp.clip(j - off, 0, n_j - 1)), off=off, n_j=n_j)))
            off += n_j
        dims = ((0,), (0,))
    a_bounds = np.cumsum([0] + [a.shape[1] // bk for a in a_list]) if mode == "nt" else None
    b_bounds = np.cumsum([0] + [b.shape[1] // bn for b in b_list]) if mode == "tn" else None

    def body(*refs):
        a_refs = refs[:na]
        b_refs = refs[na:na + nb]
        o_ref = refs[na + nb]
        acc_ref = refs[na + nb + 1] if nk > 1 else None
        j = pl.program_id(1)
        k = pl.program_id(2)

        def accumulate(p):
            if nk == 1:
                o_ref[...] = p.astype(out_dtype)
                return

            @pl.when(k == 0)
            def _():
                acc_ref[...] = p

            @pl.when(k > 0)
            def _():
                acc_ref[...] += p

        if na > 1:
            for t in range(na):
                @pl.when((k >= int(a_bounds[t])) & (k < int(a_bounds[t + 1])))
                def _(t=t):
                    accumulate(_dot(a_refs[t][...], b_refs[0][...], dims))
        elif nb > 1:
            for t in range(nb):
                @pl.when((j >= int(b_bounds[t])) & (j < int(b_bounds[t + 1])))
                def _(t=t):
                    accumulate(_dot(a_refs[0][...], b_refs[t][...], dims))
        else:
            accumulate(_dot(a_refs[0][...], b_refs[0][...], dims))

        if nk > 1:
            @pl.when(k == nk - 1)
            def _():
                o_ref[...] = acc_ref[...].astype(out_dtype)

    return pl.pallas_call(
        body, name=name, grid=grid,
        in_specs=a_specs + b_specs,
        out_specs=pl.BlockSpec((bm, bn), lambda i, j, k: (i, j)),
        out_shape=jax.ShapeDtypeStruct((M, N), out_dtype),
        scratch_shapes=[pltpu.VMEM((bm, bn), F32)] if nk > 1 else [],
        compiler_params=_cparams(("parallel", "parallel", "arbitrary")),
    )(*a_list, *b_list)


def _rms_fwd(x, g, name):
    S, D = x.shape
    tm = _blk(S, (256, 128, 64, 8))

    def body(x_ref, g_ref, o_ref):
        xv = x_ref[...]
        r = lax.rsqrt(jnp.mean(xv * xv, axis=-1, keepdims=True) + RMS_EPS)
        o_ref[...] = ((xv * r) * g_ref[...]).astype(o_ref.dtype)

    return pl.pallas_call(
        body, name=name, grid=(S // tm,),
        in_specs=[pl.BlockSpec((tm, D), lambda i: (i, 0)), pl.BlockSpec((1, D), lambda i: (0, 0))],
        out_specs=pl.BlockSpec((tm, D), lambda i: (i, 0)),
        out_shape=jax.ShapeDtypeStruct((S, D), BF16),
        compiler_params=_cparams(("parallel",)),
    )(x, g)


def _norm_residual(x, t, g, g_next, name):
    S, D = x.shape
    tm = _blk(S, (256, 128, 64, 8))
    chain = g_next is not None

    def body(*refs):
        if chain:
            x_ref, t_ref, g_ref, gn_ref, o_ref, h_ref = refs
        else:
            x_ref, t_ref, g_ref, o_ref = refs
        tv = t_ref[...]
        r = lax.rsqrt(jnp.mean(tv * tv, axis=-1, keepdims=True) + RMS_EPS)
        xn = x_ref[...] + (tv * r) * g_ref[...]
        o_ref[...] = xn
        if chain:
            rn = lax.rsqrt(jnp.mean(xn * xn, axis=-1, keepdims=True) + RMS_EPS)
            h_ref[...] = ((xn * rn) * gn_ref[...]).astype(BF16)

    row = pl.BlockSpec((tm, D), lambda i: (i, 0))
    vec = pl.BlockSpec((1, D), lambda i: (0, 0))
    return pl.pallas_call(
        body, name=name, grid=(S // tm,),
        in_specs=[row, row, vec] + ([vec] if chain else []),
        out_specs=[row, row] if chain else row,
        out_shape=[jax.ShapeDtypeStruct((S, D), F32), jax.ShapeDtypeStruct((S, D), BF16)] if chain
        else jax.ShapeDtypeStruct((S, D), F32),
        compiler_params=_cparams(("parallel",)),
    )(*([x, t, g, g_next] if chain else [x, t, g]))


def _rms_bwd(x, g, dh, res, out_dtype, name):
    S, D = x.shape
    tm = _blk(S, (256, 128, 64, 8))
    has_res = res is not None

    def body(*refs):
        if has_res:
            x_ref, g_ref, dh_ref, res_ref, dx_ref, dg_ref = refs
        else:
            x_ref, g_ref, dh_ref, dx_ref, dg_ref = refs
        xv = x_ref[...]
        r = lax.rsqrt(jnp.mean(xv * xv, axis=-1, keepdims=True) + RMS_EPS)
        xh = xv * r
        dhv = dh_ref[...].astype(F32)
        gd = dhv * g_ref[...]
        dx = r * (gd - xh * jnp.mean(gd * xh, axis=-1, keepdims=True))
        if has_res:
            dx = dx + res_ref[...]
        dx_ref[...] = dx.astype(out_dtype)
        part = jnp.sum(dhv * xh, axis=0, keepdims=True)

        @pl.when(pl.program_id(0) == 0)
        def _():
            dg_ref[...] = part

        @pl.when(pl.program_id(0) > 0)
        def _():
            dg_ref[...] += part

    row = pl.BlockSpec((tm, D), lambda i: (i, 0))
    vec = pl.BlockSpec((1, D), lambda i: (0, 0))
    ins = [x, g, dh] + ([res] if has_res else [])
    return pl.pallas_call(
        body, name=name, grid=(S // tm,),
        in_specs=[row, vec, row] + ([row] if has_res else []),
        out_specs=[row, vec],
        out_shape=[jax.ShapeDtypeStruct((S, D), out_dtype), jax.ShapeDtypeStruct((1, D), F32)],
        compiler_params=_cparams(("arbitrary",)),
    )(*ins)


def _rel_bucket(dist):
    max_exact = REL_BUCKETS // 2
    d_f = jnp.maximum(dist, 1).astype(jnp.float32)
    large = max_exact + (jnp.log(d_f / max_exact) / math.log(REL_MAX_DIST / max_exact)
                         * (REL_BUCKETS - max_exact)).astype(jnp.int32)
    large = jnp.minimum(large, REL_BUCKETS - 1)
    return jnp.where(dist < max_exact, dist, large)


def _bias_tables(rel_bias):
    qi = jnp.arange(Q_BLOCK)[:, None]
    kj = jnp.arange(2 * Q_BLOCK)[None, :]
    m = qi - kj + Q_BLOCK
    band_ok = (m >= 0) & (m <= Q_BLOCK)
    tabs = []
    for d in DILATIONS:
        bucket = _rel_bucket(jnp.maximum(m, 0) * d)
        onehot = (bucket[:, :, None] == jnp.arange(REL_BUCKETS)[None, None, :]).astype(F32)
        bias = jnp.einsum("qkb,bh->hqk", onehot, rel_bias.astype(F32), precision=lax.Precision.HIGHEST)
        tabs.append(jnp.where(band_ok[None], bias, NEG_INF))
    return jnp.stack(tabs)


HEADS_PER_STEP = {16: 1, 4: 1, 1: 8}


def _rows(r, d):
    return pl.ds(r, Q_BLOCK, stride=d) if d > 1 else pl.ds(0, Q_BLOCK)


def _attn_fwd(proj, bias, state, d, last, cat_in, name):
    S = proj.shape[0]
    R = Q_BLOCK * d
    nsb = S // R
    first = state is None
    scale = HEAD_DIM ** -0.5
    hb = HEADS_PER_STEP[d]
    W = hb * HEAD_DIM

    def body(*refs):
        q_ref, kp_ref, kc_ref, vp_ref, vc_ref, b_ref = refs[:6]
        pos = 6
        if not first:
            m_in, l_in, a_in = refs[pos:pos + 3]
            pos += 3
        if last:
            pos += 1
            o_ref, lse_ref, o_tmp = refs[pos:pos + 3]
        else:
            m_out, l_out, a_out = refs[pos:pos + 3]
        n = pl.program_id(0)
        for hh in range(hb):
            ls = slice(hh * HEAD_DIM, (hh + 1) * HEAD_DIM)
            bp = b_ref[hh, :, :Q_BLOCK]
            bc = b_ref[hh, :, Q_BLOCK:]
            for r in range(d):
                rows = _rows(r, d)
                q = q_ref[rows, ls]
                sp = _dot_nt(q, kp_ref[rows, ls]) * scale + bp
                sp = jnp.where(n == 0, NEG_INF, sp)
                sc = _dot_nt(q, kc_ref[rows, ls]) * scale + bc
                mrow = jnp.maximum(jnp.max(sp, axis=-1, keepdims=True), jnp.max(sc, axis=-1, keepdims=True))
                if first:
                    m_new = mrow
                else:
                    m_old = m_in[rows, ls][:, :1]
                    m_new = jnp.maximum(m_old, mrow)
                pp = jnp.exp(sp - m_new)
                pc = jnp.exp(sc - m_new)
                lrow = jnp.sum(pp, axis=-1, keepdims=True) + jnp.sum(pc, axis=-1, keepdims=True)
                pv = _dot_nn(pp, vp_ref[rows, ls]) + _dot_nn(pc, vc_ref[rows, ls])
                if first:
                    l_new, a_new = lrow, pv
                else:
                    alpha = jnp.exp(m_old - m_new)
                    l_new = alpha * l_in[rows, ls][:, :1] + lrow
                    a_new = alpha * a_in[rows, ls] + pv
                if last:
                    o_tmp[rows, ls] = a_new / l_new
                    lse_ref[rows, ls] = jnp.broadcast_to(m_new + jnp.log(l_new), (Q_BLOCK, HEAD_DIM))
                else:
                    m_out[rows, ls] = jnp.broadcast_to(m_new, (Q_BLOCK, HEAD_DIM))
                    l_out[rows, ls] = jnp.broadcast_to(l_new, (Q_BLOCK, HEAD_DIM))
                    a_out[rows, ls] = a_new
        if last:
            o_ref[...] = o_tmp[...].astype(BF16)

    ng = N_HEADS // hb

    def col(g0):
        return pl.BlockSpec((R, W), lambda n, h: (n, g0 * ng + h))

    def col_prev(g0):
        return pl.BlockSpec((R, W), lambda n, h: (jnp.maximum(n - 1, 0), g0 * ng + h))

    in_specs = [col(0), col_prev(1), col(1), col_prev(2), col(2),
                pl.BlockSpec((hb, Q_BLOCK, 2 * Q_BLOCK), lambda n, h: (h, 0, 0))]
    ins = [proj, proj, proj, proj, proj, bias]
    if not first:
        in_specs += [col(0)] * 3
        ins += list(state)
    st = jax.ShapeDtypeStruct((S, ATTN_W), F32)
    if last:
        in_specs.append(pl.BlockSpec(memory_space=pl.ANY))
        ins.append(cat_in)
        out_specs = [col(0), col(0)]
        out_shape = [jax.ShapeDtypeStruct(cat_in.shape, BF16), st]
        scratch = [pltpu.VMEM((R, W), F32)]
        aliases = {len(ins) - 1: 0}
    else:
        out_specs = [col(0)] * 3
        out_shape = [st, st, st]
        scratch = []
        aliases = {}
    return pl.pallas_call(
        body, name=name, grid=(nsb, ng), in_specs=in_specs, out_specs=out_specs,
        out_shape=out_shape, scratch_shapes=scratch, input_output_aliases=aliases,
        compiler_params=_cparams(("arbitrary", "arbitrary")),
    )(*ins)


def _attn_bwd(proj, bias, cat, d_cat, lse, acc, d, last, name):
    S = proj.shape[0]
    R = Q_BLOCK * d
    nsb = S // R
    first = acc is None
    scale = HEAD_DIM ** -0.5
    odt = BF16 if last else F32
    hb = HEADS_PER_STEP[d]
    W = hb * HEAD_DIM

    def body(*refs):
        q_ref, kp_ref, kc_ref, vp_ref, vc_ref, b_ref, o_ref, do_ref, lse_ref = refs[:9]
        pos = 9
        if not first:
            dq_in, dk_in, dv_in = refs[pos:pos + 3]
            pos += 3
        dq_out, dk_out, dv_out, db_out = refs[pos:pos + 4]
        pos += 4
        ck, cv, o_f, do_f, dq_t, dk_t, dv_t, db_acc = refs[pos:pos + 8]
        n = pl.program_id(1)

        @pl.when(n == 0)
        def _():
            db_acc[...] = jnp.zeros_like(db_acc)
            ck[...] = jnp.zeros_like(ck)
            cv[...] = jnp.zeros_like(cv)

        @pl.when(n < nsb)
        def _():
            o_f[...] = o_ref[...].astype(F32)
            do_f[...] = do_ref[...].astype(F32)
            for hh in range(hb):
                ls = slice(hh * HEAD_DIM, (hh + 1) * HEAD_DIM)
                bp = b_ref[hh, :, :Q_BLOCK]
                bc = b_ref[hh, :, Q_BLOCK:]
                for r in range(d):
                    rows = _rows(r, d)
                    q = q_ref[rows, ls]
                    kp = kp_ref[rows, ls]
                    kc = kc_ref[rows, ls]
                    vp = vp_ref[rows, ls]
                    vc = vc_ref[rows, ls]
                    do = do_f[rows, ls]
                    lse_r = lse_ref[rows, ls][:, :1]
                    sp = _dot_nt(q, kp) * scale + bp
                    sp = jnp.where(n == 0, NEG_INF, sp)
                    sc = _dot_nt(q, kc) * scale + bc
                    pp = jnp.exp(sp - lse_r)
                    pc = jnp.exp(sc - lse_r)
                    dd = jnp.sum(do * o_f[rows, ls], axis=-1, keepdims=True)
                    dsp = pp * (_dot_nt(do, vp) - dd)
                    dsc = pc * (_dot_nt(do, vc) - dd)
                    db_acc[hh, :, :Q_BLOCK] += dsp
                    db_acc[hh, :, Q_BLOCK:] += dsc
                    dq = (_dot_nn(dsp, kp) + _dot_nn(dsc, kc)) * scale
                    dk_prev = ck[rows, ls] + _dot_tn(dsp, q) * scale
                    dv_prev = cv[rows, ls] + _dot_tn(pp, do)
                    ck[rows, ls] = _dot_tn(dsc, q) * scale
                    cv[rows, ls] = _dot_tn(pc, do)
                    if not first:
                        dq = dq + dq_in[rows, ls]
                        dk_prev = dk_prev + dk_in[rows, ls]
                        dv_prev = dv_prev + dv_in[rows, ls]
                    dq_t[rows, ls] = dq
                    dk_t[rows, ls] = dk_prev
                    dv_t[rows, ls] = dv_prev
            dq_out[...] = dq_t[...].astype(odt)

            @pl.when(n > 0)
            def _():
                dk_out[...] = dk_t[...].astype(odt)
                dv_out[...] = dv_t[...].astype(odt)

        @pl.when(n == nsb)
        def _():
            if first:
                dk_out[...] = ck[...].astype(odt)
                dv_out[...] = cv[...].astype(odt)
            else:
                dk_out[...] = (ck[...] + dk_in[...]).astype(odt)
                dv_out[...] = (cv[...] + dv_in[...]).astype(odt)
            db_out[...] = db_acc[...]

    last_n = nsb - 1
    ng = N_HEADS // hb

    def cur(g0):
        return pl.BlockSpec((R, W), lambda h, n: (jnp.minimum(n, last_n), g0 * ng + h))

    def prev(g0):
        return pl.BlockSpec((R, W), lambda h, n: (jnp.maximum(jnp.minimum(n, last_n) - 1, 0), g0 * ng + h))

    delayed = pl.BlockSpec((R, W), lambda h, n: (jnp.maximum(n - 1, 0), h))
    in_specs = [cur(0), prev(1), cur(1), prev(2), cur(2),
                pl.BlockSpec((hb, Q_BLOCK, 2 * Q_BLOCK), lambda h, n: (h, 0, 0)),
                cur(0), cur(0), cur(0)]
    ins = [proj, proj, proj, proj, proj, bias, cat, d_cat, lse]
    if not first:
        in_specs += [cur(0), delayed, delayed]
        ins += list(acc)
    st = jax.ShapeDtypeStruct((S, ATTN_W), odt)
    tile = pltpu.VMEM((R, W), F32)
    return pl.pallas_call(
        body, name=name, grid=(ng, nsb + 1), in_specs=in_specs,
        out_specs=[cur(0), delayed, delayed,
                   pl.BlockSpec((hb, Q_BLOCK, 2 * Q_BLOCK), lambda h, n: (h, 0, 0))],
        out_shape=[st, st, st, jax.ShapeDtypeStruct((N_HEADS, Q_BLOCK, 2 * Q_BLOCK), F32)],
        scratch_shapes=[tile] * 7 + [pltpu.VMEM((hb, Q_BLOCK, 2 * Q_BLOCK), F32)],
        compiler_params=_cparams(("arbitrary", "arbitrary")),
    )(*ins)


HG_LEVELS = (32, 16, 8, 4, 2, 1)
N_LEV = len(HG_LEVELS)


def _hg_consts():
    C = HG_CHUNK
    t = np.arange(C)
    mq, mk, masks = [], [], []
    for B in HG_LEVELS:
        up = (t // B) % 2 == 1
        bs = (t // B) * B
        be = bs + B - 1
        mq.append(up[:, None] & (t[None, :] >= bs[:, None]) & (t[None, :] <= t[:, None]))
        mk.append((~up)[:, None] & (t[None, :] > t[:, None]) & (t[None, :] <= be[:, None]))
        masks.append(up[:, None] & (~up)[None, :] & ((t[:, None] // (2 * B)) == (t[None, :] // (2 * B))))
    masks.append(np.eye(C, dtype=bool))
    mb = t[None, :] <= t[:, None]
    mw = t[None, :] > t[:, None]
    m_all = np.concatenate(mq + mk + [mb, mw], axis=0).astype(np.float32)
    return jnp.asarray(m_all, BF16), jnp.asarray(np.stack(masks).astype(np.float32))


def _split3(v):
    hi = v.astype(BF16)
    r1 = v - hi.astype(F32)
    mid = r1.astype(BF16)
    lo = (r1 - mid.astype(F32)).astype(BF16)
    return jnp.concatenate([hi, mid, lo], axis=1)


def _hg_chunk_fwd(fz, iv, qz, lbh, m_all, mask_ref, st_t):
    C = HG_CHUNK
    sig = _sigmoid(fz)
    f = lbh + (1.0 - lbh) * sig
    lf = jnp.log(f)
    kk = 1.0 - f
    sq = _sigmoid(qz)
    qq = qz * sq
    a3 = lax.dot_general(m_all, _split3(lf), (((1,), (0,)), ((), ())), preferred_element_type=F32)
    args = a3[:, :HEAD_DIM] + a3[:, HEAD_DIM:2 * HEAD_DIM] + a3[:, 2 * HEAD_DIM:]
    e = jnp.exp(args)
    qs = [qq * e[j * C:(j + 1) * C] for j in range(N_LEV)]
    ks = [kk * e[(N_LEV + j) * C:(N_LEV + j + 1) * C] for j in range(N_LEV)]
    a = mask_ref[N_LEV] * _dot_nt(qq, kk)
    for j in range(N_LEV):
        a = a + mask_ref[j] * _dot_nt(qs[j], ks[j])
    eb = e[2 * N_LEV * C:(2 * N_LEV + 1) * C]
    ew = e[(2 * N_LEV + 1) * C:]
    qe = qq * eb
    w = kk * ew
    o = _dot_nt(qe, st_t) + _dot_nn(a, iv)
    eb_last = eb[C - 1:C, :]
    new_st = st_t * eb_last + _dot_tn(iv, w)
    return dict(sig=sig, f=f, kk=kk, sq=sq, qq=qq, e=e, qs=qs, ks=ks, a=a, eb=eb, ew=ew, qe=qe, w=w, o=o,
                eb_last=eb_last, new_st=new_st)


def _hg_rows(S):
    return _blk(S, (256, 128, 64))


def _host_refs(refs, n_in, n_out, n_scratch, comm):
    nci = len(comm["ins"]) if comm else 0
    nco = len(comm["out_shapes"]) if comm else 0
    a = n_in
    b = a + nci
    c = b + n_out
    d = c + nco
    e = d + n_scratch
    return refs[:a], refs[a:b], refs[b:c], refs[c:d], refs[d:e], refs[e:]


def _host_call(comm, n_in, n_out, in_specs, out_specs, out_shape, scratch, aliases):
    if comm:
        in_specs = in_specs + [ANY] * len(comm["ins"])
        out_specs = out_specs + [ANY] * len(comm["out_shapes"])
        out_shape = out_shape + comm["out_shapes"]
        scratch = scratch + [pltpu.SemaphoreType.DMA((comm["nsem"],)), pltpu.SemaphoreType.DMA((comm["nsem"],))]
        aliases = dict(aliases)
        aliases.update({n_in + i: n_out + o for i, o in comm["aliases"].items()})
    return dict(in_specs=in_specs, out_specs=out_specs, out_shape=out_shape, scratch_shapes=scratch,
                input_output_aliases=aliases)


def _hgrn_fwd(proj, lb, gain, cat_in, name, comm=None):
    S = proj.shape[0]
    rows = _hg_rows(S)
    cb = rows // HG_CHUNK
    nsteps = S // rows
    m_all, masks = _hg_consts()

    def body(*refs):
        own_in, c_in, own_out, c_out, own_scr, sems = _host_refs(refs, 9, 2, 1, comm)
        fz_ref, iv_ref, qz_ref, gz_ref, lb_ref, gain_ref, m_ref, mask_ref, _ = own_in
        o_ref, st_ref = own_out
        (st_scr,) = own_scr

        @pl.when(pl.program_id(0) == 0)
        def _():
            st_scr[...] = jnp.zeros_like(st_scr)
            if comm:
                comm["start"](c_in, c_out, *sems)

        m_all_v = m_ref[...]

        def chunk(c, carry):
            rs = pl.ds(pl.multiple_of(c * HG_CHUNK, HG_CHUNK), HG_CHUNK)
            for h in range(N_HEADS):
                sl = slice(h * HEAD_DIM, (h + 1) * HEAD_DIM)
                st_t = st_scr[h]
                st_ref[c, h] = st_t
                gz = gz_ref[rs, sl]
                iv = iv_ref[rs, sl]
                q = _hg_chunk_fwd(fz_ref[rs, sl], iv, qz_ref[rs, sl], lb_ref[:, sl], m_all_v, mask_ref, st_t)
                st_scr[h] = q["new_st"]
                o = q["o"]
                r = lax.rsqrt(jnp.mean(o * o, axis=-1, keepdims=True) + RMS_EPS)
                y = ((o * r) * gain_ref[:, sl]) * (gz * _sigmoid(gz))
                o_ref[rs, sl] = y.astype(BF16)
            return carry

        lax.fori_loop(0, cb, chunk, 0)
        if comm:
            @pl.when(pl.program_id(0) == nsteps - 1)
            def _():
                comm["wait"](c_in, c_out, *sems)

    def col(c):
        return pl.BlockSpec((rows, HG_W), lambda i: (i, c))

    vec = pl.BlockSpec((1, HG_W), lambda i: (0, 0))
    args = _host_call(
        comm, 9, 2,
        in_specs=[col(3), col(4), col(5), col(6), vec, vec,
                  pl.BlockSpec(m_all.shape, lambda i: (0, 0)), pl.BlockSpec(masks.shape, lambda i: (0, 0, 0)),
                  pl.BlockSpec(memory_space=pl.ANY)],
        out_specs=[col(1), pl.BlockSpec((cb, N_HEADS, HEAD_DIM, HEAD_DIM), lambda i: (i, 0, 0, 0))],
        out_shape=[jax.ShapeDtypeStruct(cat_in.shape, BF16),
                   jax.ShapeDtypeStruct((S // HG_CHUNK, N_HEADS, HEAD_DIM, HEAD_DIM), F32)],
        scratch=[pltpu.VMEM((N_HEADS, HEAD_DIM, HEAD_DIM), F32)],
        aliases={8: 0})
    return pl.pallas_call(
        body, name=name, grid=(nsteps,), compiler_params=_cparams(("arbitrary",)), **args,
    )(proj, proj, proj, proj, lb, gain, m_all, masks, cat_in, *(comm["ins"] if comm else []))


def _hgrn_bwd(proj, lb, gain, states, d_cat, name, comm=None):
    S = proj.shape[0]
    rows = _hg_rows(S)
    cb = rows // HG_CHUNK
    nblk = S // rows
    C = HG_CHUNK
    m_all, masks = _hg_consts()

    def body(*refs):
        own_in, c_in, own_out, c_out, own_scr, sems = _host_refs(refs, 10, 3, 1, comm)
        fz_ref, iv_ref, qz_ref, gz_ref, lb_ref, gain_ref, m_ref, mask_ref, st_ref, dy_ref = own_in
        dz_ref, dlb_ref, dgain_ref = own_out
        (dst_scr,) = own_scr

        @pl.when(pl.program_id(0) == 0)
        def _():
            dst_scr[...] = jnp.zeros_like(dst_scr)
            dlb_ref[...] = jnp.zeros_like(dlb_ref)
            dgain_ref[...] = jnp.zeros_like(dgain_ref)
            if comm:
                comm["start"](c_in, c_out, *sems)

        m_all_v = m_ref[...]
        last_row = lax.broadcasted_iota(jnp.int32, (C, HEAD_DIM), 0) == C - 1

        def chunk(ci, carry):
            c = cb - 1 - ci
            rs = pl.ds(pl.multiple_of(c * C, C), C)
            for h in range(N_HEADS):
                sl = slice(h * HEAD_DIM, (h + 1) * HEAD_DIM)
                lbh = lb_ref[:, sl]
                gh = gain_ref[:, sl]
                st_t = st_ref[c, h]
                fz = fz_ref[rs, sl]
                iv = iv_ref[rs, sl]
                qz = qz_ref[rs, sl]
                gz = gz_ref[rs, sl]
                q = _hg_chunk_fwd(fz, iv, qz, lbh, m_all_v, mask_ref, st_t)
                o = q["o"]
                r = lax.rsqrt(jnp.mean(o * o, axis=-1, keepdims=True) + RMS_EPS)
                on = o * r
                sg = _sigmoid(gz)
                gate = gz * sg
                dy = dy_ref[rs, sl].astype(F32)
                dgain_ref[:, sl] += jnp.sum(dy * on * gate, axis=0, keepdims=True)
                dgz = (dy * on * gh) * (sg * (1.0 + gz * (1.0 - sg)))
                don = dy * gh * gate
                do = r * (don - on * jnp.mean(don * on, axis=-1, keepdims=True))
                da = _dot_nt(do, iv)
                dv = _dot_tn(q["a"], do)
                dqe = _dot_nn(do, st_t)
                dst_new = _dot_tn(do, q["qe"])
                dsp = dst_scr[h]
                dw = _dot_nn(iv, dsp)
                dv = dv + _dot_nt(q["w"], dsp)
                d_eb_last = jnp.sum(dsp * st_t, axis=0, keepdims=True)
                dst_scr[h] = dsp * q["eb_last"] + dst_new
                dad = da * mask_ref[N_LEV]
                dq = _dot_nn(dad, q["kk"])
                dk = _dot_tn(dad, q["qq"])
                dargs_q = []
                dargs_k = []
                for j in range(N_LEV):
                    daj = da * mask_ref[j]
                    dqj = _dot_nn(daj, q["ks"][j])
                    dkj = _dot_tn(daj, q["qs"][j])
                    dq = dq + dqj * q["e"][j * C:(j + 1) * C]
                    dk = dk + dkj * q["e"][(N_LEV + j) * C:(N_LEV + j + 1) * C]
                    dargs_q.append(dqj * q["qs"][j])
                    dargs_k.append(dkj * q["ks"][j])
                dq = dq + dqe * q["eb"]
                darg_b = dqe * q["qe"] + jnp.where(last_row, d_eb_last * q["eb_last"], 0.0)
                dk = dk + dw * q["ew"]
                darg_w = dw * q["w"]
                dall = jnp.concatenate(dargs_q + dargs_k + [darg_b, darg_w], axis=0)
                hi = dall.astype(BF16)
                lo = (dall - hi.astype(F32)).astype(BF16)
                dl2 = lax.dot_general(m_all_v, jnp.concatenate([hi, lo], axis=1), (((0,), (0,)), ((), ())),
                                      preferred_element_type=F32)
                dlf = dl2[:, :HEAD_DIM] + dl2[:, HEAD_DIM:]
                df = dlf / q["f"] - dk
                sig = q["sig"]
                dlb_ref[:, sl] += jnp.sum(df * (1.0 - sig), axis=0, keepdims=True)
                dfz = df * (1.0 - lbh) * (sig * (1.0 - sig))
                sq = q["sq"]
                dqz = dq * (sq * (1.0 + qz * (1.0 - sq)))
                dz_ref[rs, h * HEAD_DIM:(h + 1) * HEAD_DIM] = dfz.astype(BF16)
                dz_ref[rs, HG_W + h * HEAD_DIM:HG_W + (h + 1) * HEAD_DIM] = dv.astype(BF16)
                dz_ref[rs, 2 * HG_W + h * HEAD_DIM:2 * HG_W + (h + 1) * HEAD_DIM] = dqz.astype(BF16)
                dz_ref[rs, 3 * HG_W + h * HEAD_DIM:3 * HG_W + (h + 1) * HEAD_DIM] = dgz.astype(BF16)
            return carry

        lax.fori_loop(0, cb, chunk, 0)
        if comm:
            @pl.when(pl.program_id(0) == nblk - 1)
            def _():
                comm["wait"](c_in, c_out, *sems)

    def col(c):
        return pl.BlockSpec((rows, HG_W), lambda i: (nblk - 1 - i, c))

    vec = pl.BlockSpec((1, HG_W), lambda i: (0, 0))
    args = _host_call(
        comm, 10, 3,
        in_specs=[col(3), col(4), col(5), col(6), vec, vec,
                  pl.BlockSpec(m_all.shape, lambda i: (0, 0)), pl.BlockSpec(masks.shape, lambda i: (0, 0, 0)),
                  pl.BlockSpec((cb, N_HEADS, HEAD_DIM, HEAD_DIM), lambda i: (nblk - 1 - i, 0, 0, 0)),
                  col(1)],
        out_specs=[pl.BlockSpec((rows, 4 * HG_W), lambda i: (nblk - 1 - i, 0)), vec, vec],
        out_shape=[jax.ShapeDtypeStruct((S, 4 * HG_W), BF16), jax.ShapeDtypeStruct((1, HG_W), F32),
                   jax.ShapeDtypeStruct((1, HG_W), F32)],
        scratch=[pltpu.VMEM((N_HEADS, HEAD_DIM, HEAD_DIM), F32)],
        aliases={})
    return pl.pallas_call(
        body, name=name, grid=(nblk,), compiler_params=_cparams(("arbitrary",)), **args,
    )(proj, proj, proj, proj, lb, gain, m_all, masks, states, d_cat, *(comm["ins"] if comm else []))


def _cross_fwd(cq, ckv, name):
    S = cq.shape[0]
    n_mem = ckv.shape[0]
    tq = _blk(S, (256, 128))
    scale = CROSS_DIM ** -0.5

    def body(q_ref, kv_ref, o_ref):
        for h in range(CROSS_HEADS):
            sl = slice(h * CROSS_DIM, (h + 1) * CROSS_DIM)
            k = kv_ref[:, sl]
            v = kv_ref[:, D_MODEL + h * CROSS_DIM:D_MODEL + (h + 1) * CROSS_DIM]
            s = _dot_nt(q_ref[:, sl], k) * scale
            m = jnp.max(s, axis=-1, keepdims=True)
            p = jnp.exp(s - m)
            p = p / jnp.sum(p, axis=-1, keepdims=True)
            o_ref[:, sl] = _dot_nn(p, v).astype(BF16)

    return pl.pallas_call(
        body, name=name, grid=(S // tq,),
        in_specs=[pl.BlockSpec((tq, D_MODEL), lambda i: (i, 0)), pl.BlockSpec((n_mem, 2 * D_MODEL), lambda i: (0, 0))],
        out_specs=pl.BlockSpec((tq, D_MODEL), lambda i: (i, 0)),
        out_shape=jax.ShapeDtypeStruct((S, D_MODEL), BF16),
        compiler_params=_cparams(("parallel",)),
    )(cq, ckv)


def _cross_bwd(cq, ckv, d_o, name):
    S = cq.shape[0]
    n_mem = ckv.shape[0]
    tq = _blk(S, (256, 128))
    scale = CROSS_DIM ** -0.5

    def body(q_ref, kv_ref, do_ref, dq_ref, dkv_ref):
        @pl.when(pl.program_id(0) == 0)
        def _():
            dkv_ref[...] = jnp.zeros_like(dkv_ref)

        for h in range(CROSS_HEADS):
            sl = slice(h * CROSS_DIM, (h + 1) * CROSS_DIM)
            slv = slice(D_MODEL + h * CROSS_DIM, D_MODEL + (h + 1) * CROSS_DIM)
            q = q_ref[:, sl]
            k = kv_ref[:, sl]
            v = kv_ref[:, slv]
            do = do_ref[:, sl]
            s = _dot_nt(q, k) * scale
            m = jnp.max(s, axis=-1, keepdims=True)
            p = jnp.exp(s - m)
            p = p / jnp.sum(p, axis=-1, keepdims=True)
            dp = _dot_nt(do, v)
            ds = p * (dp - jnp.sum(dp * p, axis=-1, keepdims=True)) * scale
            dq_ref[:, sl] = _dot_nn(ds, k).astype(BF16)
            dkv_ref[:, sl] += _dot_tn(ds, q)
            dkv_ref[:, slv] += _dot_tn(p, do)

    row = pl.BlockSpec((tq, D_MODEL), lambda i: (i, 0))
    kv = pl.BlockSpec((n_mem, 2 * D_MODEL), lambda i: (0, 0))
    return pl.pallas_call(
        body, name=name, grid=(S // tq,),
        in_specs=[row, kv, row], out_specs=[row, kv],
        out_shape=[jax.ShapeDtypeStruct((S, D_MODEL), BF16), jax.ShapeDtypeStruct((n_mem, 2 * D_MODEL), F32)],
        compiler_params=_cparams(("arbitrary",)),
    )(cq, ckv, d_o)


FF_BLOCK = 512


def _mm_gate_up(hf, w_gu, name, comm=None):
    S, K = hf.shape
    bm = _blk(S, (1024, 512, 256, 128))
    ni = S // bm
    nj = D_FF // FF_BLOCK

    def body(*refs):
        own_in, c_in, own_out, c_out, _, sems = _host_refs(refs, 3, 3, 0, comm)
        a_ref, bg_ref, bu_ref = own_in
        g_ref, u_ref, act_ref = own_out
        if comm:
            @pl.when((pl.program_id(0) == 0) & (pl.program_id(1) == 0))
            def _():
                comm["start"](c_in, c_out, *sems)

        a = a_ref[...]
        g = _dot_nn(a, bg_ref[...])
        u = _dot_nn(a, bu_ref[...])
        g_ref[...] = g.astype(BF16)
        u_ref[...] = u.astype(BF16)
        act_ref[...] = ((g * _sigmoid(g)) * u).astype(BF16)
        if comm:
            @pl.when((pl.program_id(0) == ni - 1) & (pl.program_id(1) == nj - 1))
            def _():
                comm["wait"](c_in, c_out, *sems)

    out = pl.BlockSpec((bm, FF_BLOCK), lambda i, j: (i, j))
    sds = jax.ShapeDtypeStruct((S, D_FF), BF16)
    args = _host_call(
        comm, 3, 3,
        in_specs=[pl.BlockSpec((bm, K), lambda i, j: (i, 0)), pl.BlockSpec((K, FF_BLOCK), lambda i, j: (0, j)),
                  pl.BlockSpec((K, FF_BLOCK), lambda i, j: (0, j + nj))],
        out_specs=[out, out, out], out_shape=[sds, sds, sds], scratch=[], aliases={})
    return pl.pallas_call(
        body, name=name, grid=(ni, nj), compiler_params=_cparams(("arbitrary", "arbitrary")), **args,
    )(hf, w_gu, w_gu, *(comm["ins"] if comm else []))


def _mm_down_x(dy, w_down, g, u, name):
    S, K = dy.shape
    bm = _blk(S, (1024, 512, 256, 128))
    nj = D_FF // FF_BLOCK

    def body(a_ref, b_ref, g_ref, u_ref, dg_ref, du_ref):
        da = _dot_nt(a_ref[...], b_ref[...])
        gv = g_ref[...].astype(F32)
        sg = _sigmoid(gv)
        dg_ref[...] = (da * u_ref[...].astype(F32) * (sg * (1.0 + gv * (1.0 - sg)))).astype(BF16)
        du_ref[...] = (da * (gv * sg)).astype(BF16)

    tile = pl.BlockSpec((bm, FF_BLOCK), lambda i, j: (i, j))
    sds = jax.ShapeDtypeStruct((S, D_FF), BF16)
    return pl.pallas_call(
        body, name=name, grid=(S // bm, nj),
        in_specs=[pl.BlockSpec((bm, K), lambda i, j: (i, 0)), pl.BlockSpec((FF_BLOCK, K), lambda i, j: (j, 0)),
                  tile, tile],
        out_specs=[tile, tile], out_shape=[sds, sds],
        compiler_params=_cparams(("parallel", "arbitrary")),
    )(dy, w_down, g, u)


def _loss_head(y, target, name):
    S, D = y.shape
    tm = _blk(S, (256, 128, 64, 8))

    def body(y_ref, t_ref, dy_ref, l_ref):
        diff = y_ref[...] - t_ref[...]
        dy_ref[...] = diff * (1.0 / D)
        sq = (diff * diff) * (0.5 / D)
        part = jnp.sum(sq.reshape(tm // 8, 8, D), axis=0)

        @pl.when(pl.program_id(0) == 0)
        def _():
            l_ref[...] = part

        @pl.when(pl.program_id(0) > 0)
        def _():
            l_ref[...] += part

    row = pl.BlockSpec((tm, D), lambda i: (i, 0))
    return pl.pallas_call(
        body, name=name, grid=(S // tm,), in_specs=[row, row],
        out_specs=[row, pl.BlockSpec((8, D), lambda i: (0, 0))],
        out_shape=[jax.ShapeDtypeStruct((S, D), F32), jax.ShapeDtypeStruct((8, D), F32)],
        compiler_params=_cparams(("arbitrary",)),
    )(y, target)


def _layer_fwd(x0, mem, g, w, lb, hg_gain, bias, tag, comm=None):
    S = x0.shape[0]
    h0 = _rms_fwd(x0, g[0], f"rms0_{tag}")
    proj = _mm(h0, w["w_in"], "nn", F32, f"mm_in_{tag}")
    state = None
    cat = lax.empty((S, D_MODEL), BF16)
    for bi, d in enumerate(DILATIONS):
        last = bi == len(DILATIONS) - 1
        out = _attn_fwd(proj, bias[bi], state, d, last, cat if last else None, f"attn_fwd{d}_{tag}")
        if last:
            cat, lse = out
        else:
            state = tuple(out)
    comm = comm or {}
    cat, states, *carried_a = _hgrn_fwd(proj, lb, hg_gain, cat, f"hgrn_fwd_{tag}", comm.get("hgrn"))
    mix = _mm(cat, w["w_out"], "nn", F32, f"mm_out_{tag}")
    x1, hc = _norm_residual(x0, mix, g[1], g[2], f"res1_{tag}")
    cq = _mm(hc, w["w_cq"], "nn", BF16, f"mm_cq_{tag}")
    mn = _rms_fwd(mem, g[3], f"rms3_{tag}")
    ckv = _mm(mn, w["w_ckv"], "nn", BF16, f"mm_ckv_{tag}")
    cop = _cross_fwd(cq, ckv, f"cross_fwd_{tag}")
    co = _mm(cop, w["w_co"], "nn", F32, f"mm_co_{tag}")
    x2, hf = _norm_residual(x1, co, g[4], g[5], f"res2_{tag}")
    gate, up, act, *carried_b = _mm_gate_up(hf, w["w_gate_up"], f"mm_gu_{tag}", comm.get("gate_up"))
    carried = dict(hgrn=carried_a, gate_up=carried_b)
    y = _mm(act, w["w_down"], "nn", F32, f"mm_down_{tag}")
    x3 = _norm_residual(x2, y, g[6], None, f"res3_{tag}")
    saved = dict(x0=x0, h0=h0, proj=proj, cat=cat, lse=lse, states=states, mix=mix, x1=x1, hc=hc, cq=cq, mn=mn,
                 ckv=ckv, cop=cop, co=co, x2=x2, hf=hf, gate=gate, up=up, act=act, y=y)
    return x3, saved, carried


def _layer_bwd(dx3, sv, mem, g, w, lb, hg_gain, bias, tag, comm_fn=None):
    dg = [None] * N_NORMS
    gw = {}
    dy, dg[6] = _rms_bwd(sv["y"], g[6], dx3, None, BF16, f"rmsb6_{tag}")
    dgu = list(_mm_down_x(dy, w["w_down"], sv["gate"], sv["up"], f"mmb_down_x_{tag}"))
    gw["w_down"] = _mm(sv["act"], dy, "tn", BF16, f"mmb_down_w_{tag}")
    gw["w_gate_up"] = _mm(sv["hf"], dgu, "tn", BF16, f"mmb_gu_w_{tag}")
    d_hf = _mm(dgu, w["w_gate_up"], "nt", F32, f"mmb_gu_x_{tag}")
    dx2, dg[5] = _rms_bwd(sv["x2"], g[5], d_hf, dx3, F32, f"rmsb5_{tag}")
    d_co, dg[4] = _rms_bwd(sv["co"], g[4], dx2, None, BF16, f"rmsb4_{tag}")
    d_cop = _mm(d_co, w["w_co"], "nt", BF16, f"mmb_co_x_{tag}")
    gw["w_co"] = _mm(sv["cop"], d_co, "tn", BF16, f"mmb_co_w_{tag}")
    d_cq, d_ckv = _cross_bwd(sv["cq"], sv["ckv"], d_cop, f"cross_bwd_{tag}")
    gw["w_cq"] = _mm(sv["hc"], d_cq, "tn", BF16, f"mmb_cq_w_{tag}")
    d_hc = _mm(d_cq, w["w_cq"], "nt", F32, f"mmb_cq_x_{tag}")
    gw["w_ckv"] = _mm(sv["mn"], d_ckv, "tn", BF16, f"mmb_ckv_w_{tag}")
    d_mn = _mm(d_ckv, w["w_ckv"], "nt", F32, f"mmb_ckv_x_{tag}")
    _, dg[3] = _rms_bwd(mem, g[3], d_mn, None, BF16, f"rmsb3_{tag}")
    dx1, dg[2] = _rms_bwd(sv["x1"], g[2], d_hc, dx2, F32, f"rmsb2_{tag}")
    d_mix, dg[1] = _rms_bwd(sv["mix"], g[1], dx1, None, BF16, f"rmsb1_{tag}")
    d_cat = _mm(d_mix, w["w_out"], "nt", BF16, f"mmb_out_x_{tag}")
    gw["w_out"] = _mm(sv["cat"], d_mix, "tn", BF16, f"mmb_out_w_{tag}")
    acc = None
    dbias = []
    for bi, d in enumerate(DILATIONS):
        last = bi == len(DILATIONS) - 1
        dq, dk, dv, db = _attn_bwd(sv["proj"], bias[bi], sv["cat"], d_cat, sv["lse"], acc, d, last,
                                   f"attn_bwd{d}_{tag}")
        acc = (dq, dk, dv)
        dbias.append(db)
    comm = comm_fn(dict(gw)) if comm_fn else None
    d_hz, dlb, dgain, *carried = _hgrn_bwd(sv["proj"], lb, hg_gain, sv["states"], d_cat, f"hgrn_bwd_{tag}", comm)
    parts = [acc[0], acc[1], acc[2], d_hz]
    gw["w_in"] = _mm(sv["h0"], parts, "tn", BF16, f"mmb_in_w_{tag}")
    d_h0 = _mm(parts, w["w_in"], "nt", F32, f"mmb_in_x_{tag}")
    dx0, dg[0] = _rms_bwd(sv["x0"], g[0], d_h0, dx1, F32, f"rmsb0_{tag}")
    return dx0, gw, jnp.stack(dg), dlb, dgain, jnp.stack(dbias), carried


def _lb_all(lb_logits):
    p = jax.nn.softmax(lb_logits.astype(F32), axis=0)
    return jnp.cumsum(p, axis=0) - p


def _local_step(x, mem, target, rel_bias, lb_logits, norm_gains, hg_norm, weights_of_layer, hooks=None):
    hooks = hooks or {}
    L = lb_logits.shape[0]
    bias, bias_vjp = jax.vjp(_bias_tables, rel_bias)
    lb_all, lb_vjp = jax.vjp(_lb_all, lb_logits)
    gains = norm_gains.reshape(L, N_NORMS, 1, D_MODEL)
    saved = []
    h = x
    ws = []
    for l in range(L):
        w = weights_of_layer(l)
        ws.append(w)
        comm = hooks["fwd_comm"](l) if "fwd_comm" in hooks else None
        h, sv, carried = _layer_fwd(h, mem, gains[l], w, lb_all[l:l + 1], hg_norm[l:l + 1], bias, f"l{l}", comm)
        if comm:
            hooks["fwd_done"](l, carried)
        saved.append(sv)
    dy, lparts = _loss_head(h, target, "loss_head")
    loss = jnp.sum(lparts)
    d_gains, d_lb, d_hg, gws = [None] * L, [None] * L, [None] * L, [None] * L
    d_bias = jnp.zeros_like(bias)
    dh = dy
    for l in reversed(range(L)):
        comm_fn = functools.partial(hooks["bwd_comm"], l) if "bwd_comm" in hooks else None
        dh, gw, dgl, dlbl, dhgl, dbl, carried = _layer_bwd(dh, saved[l], mem, gains[l], ws[l], lb_all[l:l + 1],
                                                           hg_norm[l:l + 1], bias, f"l{l}", comm_fn)
        if comm_fn:
            hooks["bwd_done"](l, carried)
        d_gains[l], d_lb[l], d_hg[l], gws[l] = dgl.reshape(N_NORMS, D_MODEL), dlbl[0], dhgl[0], gw
        d_bias = d_bias + dbl
        if "layer_grads" in hooks:
            gws[l] = hooks["layer_grads"](l, gw)
    (d_rel_bias,) = bias_vjp(d_bias)
    (d_lb_logits,) = lb_vjp(jnp.stack(d_lb))
    return loss, dh, gws, d_rel_bias, d_lb_logits, jnp.stack(d_gains), jnp.stack(d_hg)


def kernel(x, mem, rel_bias, lb_logits, norm_gains, w_in, hg_norm, w_out, w_cq, w_ckv, w_co, w_gate_up, w_down, loss_target, m_rel_bias, m_lb_logits, m_norm_gains, m_w_in, m_hg_norm, m_w_out, m_w_cq, m_w_ckv, m_w_co, m_w_gate_up, m_w_down, v_rel_bias, v_lb_logits, v_norm_gains, v_w_in, v_hg_norm, v_w_out, v_w_cq, v_w_ckv, v_w_co, v_w_gate_up, v_w_down):
    return _train_step(
        x, mem, loss_target,
        dict(rel_bias=rel_bias, lb_logits=lb_logits, norm_gains=norm_gains, hg_norm=hg_norm, w_in=w_in, w_out=w_out,
             w_cq=w_cq, w_ckv=w_ckv, w_co=w_co, w_gate_up=w_gate_up, w_down=w_down),
        dict(rel_bias=m_rel_bias, lb_logits=m_lb_logits, norm_gains=m_norm_gains, hg_norm=m_hg_norm, w_in=m_w_in,
             w_out=m_w_out, w_cq=m_w_cq, w_ckv=m_w_ckv, w_co=m_w_co, w_gate_up=m_w_gate_up, w_down=m_w_down),
        dict(rel_bias=v_rel_bias, lb_logits=v_lb_logits, norm_gains=v_norm_gains, hg_norm=v_hg_norm, w_in=v_w_in,
             w_out=v_w_out, w_cq=v_w_cq, w_ckv=v_w_ckv, w_co=v_w_co, w_gate_up=v_w_gate_up, w_down=v_w_down))


BIG = ("w_in", "w_out", "w_cq", "w_ckv", "w_co", "w_gate_up", "w_down")
SHARD_AXIS = dict(w_in=1, w_out=0, w_cq=0, w_ckv=1, w_co=0, w_gate_up=1, w_down=0)
NB = len(BIG)
AG_WITH_HGRN = ("w_in", "w_out", "w_cq", "w_ckv", "w_co")
AG_WITH_FFN = ("w_gate_up", "w_down")
OUT_ORDER = ("rel_bias", "lb_logits", "norm_gains", "w_in", "hg_norm", "w_out", "w_cq", "w_ckv", "w_co",
             "w_gate_up", "w_down")
ANY = pl.BlockSpec(memory_space=pl.ANY)


def _place():
    x, y, c = lax.axis_index("x"), lax.axis_index("y"), lax.axis_index("c")
    chips = [(1 - x, y), (x, 1 - y), (1 - x, 1 - y)]
    return x, y, c, chips


def _remote(src, dst, send_sem, recv_sem, to):
    return pltpu.make_async_remote_copy(src_ref=src, dst_ref=dst, send_sem=send_sem, recv_sem=recv_sem,
                                        device_id=to, device_id_type=MESH_ID)


def _half_region(ref, axis, chip, half, lead=()):
    R, C = ref.shape[-2:]
    if axis == 0:
        rs = R // N_CHIPS
        return ref.at[(*lead, pl.ds(chip * rs + half * (rs // 2), rs // 2), slice(None))]
    cs = C // N_CHIPS
    return ref.at[(*lead, pl.ds(half * (R // 2), R // 2), pl.ds(chip * cs, cs))]


def _cast_place(n, w, l, where, name):
    _, rs, cs = w.shape
    tr = _blk(rs, (256, 128, 64, 32, 16))
    nt = rs // tr
    if SHARD_AXIS[n] == 0:
        full = (rs * N_CHIPS, cs)
        o_spec = pl.BlockSpec((tr, cs), lambda i, wh: (wh[0] * nt + i, 0))
    else:
        full = (rs, cs * N_CHIPS)
        o_spec = pl.BlockSpec((tr, cs), lambda i, wh: (i, wh[0]))

    def body(wh, w_ref, o_ref):
        o_ref[...] = w_ref[...].astype(BF16)

    return pl.pallas_call(
        body, name=name,
        grid_spec=pltpu.PrefetchScalarGridSpec(
            num_scalar_prefetch=1, grid=(nt,),
            in_specs=[pl.BlockSpec((None, tr, cs), lambda i, wh: (l, i, 0))], out_specs=o_spec),
        out_shape=jax.ShapeDtypeStruct(full, BF16), compiler_params=_cparams(("parallel",)),
    )(where, w)


def _allgather_weights(placed, name):
    names = list(BIG)
    nb = NB

    def body(*refs):
        outs = refs[NB:2 * NB]
        send, recv = refs[2 * NB:]
        x, y, c, chips = _place()
        me = 2 * x + y
        sib = (x, y, 1 - c)
        sends = []
        for j, chip in enumerate(chips):
            for wi, n in enumerate(names):
                reg = _half_region(outs[wi], SHARD_AXIS[n], me, c)
                cp = _remote(reg, reg, send.at[j * nb + wi], recv.at[j * nb + wi], (*chip, c))
                cp.start()
                sends.append(cp)
        for j, chip in enumerate(chips):
            them = 2 * chip[0] + chip[1]
            for wi, n in enumerate(names):
                reg = _half_region(outs[wi], SHARD_AXIS[n], them, c)
                _remote(reg, reg, send.at[j * nb + wi], recv.at[j * nb + wi], (*chip, c)).wait_recv()
                cp = _remote(reg, reg, send.at[(3 + j) * nb + wi], recv.at[(3 + j) * nb + wi], sib)
                cp.start()
                sends.append(cp)
        for j, chip in enumerate(chips):
            them = 2 * chip[0] + chip[1]
            for wi, n in enumerate(names):
                reg = _half_region(outs[wi], SHARD_AXIS[n], them, 1 - c)
                _remote(reg, reg, send.at[(3 + j) * nb + wi], recv.at[(3 + j) * nb + wi], sib).wait_recv()
        for cp in sends:
            cp.wait_send()

    outs = pl.pallas_call(
        body, name=name, in_specs=[ANY] * NB, out_specs=[ANY] * NB,
        out_shape=[jax.ShapeDtypeStruct(placed[n].shape, BF16) for n in names],
        scratch_shapes=[pltpu.SemaphoreType.DMA((6 * NB,)), pltpu.SemaphoreType.DMA((6 * NB,))],
        input_output_aliases={wi: wi for wi in range(NB)},
    )(*[placed[n] for n in names])
    return dict(zip(names, outs))


def _comm_spec(ins, out_shapes, aliases, nsem, start, wait):
    return dict(ins=list(ins), out_shapes=list(out_shapes), aliases=dict(aliases), nsem=nsem, start=start, wait=wait)


def _run_comm(spec, name):
    ni, no = len(spec["ins"]), len(spec["out_shapes"])

    def body(*refs):
        ins, outs = refs[:ni], refs[ni:ni + no]
        send, recv = refs[ni + no:]
        spec["start"](ins, outs, send, recv)
        spec["wait"](ins, outs, send, recv)

    return pl.pallas_call(
        body, name=name, in_specs=[ANY] * ni, out_specs=[ANY] * no, out_shape=spec["out_shapes"],
        scratch_shapes=[pltpu.SemaphoreType.DMA((spec["nsem"],)), pltpu.SemaphoreType.DMA((spec["nsem"],))],
        input_output_aliases=spec["aliases"],
    )(*spec["ins"])


def _ag_ici_spec(placed):
    names = list(placed)
    nb = len(names)

    def start(ins, outs, send, recv):
        x, y, c, chips = _place()
        me = 2 * x + y
        for j, chip in enumerate(chips):
            for wi, n in enumerate(names):
                reg = _half_region(outs[wi], SHARD_AXIS[n], me, c)
                _remote(reg, reg, send.at[j * nb + wi], recv.at[j * nb + wi], (*chip, c)).start()

    def wait(ins, outs, send, recv):
        x, y, c, chips = _place()
        me = 2 * x + y
        for j, chip in enumerate(chips):
            them = 2 * chip[0] + chip[1]
            for wi, n in enumerate(names):
                reg = _half_region(outs[wi], SHARD_AXIS[n], them, c)
                _remote(reg, reg, send.at[j * nb + wi], recv.at[j * nb + wi], (*chip, c)).wait_recv()
        for j, chip in enumerate(chips):
            for wi, n in enumerate(names):
                reg = _half_region(outs[wi], SHARD_AXIS[n], me, c)
                _remote(reg, reg, send.at[j * nb + wi], recv.at[j * nb + wi], (*chip, c)).wait_send()

    return _comm_spec([placed[n] for n in names], [jax.ShapeDtypeStruct(placed[n].shape, BF16) for n in names],
                      {wi: wi for wi in range(nb)}, 3 * nb, start, wait)


def _ag_sibling_spec(arrays):
    names = list(arrays)
    nb = len(names)

    def start(ins, outs, send, recv):
        x, y, c, chips = _place()
        for j, chip in enumerate(chips):
            them = 2 * chip[0] + chip[1]
            for wi, n in enumerate(names):
                reg = _half_region(outs[wi], SHARD_AXIS[n], them, c)
                _remote(reg, reg, send.at[j * nb + wi], recv.at[j * nb + wi], (x, y, 1 - c)).start()

    def wait(ins, outs, send, recv):
        x, y, c, chips = _place()
        for j, chip in enumerate(chips):
            them = 2 * chip[0] + chip[1]
            for wi, n in enumerate(names):
                reg = _half_region(outs[wi], SHARD_AXIS[n], them, 1 - c)
                _remote(reg, reg, send.at[j * nb + wi], recv.at[j * nb + wi], (x, y, 1 - c)).wait_recv()
        for j, chip in enumerate(chips):
            them = 2 * chip[0] + chip[1]
            for wi, n in enumerate(names):
                reg = _half_region(outs[wi], SHARD_AXIS[n], them, c)
                _remote(reg, reg, send.at[j * nb + wi], recv.at[j * nb + wi], (x, y, 1 - c)).wait_send()

    return _comm_spec([arrays[n] for n in names], [jax.ShapeDtypeStruct(arrays[n].shape, BF16) for n in names],
                      {wi: wi for wi in range(nb)}, 3 * nb, start, wait)


def _allgather_gains(g_shard, name):
    A, Cs = g_shard.shape

    def body(in_ref, out_ref, send, recv, local):
        x, y, c, chips = _place()
        me = 2 * x + y
        mine = pltpu.make_async_copy(in_ref, out_ref.at[me], local)
        mine.start()
        cps = [_remote(in_ref, out_ref.at[me], send.at[j], recv.at[j], (*chip, c)) for j, chip in enumerate(chips)]
        for cp in cps:
            cp.start()
        for j, chip in enumerate(chips):
            them = 2 * chip[0] + chip[1]
            _remote(in_ref, out_ref.at[them], send.at[j], recv.at[j], (*chip, c)).wait_recv()
        for cp in cps:
            cp.wait_send()
        mine.wait()

    return pl.pallas_call(
        body, name=name, in_specs=[ANY], out_specs=ANY,
        out_shape=jax.ShapeDtypeStruct((N_CHIPS, A, Cs), F32),
        scratch_shapes=[pltpu.SemaphoreType.DMA((3,)), pltpu.SemaphoreType.DMA((3,)), pltpu.SemaphoreType.DMA],
    )(g_shard)


def _half_shape(n, shape):
    R, C = shape
    if SHARD_AXIS[n] == 0:
        return (N_CHIPS, R // N_CHIPS // 2, C)
    return (R // 2, C)


def _as_halves(n, g):
    R, C = g.shape
    if SHARD_AXIS[n] == 0:
        return g.reshape(N_CHIPS, R // N_CHIPS, C)
    return g


def _my_half(n, ref, half):
    if SHARD_AXIS[n] == 0:
        hs = ref.shape[1] // 2
        return ref.at[:, pl.ds(half * hs, hs), :]
    hs = ref.shape[0] // 2
    return ref.at[pl.ds(half * hs, hs), :]


def _swap_sibling_halves(gw, name):
    names = list(gw)
    nb = len(names)

    def body(*refs):
        ins = refs[:nb]
        outs = refs[nb:2 * nb]
        send, recv = refs[2 * nb:]
        x, y, c, _ = _place()
        sib = (x, y, 1 - c)
        cps = []
        for wi, n in enumerate(names):
            cp = _remote(_my_half(n, ins[wi], 1 - c), outs[wi], send.at[wi], recv.at[wi], sib)
            cp.start()
            cps.append(cp)
        for cp in cps:
            cp.wait()

    shapes = []
    for n in names:
        g = gw[n]
        shapes.append((g.shape[0], g.shape[1] // 2, g.shape[2]) if g.ndim == 3 else (g.shape[0] // 2, g.shape[1]))
    outs = pl.pallas_call(
        body, name=name, in_specs=[ANY] * nb, out_specs=[ANY] * nb,
        out_shape=[jax.ShapeDtypeStruct(s, BF16) for s in shapes],
        scratch_shapes=[pltpu.SemaphoreType.DMA((nb,)), pltpu.SemaphoreType.DMA((nb,))],
    )(*[gw[n] for n in names])
    return dict(zip(names, outs))


def _pair_sum(n, g, other, c_arr, name):
    if g.ndim == 3:
        nc, rs, C = g.shape
        hs = rs // 2
        tr = _blk(hs, (256, 128, 64, 32, 16))
        nt = hs // tr
        grid = (nc, nt)
        g_spec = pl.BlockSpec((1, tr, C), lambda k, i, c_ref: (k, c_ref[0] * nt + i, 0))
        o_spec = pl.BlockSpec((1, tr, C), lambda k, i, c_ref: (k, i, 0))
    else:
        R, C = g.shape
        hs = R // 2
        cs = C // N_CHIPS
        tr = _blk(hs, (256, 128, 64, 32, 16))
        nt = hs // tr
        grid = (nt, N_CHIPS)
        g_spec = pl.BlockSpec((tr, cs), lambda i, k, c_ref: (c_ref[0] * nt + i, k))
        o_spec = pl.BlockSpec((tr, cs), lambda i, k, c_ref: (i, k))

    def body(c_ref, g_ref, o_ref, out_ref):
        out_ref[...] = (g_ref[...].astype(F32) + o_ref[...].astype(F32)).astype(BF16)

    return pl.pallas_call(
        body, name=name,
        grid_spec=pltpu.PrefetchScalarGridSpec(num_scalar_prefetch=1, grid=grid, in_specs=[g_spec, o_spec],
                                               out_specs=o_spec),
        out_shape=jax.ShapeDtypeStruct(other.shape, BF16),
        compiler_params=_cparams(("parallel", "parallel")),
    )(c_arr, g, other)


def _scatter_spec(pairs):
    names = list(pairs)
    nb = len(names)

    def piece(n, ref, chip):
        if SHARD_AXIS[n] == 0:
            return ref.at[chip]
        cs = ref.shape[1] // N_CHIPS
        return ref.at[:, pl.ds(chip * cs, cs)]

    def start(ins, outs, send, recv):
        x, y, c, chips = _place()
        for j, chip in enumerate(chips):
            them = 2 * chip[0] + chip[1]
            for wi, n in enumerate(names):
                _remote(piece(n, ins[wi], them), outs[wi].at[j], send.at[j * nb + wi], recv.at[j * nb + wi],
                        (*chip, c)).start()

    def wait(ins, outs, send, recv):
        x, y, c, chips = _place()
        me = 2 * x + y
        for j, chip in enumerate(chips):
            for wi, n in enumerate(names):
                _remote(piece(n, ins[wi], me), outs[wi].at[j], send.at[j * nb + wi], recv.at[j * nb + wi],
                        (*chip, c)).wait_recv()
        for j, chip in enumerate(chips):
            them = 2 * chip[0] + chip[1]
            for wi, n in enumerate(names):
                _remote(piece(n, ins[wi], them), outs[wi].at[j], send.at[j * nb + wi], recv.at[j * nb + wi],
                        (*chip, c)).wait_send()

    shapes = []
    for n in names:
        p = pairs[n]
        shapes.append((3,) + (p.shape[1:] if p.ndim == 3 else (p.shape[0], p.shape[1] // N_CHIPS)))
    return _comm_spec([pairs[n] for n in names], [jax.ShapeDtypeStruct(s, BF16) for s in shapes], {}, 3 * nb,
                      start, wait)


def _sum_final(n, pair, q, stack, l, where, name):
    hs, cc = q.shape[1:]
    tr = _blk(hs, (256, 128, 64, 32, 16))
    nt = hs // tr
    if SHARD_AXIS[n] == 0:
        p_spec = pl.BlockSpec((None, tr, cc), lambda i, wh: (wh[0], i, 0))
    else:
        p_spec = pl.BlockSpec((tr, cc), lambda i, wh: (i, wh[0]))

    def body(wh, p_ref, q_ref, stack_any, o_ref):
        o_ref[...] = ((p_ref[...].astype(F32) + q_ref[0].astype(F32)) + q_ref[1].astype(F32)) + q_ref[2].astype(F32)

    return pl.pallas_call(
        body, name=name,
        grid_spec=pltpu.PrefetchScalarGridSpec(
            num_scalar_prefetch=1, grid=(nt,),
            in_specs=[p_spec, pl.BlockSpec((3, tr, cc), lambda i, wh: (0, i, 0)), ANY],
            out_specs=pl.BlockSpec((None, tr, cc), lambda i, wh: (l, wh[1] * nt + i, 0))),
        out_shape=jax.ShapeDtypeStruct(stack.shape, F32),
        input_output_aliases={3: 0},
        compiler_params=_cparams(("parallel",)),
    )(where, pair, q, stack)


def _sum_slots(q, name):
    K, A, B = q.shape
    tr = _blk(A, (256, 128, 64, 32, 16, 8))

    def body(q_ref, o_ref):
        s = q_ref[0].astype(F32)
        for k in range(1, K):
            s = s + q_ref[k].astype(F32)
        o_ref[...] = s

    return pl.pallas_call(
        body, name=name, grid=(A // tr,),
        in_specs=[pl.BlockSpec((K, tr, B), lambda i: (0, i, 0))], out_specs=pl.BlockSpec((tr, B), lambda i: (i, 0)),
        out_shape=jax.ShapeDtypeStruct((A, B), F32), compiler_params=_cparams(("parallel",)),
    )(q)


def _finish_grads(stacks, small, name):
    names = list(BIG)
    A = small.shape[0]
    nh = NB

    def body(*refs):
        small_ref = refs[nh]
        outs = refs[nh + 1:nh + 1 + NB]
        slots = refs[nh + 1 + NB]
        send, recv, local = refs[nh + 2 + NB:]
        x, y, c, _ = _place()
        sib = (x, y, 1 - c)
        me = 4 * x + 2 * y + c
        locals_, sends = [], []
        for wi, n in enumerate(names):
            hr = outs[wi].shape[1] // 2
            reg = outs[wi].at[:, pl.ds(c * hr, hr), :]
            cp = _remote(reg, reg, send.at[wi], recv.at[wi], sib)
            cp.start()
            sends.append(cp)
        cp = pltpu.make_async_copy(small_ref, slots.at[me], local)
        cp.start()
        locals_.append(cp)
        peers = []
        for dx in range(2):
            for dy in range(2):
                for dc in range(2):
                    if dx or dy or dc:
                        peers.append((dx, dy, dc))
        for j, (dx, dy, dc) in enumerate(peers):
            to = (jnp.bitwise_xor(x, dx), jnp.bitwise_xor(y, dy), jnp.bitwise_xor(c, dc))
            cp = _remote(small_ref, slots.at[me], send.at[nh + j], recv.at[nh + j], to)
            cp.start()
            sends.append(cp)
        for wi, n in enumerate(names):
            hr = outs[wi].shape[1] // 2
            reg = outs[wi].at[:, pl.ds((1 - c) * hr, hr), :]
            _remote(reg, reg, send.at[wi], recv.at[wi], sib).wait_recv()
        for j, (dx, dy, dc) in enumerate(peers):
            frm = 4 * jnp.bitwise_xor(x, dx) + 2 * jnp.bitwise_xor(y, dy) + jnp.bitwise_xor(c, dc)
            _remote(small_ref, slots.at[frm], send.at[nh + j], recv.at[nh + j], sib).wait_recv()
        for cp in sends:
            cp.wait_send()
        for cp in locals_:
            cp.wait()

    res = pl.pallas_call(
        body, name=name, in_specs=[ANY] * (nh + 1), out_specs=[ANY] * (NB + 1),
        out_shape=[jax.ShapeDtypeStruct(stacks[n].shape, F32) for n in names]
        + [jax.ShapeDtypeStruct((8, A, LANES), F32)],
        scratch_shapes=[pltpu.SemaphoreType.DMA((nh + 7,)), pltpu.SemaphoreType.DMA((nh + 7,)),
                        pltpu.SemaphoreType.DMA],
        input_output_aliases={wi: wi for wi in range(NB)},
    )(*[stacks[n] for n in names], small)
    return dict(zip(names, res[:NB])), res[NB]


def _adamw(w, g, m, v, name):
    shape = w.shape
    if w.ndim == 2:
        w, g, m, v = (t.reshape((1,) + shape) for t in (w, g, m, v))
    L, A, B = w.shape
    tr = _blk(A, (128, 64, 32, 16, 8))
    c1 = 1.0 - ADAM_B1 ** ADAM_STEP
    c2 = 1.0 - ADAM_B2 ** ADAM_STEP

    def body(w_ref, g_ref, m_ref, v_ref, d_ref, nm_ref, nv_ref):
        gv = g_ref[...]
        nm = ADAM_B1 * m_ref[...] + (1.0 - ADAM_B1) * gv
        nv = ADAM_B2 * v_ref[...] + (1.0 - ADAM_B2) * (gv * gv)
        m_hat = nm / c1
        v_hat = nv / c2
        d_ref[...] = -ADAM_LR * (m_hat / (jnp.sqrt(v_hat) + ADAM_EPS) + ADAM_WD * w_ref[...])
        nm_ref[...] = nm
        nv_ref[...] = nv

    spec = pl.BlockSpec((1, tr, B), lambda l, i: (l, i, 0))
    sds = jax.ShapeDtypeStruct((L, A, B), F32)
    outs = pl.pallas_call(
        body, name=name, grid=(L, A // tr), in_specs=[spec] * 4, out_specs=[spec] * 3, out_shape=[sds] * 3,
        compiler_params=_cparams(("parallel", "parallel")),
    )(w, g, m, v)
    return tuple(o.reshape(shape) for o in outs)


SMALL_ROWS = 520


def _train_step(x, mem, target, w, m, v):
    L = w["lb_logits"].shape[0]
    cx, cy, cc = lax.axis_index("x"), lax.axis_index("y"), lax.axis_index("c")
    me = 2 * cx + cy
    c_arr = jnp.reshape(cc, (1,)).astype(jnp.int32)
    where = jnp.stack([me, cc]).astype(jnp.int32)

    placed = [{n: _cast_place(n, w[n], l, where, f"cast_{n}_l{l}") for n in BIG} for l in range(L)]
    full = [None] * L
    full[0] = _allgather_weights(placed[0], "allgather_l0")
    gs = _allgather_gains(w["norm_gains"].reshape(L * N_NORMS, -1), "allgather_gains")
    gains = jnp.transpose(gs, (1, 0, 2)).reshape(L, N_NORMS, D_MODEL)

    def fwd_comm(l):
        if l + 1 == L:
            return None
        return dict(hgrn=_ag_ici_spec({n: placed[l + 1][n] for n in AG_WITH_HGRN}),
                    gate_up=_ag_ici_spec({n: placed[l + 1][n] for n in AG_WITH_FFN}))

    def fwd_done(l, outs):
        arrived = dict(zip(AG_WITH_HGRN, outs["hgrn"]))
        arrived.update(zip(AG_WITH_FFN, outs["gate_up"]))
        arrived = {n: arrived[n] for n in BIG}
        full[l + 1] = dict(zip(BIG, _run_comm(_ag_sibling_spec(arrived), f"allgather_sibling_l{l + 1}")))

    stacks = {n: lax.empty(w[n].shape, F32) for n in BIG}
    late = {}
    in_flight = {}

    def pair_sums(l, gw, tag):
        views = {n: _as_halves(n, g) for n, g in gw.items()}
        theirs = _swap_sibling_halves(views, f"swap_halves_{tag}_l{l}")
        return {n: _pair_sum(n, views[n], theirs[n], c_arr, f"pair_sum_{n}_l{l}") for n in gw}

    def finish(l, n, pair, slot):
        stacks[n] = _sum_final(n, pair, slot, stacks[n], l, where, f"sum_chips_{n}_l{l}")

    def bwd_comm(l, gw_ready):
        pairs = pair_sums(l, gw_ready, "early")
        owners = [(l, n) for n in pairs]
        if l + 1 in late:
            pairs["w_in"] = late.pop(l + 1)
            owners.append((l + 1, "w_in"))
        in_flight[l] = (owners, pairs)
        return _scatter_spec(pairs)

    def bwd_done(l, outs):
        owners, pairs = in_flight.pop(l)
        for (lo, n), slot in zip(owners, outs):
            finish(lo, n, pairs[n], slot)

    def layer_grads(l, gw):
        pair = pair_sums(l, {"w_in": gw["w_in"]}, "late")["w_in"]
        if l == 0:
            (slot,) = _run_comm(_scatter_spec({"w_in": pair}), "scatter_l0")
            finish(0, "w_in", pair, slot)
        else:
            late[l] = pair
        return None

    loss, dx, _, d_rb, d_lb, d_gains, d_hg = _local_step(
        x[0], mem[0], target[0], w["rel_bias"], w["lb_logits"], gains, w["hg_norm"], lambda l: full[l],
        dict(fwd_comm=fwd_comm, fwd_done=fwd_done, bwd_comm=bwd_comm, bwd_done=bwd_done, layer_grads=layer_grads))

    flat = jnp.concatenate([d_rb.reshape(-1), d_lb.reshape(-1), d_hg.reshape(-1), d_gains.reshape(-1)])
    small = jnp.pad(flat, (0, SMALL_ROWS * LANES - flat.shape[0])).reshape(SMALL_ROWS, LANES)
    grads, slots = _finish_grads(stacks, small, "finish_grads")
    tot = _sum_slots(slots, "sum_small").reshape(-1)
    n_rb, n_lb = d_rb.size, d_lb.size
    grads["rel_bias"] = tot[:n_rb].reshape(d_rb.shape)
    grads["lb_logits"] = tot[n_rb:n_rb + n_lb].reshape(d_lb.shape)
    grads["hg_norm"] = tot[n_rb + n_lb:n_rb + 2 * n_lb].reshape(d_hg.shape)
    g_full = tot[n_rb + 2 * n_lb:n_rb + 2 * n_lb + d_gains.size].reshape(d_gains.shape)
    cs = D_MODEL // N_CHIPS
    grads["norm_gains"] = lax.dynamic_slice_in_dim(g_full, me * cs, cs, axis=2)

    delta, new_m, new_v = {}, {}, {}
    for n in OUT_ORDER:
        delta[n], new_m[n], new_v[n] = _adamw(w[n], grads[n], m[n], v[n], f"adamw_{n}")
    loss = lax.psum(loss, ("x", "y", "c"))
    return (loss, dx[None], *[grads[n] for n in OUT_ORDER], *[delta[n] for n in OUT_ORDER],
            *[new_m[n] for n in OUT_ORDER], *[new_v[n] for n in OUT_ORDER])
```

```python
import functools
import math

import numpy as np
import jax
import jax.numpy as jnp
from jax import lax
from jax.experimental import pallas as pl
from jax.experimental.pallas import tpu as pltpu

F32 = jnp.float32
BF16 = jnp.bfloat16
MESH_ID = pl.DeviceIdType.MESH

D_MODEL = 2048
HEAD_DIM = 128
N_HEADS = 8
ATTN_W = 1024
HG_W = 1024
HG_CHUNK = 64
Q_BLOCK = 128
DILATIONS = (16, 4, 1)
REL_BUCKETS = 32
REL_MAX_DIST = 2048
CROSS_HEADS = 4
CROSS_DIM = 512
D_FF = 5632
RMS_EPS = 1e-6
NEG_INF = -1e30
N_NORMS = 7

ADAM_LR = 0.001
ADAM_B1 = 0.9
ADAM_B2 = 0.999
ADAM_EPS = 1e-08
ADAM_WD = 0.01
ADAM_STEP = 10

VMEM_LIMIT_V7X = 56 * 1024 * 1024
LANES = 128
N_CHIPS = 4


def _cparams(sem=None):
    if sem is None:
        return pltpu.CompilerParams(vmem_limit_bytes=VMEM_LIMIT_V7X)
    return pltpu.CompilerParams(dimension_semantics=sem, vmem_limit_bytes=VMEM_LIMIT_V7X)


def _blk(n, prefs):
    for p in prefs:
        if p <= n and n % p == 0:
            return p
    return n


def _dot(a, b, dims):
    return lax.dot_general(a.astype(BF16), b.astype(BF16), (dims, ((), ())), preferred_element_type=F32)


def _dot_nn(a, b):
    return _dot(a, b, ((1,), (0,)))


def _dot_nt(a, b):
    return _dot(a, b, ((1,), (1,)))


def _dot_tn(a, b):
    return _dot(a, b, ((0,), (0,)))


def _sigmoid(x):
    return 1.0 / (1.0 + jnp.exp(-x))


def _mm(a_list, b_list, mode, out_dtype, name):
    if not isinstance(a_list, (list, tuple)):
        a_list = [a_list]
    if not isinstance(b_list, (list, tuple)):
        b_list = [b_list]
    na, nb = len(a_list), len(b_list)
    deep = (2048, 2816, 1024, 1408, 512, 256, 128)
    if mode == "nn":
        M, K = a_list[0].shape
        N = b_list[0].shape[1]
        bm, bn, bk = _blk(M, (1024, 512, 256, 128)), _blk(N, (1024, 512, 256, 128)), _blk(K, deep)
    elif mode == "nt":
        M = a_list[0].shape[0]
        K = sum(a.shape[1] for a in a_list)
        N = b_list[0].shape[0]
        bm, bn = _blk(M, (1024, 512, 256, 128)), _blk(N, (1024, 512, 256, 128))
        bk = _blk(math.gcd(*[a.shape[1] for a in a_list]), deep)
    else:
        K, M = a_list[0].shape
        N = sum(b.shape[1] for b in b_list)
        bn = _blk(math.gcd(*[b.shape[1] for b in b_list]), (1024, 2816, 512, 256, 128))
        wide = bn > 1024
        bm = _blk(M, (512, 256, 128) if wide else (1024, 1408, 512, 256, 128))
        bk = _blk(K, (1024, 512, 256, 128) if wide else (2048, 1024, 512, 256, 128))
    nk = K // bk
    grid = (M // bm, N // bn, nk)

    if mode == "nn":
        a_specs = [pl.BlockSpec((bm, bk), lambda i, j, k: (i, k))]
        b_specs = [pl.BlockSpec((bk, bn), lambda i, j, k: (k, j))]
        dims = ((1,), (0,))
    elif mode == "nt":
        a_specs, off = [], 0
        for a in a_list:
            n_i = a.shape[1] // bk
            a_specs.append(pl.BlockSpec((bm, bk), functools.partial(
                lambda i, j, k, off, n_i: (i, jnp.clip(k - off, 0, n_i - 1)), off=off, n_i=n_i)))
            off += n_i
        b_specs = [pl.BlockSpec((bn, bk), lambda i, j, k: (j, k))]
        dims = ((1,), (1,))
    else:
        a_specs = [pl.BlockSpec((bk, bm), lambda i, j, k: (k, i))]
        b_specs, off = [], 0
        for b in b_list:
            n_j = b.shape[1] // bn
            b_specs.append(pl.BlockSpec((bk, bn), functools.partial(
                lambda i, j, k, off, n_j: (jnp.where((j >= off) & (j < off + n_j), k, 0),
                                           jnp.clip(j - off, 0, n_j - 1)), off=off, n_j=n_j)))
            off += n_j
        dims = ((0,), (0,))
    a_bounds = np.cumsum([0] + [a.shape[1] // bk for a in a_list]) if mode == "nt" else None
    b_bounds = np.cumsum([0] + [b.shape[1] // bn for b in b_list]) if mode == "tn" else None

    def body(*refs):
        a_refs = refs[:na]
        b_refs = refs[na:na + nb]
        o_ref = refs[na + nb]
        acc_ref = refs[na + nb + 1] if nk > 1 else None
        j = pl.program_id(1)
        k = pl.program_id(2)

        def accumulate(p):
            if nk == 1:
                o_ref[...] = p.astype(out_dtype)
                return

            @pl.when(k == 0)
            def _():
                acc_ref[...] = p

            @pl.when(k > 0)
            def _():
                acc_ref[...] += p

        if na > 1:
            for t in range(na):
                @pl.when((k >= int(a_bounds[t])) & (k < int(a_bounds[t + 1])))
                def _(t=t):
                    accumulate(_dot(a_refs[t][...], b_refs[0][...], dims))
        elif nb > 1:
            for t in range(nb):
                @pl.when((j >= int(b_bounds[t])) & (j < int(b_bounds[t + 1])))
                def _(t=t):
                    accumulate(_dot(a_refs[0][...], b_refs[t][...], dims))
        else:
            accumulate(_dot(a_refs[0][...], b_refs[0][...], dims))

        if nk > 1:
            @pl.when(k == nk - 1)
            def _():
                o_ref[...] = acc_ref[...].astype(out_dtype)

    return pl.pallas_call(
        body, name=name, grid=grid,
        in_specs=a_specs + b_specs,
        out_specs=pl.BlockSpec((bm, bn), lambda i, j, k: (i, j)),
        out_shape=jax.ShapeDtypeStruct((M, N), out_dtype),
        scratch_shapes=[pltpu.VMEM((bm, bn), F32)] if nk > 1 else [],
        compiler_params=_cparams(("parallel", "parallel", "arbitrary")),
    )(*a_list, *b_list)


def _rms_fwd(x, g, name):
    S, D = x.shape
    tm = _blk(S, (256, 128, 64, 8))

    def body(x_ref, g_ref, o_ref):
        xv = x_ref[...]
        r = lax.rsqrt(jnp.mean(xv * xv, axis=-1, keepdims=True) + RMS_EPS)
        o_ref[...] = ((xv * r) * g_ref[...]).astype(o_ref.dtype)

    return pl.pallas_call(
        body, name=name, grid=(S // tm,),
        in_specs=[pl.BlockSpec((tm, D), lambda i: (i, 0)), pl.BlockSpec((1, D), lambda i: (0, 0))],
        out_specs=pl.BlockSpec((tm, D), lambda i: (i, 0)),
        out_shape=jax.ShapeDtypeStruct((S, D), BF16),
        compiler_params=_cparams(("parallel",)),
    )(x, g)


def _norm_residual(x, t, g, g_next, name):
    S, D = x.shape
    tm = _blk(S, (256, 128, 64, 8))
    chain = g_next is not None

    def body(*refs):
        if chain:
            x_ref, t_ref, g_ref, gn_ref, o_ref, h_ref = refs
        else:
            x_ref, t_ref, g_ref, o_ref = refs
        tv = t_ref[...]
        r = lax.rsqrt(jnp.mean(tv * tv, axis=-1, keepdims=True) + RMS_EPS)
        xn = x_ref[...] + (tv * r) * g_ref[...]
        o_ref[...] = xn
        if chain:
            rn = lax.rsqrt(jnp.mean(xn * xn, axis=-1, keepdims=True) + RMS_EPS)
            h_ref[...] = ((xn * rn) * gn_ref[...]).astype(BF16)

    row = pl.BlockSpec((tm, D), lambda i: (i, 0))
    vec = pl.BlockSpec((1, D), lambda i: (0, 0))
    return pl.pallas_call(
        body, name=name, grid=(S // tm,),
        in_specs=[row, row, vec] + ([vec] if chain else []),
        out_specs=[row, row] if chain else row,
        out_shape=[jax.ShapeDtypeStruct((S, D), F32), jax.ShapeDtypeStruct((S, D), BF16)] if chain
        else jax.ShapeDtypeStruct((S, D), F32),
        compiler_params=_cparams(("parallel",)),
    )(*([x, t, g, g_next] if chain else [x, t, g]))


def _rms_bwd(x, g, dh, res, out_dtype, name):
    S, D = x.shape
    tm = _blk(S, (256, 128, 64, 8))
    has_res = res is not None

    def body(*refs):
        if has_res:
            x_ref, g_ref, dh_ref, res_ref, dx_ref, dg_ref = refs
        else:
            x_ref, g_ref, dh_ref, dx_ref, dg_ref = refs
        xv = x_ref[...]
        r = lax.rsqrt(jnp.mean(xv * xv, axis=-1, keepdims=True) + RMS_EPS)
        xh = xv * r
        dhv = dh_ref[...].astype(F32)
        gd = dhv * g_ref[...]
        dx = r * (gd - xh * jnp.mean(gd * xh, axis=-1, keepdims=True))
        if has_res:
            dx = dx + res_ref[...]
        dx_ref[...] = dx.astype(out_dtype)
        part = jnp.sum(dhv * xh, axis=0, keepdims=True)

        @pl.when(pl.program_id(0) == 0)
        def _():
            dg_ref[...] = part

        @pl.when(pl.program_id(0) > 0)
        def _():
            dg_ref[...] += part

    row = pl.BlockSpec((tm, D), lambda i: (i, 0))
    vec = pl.BlockSpec((1, D), lambda i: (0, 0))
    ins = [x, g, dh] + ([res] if has_res else [])
    return pl.pallas_call(
        body, name=name, grid=(S // tm,),
        in_specs=[row, vec, row] + ([row] if has_res else []),
        out_specs=[row, vec],
        out_shape=[jax.ShapeDtypeStruct((S, D), out_dtype), jax.ShapeDtypeStruct((1, D), F32)],
        compiler_params=_cparams(("arbitrary",)),
    )(*ins)


def _rel_bucket(dist):
    max_exact = REL_BUCKETS // 2
    d_f = jnp.maximum(dist, 1).astype(jnp.float32)
    large = max_exact + (jnp.log(d_f / max_exact) / math.log(REL_MAX_DIST / max_exact)
                         * (REL_BUCKETS - max_exact)).astype(jnp.int32)
    large = jnp.minimum(large, REL_BUCKETS - 1)
    return jnp.where(dist < max_exact, dist, large)


def _bias_tables(rel_bias):
    qi = jnp.arange(Q_BLOCK)[:, None]
    kj = jnp.arange(2 * Q_BLOCK)[None, :]
    m = qi - kj + Q_BLOCK
    band_ok = (m >= 0) & (m <= Q_BLOCK)
    tabs = []
    for d in DILATIONS:
        bucket = _rel_bucket(jnp.maximum(m, 0) * d)
        onehot = (bucket[:, :, None] == jnp.arange(REL_BUCKETS)[None, None, :]).astype(F32)
        bias = jnp.einsum("qkb,bh->hqk", onehot, rel_bias.astype(F32), precision=lax.Precision.HIGHEST)
        tabs.append(jnp.where(band_ok[None], bias, NEG_INF))
    return jnp.stack(tabs)


HEADS_PER_STEP = {16: 1, 4: 1, 1: 8}


def _rows(r, d):
    return pl.ds(r, Q_BLOCK, stride=d) if d > 1 else pl.ds(0, Q_BLOCK)


def _attn_fwd(proj, bias, state, d, last, cat_in, name):
    S = proj.shape[0]
    R = Q_BLOCK * d
    nsb = S // R
    first = state is None
    scale = HEAD_DIM ** -0.5
    hb = HEADS_PER_STEP[d]
    W = hb * HEAD_DIM

    def body(*refs):
        q_ref, kp_ref, kc_ref, vp_ref, vc_ref, b_ref = refs[:6]
        pos = 6
        if not first:
            m_in, l_in, a_in = refs[pos:pos + 3]
            pos += 3
        if last:
            pos += 1
            o_ref, lse_ref, o_tmp = refs[pos:pos + 3]
        else:
            m_out, l_out, a_out = refs[pos:pos + 3]
        n = pl.program_id(0)
        for hh in range(hb):
            ls = slice(hh * HEAD_DIM, (hh + 1) * HEAD_DIM)
            bp = b_ref[hh, :, :Q_BLOCK]
            bc = b_ref[hh, :, Q_BLOCK:]
            for r in range(d):
                rows = _rows(r, d)
                q = q_ref[rows, ls]
                sp = _dot_nt(q, kp_ref[rows, ls]) * scale + bp
                sp = jnp.where(n == 0, NEG_INF, sp)
                sc = _dot_nt(q, kc_ref[rows, ls]) * scale + bc
                mrow = jnp.maximum(jnp.max(sp, axis=-1, keepdims=True), jnp.max(sc, axis=-1, keepdims=True))
                if first:
                    m_new = mrow
                else:
                    m_old = m_in[rows, ls][:, :1]
                    m_new = jnp.maximum(m_old, mrow)
                pp = jnp.exp(sp - m_new)
                pc = jnp.exp(sc - m_new)
                lrow = jnp.sum(pp, axis=-1, keepdims=True) + jnp.sum(pc, axis=-1, keepdims=True)
                pv = _dot_nn(pp, vp_ref[rows, ls]) + _dot_nn(pc, vc_ref[rows, ls])
                if first:
                    l_new, a_new = lrow, pv
                else:
                    alpha = jnp.exp(m_old - m_new)
                    l_new = alpha * l_in[rows, ls][:, :1] + lrow
                    a_new = alpha * a_in[rows, ls] + pv
                if last:
                    o_tmp[rows, ls] = a_new / l_new
                    lse_ref[rows, ls] = jnp.broadcast_to(m_new + jnp.log(l_new), (Q_BLOCK, HEAD_DIM))
                else:
                    m_out[rows, ls] = jnp.broadcast_to(m_new, (Q_BLOCK, HEAD_DIM))
                    l_out[rows, ls] = jnp.broadcast_to(l_new, (Q_BLOCK, HEAD_DIM))
                    a_out[rows, ls] = a_new
        if last:
            o_ref[...] = o_tmp[...].astype(BF16)

    ng = N_HEADS // hb

    def col(g0):
        return pl.BlockSpec((R, W), lambda n, h: (n, g0 * ng + h))

    def col_prev(g0):
        return pl.BlockSpec((R, W), lambda n, h: (jnp.maximum(n - 1, 0), g0 * ng + h))

    in_specs = [col(0), col_prev(1), col(1), col_prev(2), col(2),
                pl.BlockSpec((hb, Q_BLOCK, 2 * Q_BLOCK), lambda n, h: (h, 0, 0))]
    ins = [proj, proj, proj, proj, proj, bias]
    if not first:
        in_specs += [col(0)] * 3
        ins += list(state)
    st = jax.ShapeDtypeStruct((S, ATTN_W), F32)
    if last:
        in_specs.append(pl.BlockSpec(memory_space=pl.ANY))
        ins.append(cat_in)
        out_specs = [col(0), col(0)]
        out_shape = [jax.ShapeDtypeStruct(cat_in.shape, BF16), st]
        scratch = [pltpu.VMEM((R, W), F32)]
        aliases = {len(ins) - 1: 0}
    else:
        out_specs = [col(0)] * 3
        out_shape = [st, st, st]
        scratch = []
        aliases = {}
    return pl.pallas_call(
        body, name=name, grid=(nsb, ng), in_specs=in_specs, out_specs=out_specs,
        out_shape=out_shape, scratch_shapes=scratch, input_output_aliases=aliases,
        compiler_params=_cparams(("arbitrary", "arbitrary")),
    )(*ins)


def _attn_bwd(proj, bias, cat, d_cat, lse, acc, d, last, name):
    S = proj.shape[0]
    R = Q_BLOCK * d
    nsb = S // R
    first = acc is None
    scale = HEAD_DIM ** -0.5
    odt = BF16 if last else F32
    hb = HEADS_PER_STEP[d]
    W = hb * HEAD_DIM

    def body(*refs):
        q_ref, kp_ref, kc_ref, vp_ref, vc_ref, b_ref, o_ref, do_ref, lse_ref = refs[:9]
        pos = 9
        if not first:
            dq_in, dk_in, dv_in = refs[pos:pos + 3]
            pos += 3
        dq_out, dk_out, dv_out, db_out = refs[pos:pos + 4]
        pos += 4
        ck, cv, o_f, do_f, dq_t, dk_t, dv_t, db_acc = refs[pos:pos + 8]
        n = pl.program_id(1)

        @pl.when(n == 0)
        def _():
            db_acc[...] = jnp.zeros_like(db_acc)
            ck[...] = jnp.zeros_like(ck)
            cv[...] = jnp.zeros_like(cv)

        @pl.when(n < nsb)
        def _():
            o_f[...] = o_ref[...].astype(F32)
            do_f[...] = do_ref[...].astype(F32)
            for hh in range(hb):
                ls = slice(hh * HEAD_DIM, (hh + 1) * HEAD_DIM)
                bp = b_ref[hh, :, :Q_BLOCK]
                bc = b_ref[hh, :, Q_BLOCK:]
                for r in range(d):
                    rows = _rows(r, d)
                    q = q_ref[rows, ls]
                    kp = kp_ref[rows, ls]
                    kc = kc_ref[rows, ls]
                    vp = vp_ref[rows, ls]
                    vc = vc_ref[rows, ls]
                    do = do_f[rows, ls]
                    lse_r = lse_ref[rows, ls][:, :1]
                    sp = _dot_nt(q, kp) * scale + bp
                    sp = jnp.where(n == 0, NEG_INF, sp)
                    sc = _dot_nt(q, kc) * scale + bc
                    pp = jnp.exp(sp - lse_r)
                    pc = jnp.exp(sc - lse_r)
                    dd = jnp.sum(do * o_f[rows, ls], axis=-1, keepdims=True)
                    dsp = pp * (_dot_nt(do, vp) - dd)
                    dsc = pc * (_dot_nt(do, vc) - dd)
                    db_acc[hh, :, :Q_BLOCK] += dsp
                    db_acc[hh, :, Q_BLOCK:] += dsc
                    dq = (_dot_nn(dsp, kp) + _dot_nn(dsc, kc)) * scale
                    dk_prev = ck[rows, ls] + _dot_tn(dsp, q) * scale
                    dv_prev = cv[rows, ls] + _dot_tn(pp, do)
                    ck[rows, ls] = _dot_tn(dsc, q) * scale
                    cv[rows, ls] = _dot_tn(pc, do)
                    if not first:
                        dq = dq + dq_in[rows, ls]
                        dk_prev = dk_prev + dk_in[rows, ls]
                        dv_prev = dv_prev + dv_in[rows, ls]
                    dq_t[rows, ls] = dq
                    dk_t[rows, ls] = dk_prev
                    dv_t[rows, ls] = dv_prev
            dq_out[...] = dq_t[...].astype(odt)

            @pl.when(n > 0)
            def _():
                dk_out[...] = dk_t[...].astype(odt)
                dv_out[...] = dv_t[...].astype(odt)

        @pl.when(n == nsb)
        def _():
            if first:
                dk_out[...] = ck[...].astype(odt)
                dv_out[...] = cv[...].astype(odt)
            else:
                dk_out[...] = (ck[...] + dk_in[...]).astype(odt)
                dv_out[...] = (cv[...] + dv_in[...]).astype(odt)
            db_out[...] = db_acc[...]

    last_n = nsb - 1
    ng = N_HEADS // hb

    def cur(g0):
        return pl.BlockSpec((R, W), lambda h, n: (jnp.minimum(n, last_n), g0 * ng + h))

    def prev(g0):
        return pl.BlockSpec((R, W), lambda h, n: (jnp.maximum(jnp.minimum(n, last_n) - 1, 0), g0 * ng + h))

    delayed = pl.BlockSpec((R, W), lambda h, n: (jnp.maximum(n - 1, 0), h))
    in_specs = [cur(0), prev(1), cur(1), prev(2), cur(2),
                pl.BlockSpec((hb, Q_BLOCK, 2 * Q_BLOCK), lambda h, n: (h, 0, 0)),
                cur(0), cur(0), cur(0)]
    ins = [proj, proj, proj, proj, proj, bias, cat, d_cat, lse]
    if not first:
        in_specs += [cur(0), delayed, delayed]
        ins += list(acc)
    st = jax.ShapeDtypeStruct((S, ATTN_W), odt)
    tile = pltpu.VMEM((R, W), F32)
    return pl.pallas_call(
        body, name=name, grid=(ng, nsb + 1), in_specs=in_specs,
        out_specs=[cur(0), delayed, delayed,
                   pl.BlockSpec((hb, Q_BLOCK, 2 * Q_BLOCK), lambda h, n: (h, 0, 0))],
        out_shape=[st, st, st, jax.ShapeDtypeStruct((N_HEADS, Q_BLOCK, 2 * Q_BLOCK), F32)],
        scratch_shapes=[tile] * 7 + [pltpu.VMEM((hb, Q_BLOCK, 2 * Q_BLOCK), F32)],
        compiler_params=_cparams(("arbitrary", "arbitrary")),
    )(*ins)


HG_LEVELS = (32, 16, 8, 4, 2, 1)
N_LEV = len(HG_LEVELS)


def _hg_consts():
    C = HG_CHUNK
    t = np.arange(C)
    mq, mk, masks = [], [], []
    for B in HG_LEVELS:
        up = (t // B) % 2 == 1
        bs = (t // B) * B
        be = bs + B - 1
        mq.append(up[:, None] & (t[None, :] >= bs[:, None]) & (t[None, :] <= t[:, None]))
        mk.append((~up)[:, None] & (t[None, :] > t[:, None]) & (t[None, :] <= be[:, None]))
        masks.append(up[:, None] & (~up)[None, :] & ((t[:, None] // (2 * B)) == (t[None, :] // (2 * B))))
    masks.append(np.eye(C, dtype=bool))
    mb = t[None, :] <= t[:, None]
    mw = t[None, :] > t[:, None]
    m_all = np.concatenate(mq + mk + [mb, mw], axis=0).astype(np.float32)
    return jnp.asarray(m_all, BF16), jnp.asarray(np.stack(masks).astype(np.float32))


def _split3(v):
    hi = v.astype(BF16)
    r1 = v - hi.astype(F32)
    mid = r1.astype(BF16)
    lo = (r1 - mid.astype(F32)).astype(BF16)
    return jnp.concatenate([hi, mid, lo], axis=1)


def _hg_chunk_fwd(fz, iv, qz, lbh, m_all, mask_ref, st_t):
    C = HG_CHUNK
    sig = _sigmoid(fz)
    f = lbh + (1.0 - lbh) * sig
    lf = jnp.log(f)
    kk = 1.0 - f
    sq = _sigmoid(qz)
    qq = qz * sq
    a3 = lax.dot_general(m_all, _split3(lf), (((1,), (0,)), ((), ())), preferred_element_type=F32)
    args = a3[:, :HEAD_DIM] + a3[:, HEAD_DIM:2 * HEAD_DIM] + a3[:, 2 * HEAD_DIM:]
    e = jnp.exp(args)
    qs = [qq * e[j * C:(j + 1) * C] for j in range(N_LEV)]
    ks = [kk * e[(N_LEV + j) * C:(N_LEV + j + 1) * C] for j in range(N_LEV)]
    a = mask_ref[N_LEV] * _dot_nt(qq, kk)
    for j in range(N_LEV):
        a = a + mask_ref[j] * _dot_nt(qs[j], ks[j])
    eb = e[2 * N_LEV * C:(2 * N_LEV + 1) * C]
    ew = e[(2 * N_LEV + 1) * C:]
    qe = qq * eb
    w = kk * ew
    o = _dot_nt(qe, st_t) + _dot_nn(a, iv)
    eb_last = eb[C - 1:C, :]
    new_st = st_t * eb_last + _dot_tn(iv, w)
    return dict(sig=sig, f=f, kk=kk, sq=sq, qq=qq, e=e, qs=qs, ks=ks, a=a, eb=eb, ew=ew, qe=qe, w=w, o=o,
                eb_last=eb_last, new_st=new_st)


def _hg_rows(S):
    return _blk(S, (256, 128, 64))


def _host_refs(refs, n_in, n_out, n_scratch, comm):
    nci = len(comm["ins"]) if comm else 0
    nco = len(comm["out_shapes"]) if comm else 0
    a = n_in
    b = a + nci
    c = b + n_out
    d = c + nco
    e = d + n_scratch
    return refs[:a], refs[a:b], refs[b:c], refs[c:d], refs[d:e], refs[e:]


def _host_call(comm, n_in, n_out, in_specs, out_specs, out_shape, scratch, aliases):
    if comm:
        in_specs = in_specs + [ANY] * len(comm["ins"])
        out_specs = out_specs + [ANY] * len(comm["out_shapes"])
        out_shape = out_shape + comm["out_shapes"]
        scratch = scratch + [pltpu.SemaphoreType.DMA((comm["nsem"],)), pltpu.SemaphoreType.DMA((comm["nsem"],))]
        aliases = dict(aliases)
        aliases.update({n_in + i: n_out + o for i, o in comm["aliases"].items()})
    return dict(in_specs=in_specs, out_specs=out_specs, out_shape=out_shape, scratch_shapes=scratch,
                input_output_aliases=aliases)


def _hgrn_fwd(proj, lb, gain, cat_in, name, comm=None):
    S = proj.shape[0]
    rows = _hg_rows(S)
    cb = rows // HG_CHUNK
    nsteps = S // rows
    m_all, masks = _hg_consts()

    def body(*refs):
        own_in, c_in, own_out, c_out, own_scr, sems = _host_refs(refs, 9, 4, 1, comm)
        fz_ref, iv_ref, qz_ref, gz_ref, lb_ref, gain_ref, m_ref, mask_ref, _ = own_in
        o_ref, st_ref, a_ref, raw_ref = own_out
        (st_scr,) = own_scr

        @pl.when(pl.program_id(0) == 0)
        def _():
            st_scr[...] = jnp.zeros_like(st_scr)
            if comm:
                comm["start"](c_in, c_out, *sems)

        m_all_v = m_ref[...]

        def chunk(c, carry):
            rs = pl.ds(pl.multiple_of(c * HG_CHUNK, HG_CHUNK), HG_CHUNK)
            for h in range(N_HEADS):
                sl = slice(h * HEAD_DIM, (h + 1) * HEAD_DIM)
                st_t = st_scr[h]
                st_ref[c, h] = st_t
                gz = gz_ref[rs, sl]
                iv = iv_ref[rs, sl]
                q = _hg_chunk_fwd(fz_ref[rs, sl], iv, qz_ref[rs, sl], lb_ref[:, sl], m_all_v, mask_ref, st_t)
                st_scr[h] = q["new_st"]
                o = q["o"]
                a_ref[c, h] = q["a"].astype(BF16)
                raw_ref[rs, sl] = o
                r = lax.rsqrt(jnp.mean(o * o, axis=-1, keepdims=True) + RMS_EPS)
                y = ((o * r) * gain_ref[:, sl]) * (gz * _sigmoid(gz))
                o_ref[rs, sl] = y.astype(BF16)
            return carry

        lax.fori_loop(0, cb, chunk, 0)
        if comm:
            @pl.when(pl.program_id(0) == nsteps - 1)
            def _():
                comm["wait"](c_in, c_out, *sems)

    def col(c):
        return pl.BlockSpec((rows, HG_W), lambda i: (i, c))

    vec = pl.BlockSpec((1, HG_W), lambda i: (0, 0))
    args = _host_call(
        comm, 9, 4,
        in_specs=[col(3), col(4), col(5), col(6), vec, vec,
                  pl.BlockSpec(m_all.shape, lambda i: (0, 0)), pl.BlockSpec(masks.shape, lambda i: (0, 0, 0)),
                  pl.BlockSpec(memory_space=pl.ANY)],
        out_specs=[col(1), pl.BlockSpec((cb, N_HEADS, HEAD_DIM, HEAD_DIM), lambda i: (i, 0, 0, 0)),
                   pl.BlockSpec((cb, N_HEADS, HG_CHUNK, HG_CHUNK), lambda i: (i, 0, 0, 0)), col(0)],
        out_shape=[jax.ShapeDtypeStruct(cat_in.shape, BF16),
                   jax.ShapeDtypeStruct((S // HG_CHUNK, N_HEADS, HEAD_DIM, HEAD_DIM), F32),
                   jax.ShapeDtypeStruct((S // HG_CHUNK, N_HEADS, HG_CHUNK, HG_CHUNK), BF16),
                   jax.ShapeDtypeStruct((S, HG_W), F32)],
        scratch=[pltpu.VMEM((N_HEADS, HEAD_DIM, HEAD_DIM), F32)],
        aliases={8: 0})
    return pl.pallas_call(
        body, name=name, grid=(nsteps,), compiler_params=_cparams(("arbitrary",)), **args,
    )(proj, proj, proj, proj, lb, gain, m_all, masks, cat_in, *(comm["ins"] if comm else []))


def _hgrn_bwd(proj, lb, gain, states, a_saved, o_saved, d_cat, name, comm=None):
    S = proj.shape[0]
    rows = _hg_rows(S)
    cb = rows // HG_CHUNK
    nblk = S // rows
    C = HG_CHUNK
    m_all, masks = _hg_consts()

    def body(*refs):
        own_in, c_in, own_out, c_out, own_scr, sems = _host_refs(refs, 12, 3, 1, comm)
        fz_ref, iv_ref, qz_ref, gz_ref, lb_ref, gain_ref, m_ref, mask_ref, st_ref, a_ref, raw_ref, dy_ref = own_in
        dz_ref, dlb_ref, dgain_ref = own_out
        (dst_scr,) = own_scr

        @pl.when(pl.program_id(0) == 0)
        def _():
            dst_scr[...] = jnp.zeros_like(dst_scr)
            dlb_ref[...] = jnp.zeros_like(dlb_ref)
            dgain_ref[...] = jnp.zeros_like(dgain_ref)
            if comm:
                comm["start"](c_in, c_out, *sems)

        m_all_v = m_ref[...]
        last_row = lax.broadcasted_iota(jnp.int32, (C, HEAD_DIM), 0) == C - 1

        def chunk(ci, carry):
            c = cb - 1 - ci
            rs = pl.ds(pl.multiple_of(c * C, C), C)
            for h in range(N_HEADS):
                sl = slice(h * HEAD_DIM, (h + 1) * HEAD_DIM)
                lbh = lb_ref[:, sl]
                gh = gain_ref[:, sl]
                st_t = st_ref[c, h]
                fz = fz_ref[rs, sl]
                iv = iv_ref[rs, sl]
                qz = qz_ref[rs, sl]
                gz = gz_ref[rs, sl]
                q = _hg_chunk_fwd(fz, iv, qz, lbh, m_all_v, mask_ref, st_t)
                o = raw_ref[rs, sl]
                a_fwd = a_ref[c, h]
                r = lax.rsqrt(jnp.mean(o * o, axis=-1, keepdims=True) + RMS_EPS)
                on = o * r
                sg = _sigmoid(gz)
                gate = gz * sg
                dy = dy_ref[rs, sl].astype(F32)
                dgain_ref[:, sl] += jnp.sum(dy * on * gate, axis=0, keepdims=True)
                dgz = (dy * on * gh) * (sg * (1.0 + gz * (1.0 - sg)))
                don = dy * gh * gate
                do = r * (don - on * jnp.mean(don * on, axis=-1, keepdims=True))
                da = _dot_nt(do, iv)
                dv = _dot_tn(a_fwd, do)
                dqe = _dot_nn(do, st_t)
                dst_new = _dot_tn(do, q["qe"])
                dsp = dst_scr[h]
                dw = _dot_nn(iv, dsp)
                dv = dv + _dot_nt(q["w"], dsp)
                d_eb_last = jnp.sum(dsp * st_t, axis=0, keepdims=True)
                dst_scr[h] = dsp * q["eb_last"] + dst_new
                dad = da * mask_ref[N_LEV]
                dq = _dot_nn(dad, q["kk"])
                dk = _dot_tn(dad, q["qq"])
                dargs_q = []
                dargs_k = []
                for j in range(N_LEV):
                    daj = da * mask_ref[j]
                    dqj = _dot_nn(daj, q["ks"][j])
                    dkj = _dot_tn(daj, q["qs"][j])
                    dq = dq + dqj * q["e"][j * C:(j + 1) * C]
                    dk = dk + dkj * q["e"][(N_LEV + j) * C:(N_LEV + j + 1) * C]
                    dargs_q.append(dqj * q["qs"][j])
                    dargs_k.append(dkj * q["ks"][j])
                dq = dq + dqe * q["eb"]
                darg_b = dqe * q["qe"] + jnp.where(last_row, d_eb_last * q["eb_last"], 0.0)
                dk = dk + dw * q["ew"]
                darg_w = dw * q["w"]
                dall = jnp.concatenate(dargs_q + dargs_k + [darg_b, darg_w], axis=0)
                hi = dall.astype(BF16)
                lo = (dall - hi.astype(F32)).astype(BF16)
                dl2 = lax.dot_general(m_all_v, jnp.concatenate([hi, lo], axis=1), (((0,), (0,)), ((), ())),
                                      preferred_element_type=F32)
                dlf = dl2[:, :HEAD_DIM] + dl2[:, HEAD_DIM:]
                df = dlf / q["f"] - dk
                sig = q["sig"]
                dlb_ref[:, sl] += jnp.sum(df * (1.0 - sig), axis=0, keepdims=True)
                dfz = df * (1.0 - lbh) * (sig * (1.0 - sig))
                sq = q["sq"]
                dqz = dq * (sq * (1.0 + qz * (1.0 - sq)))
                dz_ref[rs, h * HEAD_DIM:(h + 1) * HEAD_DIM] = dfz.astype(BF16)
                dz_ref[rs, HG_W + h * HEAD_DIM:HG_W + (h + 1) * HEAD_DIM] = dv.astype(BF16)
                dz_ref[rs, 2 * HG_W + h * HEAD_DIM:2 * HG_W + (h + 1) * HEAD_DIM] = dqz.astype(BF16)
                dz_ref[rs, 3 * HG_W + h * HEAD_DIM:3 * HG_W + (h + 1) * HEAD_DIM] = dgz.astype(BF16)
            return carry

        lax.fori_loop(0, cb, chunk, 0)
        if comm:
            @pl.when(pl.program_id(0) == nblk - 1)
            def _():
                comm["wait"](c_in, c_out, *sems)

    def col(c):
        return pl.BlockSpec((rows, HG_W), lambda i: (nblk - 1 - i, c))

    vec = pl.BlockSpec((1, HG_W), lambda i: (0, 0))
    args = _host_call(
        comm, 12, 3,
        in_specs=[col(3), col(4), col(5), col(6), vec, vec,
                  pl.BlockSpec(m_all.shape, lambda i: (0, 0)), pl.BlockSpec(masks.shape, lambda i: (0, 0, 0)),
                  pl.BlockSpec((cb, N_HEADS, HEAD_DIM, HEAD_DIM), lambda i: (nblk - 1 - i, 0, 0, 0)),
                  pl.BlockSpec((cb, N_HEADS, HG_CHUNK, HG_CHUNK), lambda i: (nblk - 1 - i, 0, 0, 0)),
                  col(0), col(1)],
        out_specs=[pl.BlockSpec((rows, 4 * HG_W), lambda i: (nblk - 1 - i, 0)), vec, vec],
        out_shape=[jax.ShapeDtypeStruct((S, 4 * HG_W), BF16), jax.ShapeDtypeStruct((1, HG_W), F32),
                   jax.ShapeDtypeStruct((1, HG_W), F32)],
        scratch=[pltpu.VMEM((N_HEADS, HEAD_DIM, HEAD_DIM), F32)],
        aliases={})
    return pl.pallas_call(
        body, name=name, grid=(nblk,), compiler_params=_cparams(("arbitrary",)), **args,
    )(proj, proj, proj, proj, lb, gain, m_all, masks, states, a_saved, o_saved, d_cat,
      *(comm["ins"] if comm else []))


def _cross_fwd(cq, ckv, name):
    S = cq.shape[0]
    n_mem = ckv.shape[0]
    tq = _blk(S, (256, 128))
    scale = CROSS_DIM ** -0.5

    def body(q_ref, kv_ref, o_ref):
        for h in range(CROSS_HEADS):
            sl = slice(h * CROSS_DIM, (h + 1) * CROSS_DIM)
            k = kv_ref[:, sl]
            v = kv_ref[:, D_MODEL + h * CROSS_DIM:D_MODEL + (h + 1) * CROSS_DIM]
            s = _dot_nt(q_ref[:, sl], k) * scale
            m = jnp.max(s, axis=-1, keepdims=True)
            p = jnp.exp(s - m)
            p = p / jnp.sum(p, axis=-1, keepdims=True)
            o_ref[:, sl] = _dot_nn(p, v).astype(BF16)

    return pl.pallas_call(
        body, name=name, grid=(S // tq,),
        in_specs=[pl.BlockSpec((tq, D_MODEL), lambda i: (i, 0)), pl.BlockSpec((n_mem, 2 * D_MODEL), lambda i: (0, 0))],
        out_specs=pl.BlockSpec((tq, D_MODEL), lambda i: (i, 0)),
        out_shape=jax.ShapeDtypeStruct((S, D_MODEL), BF16),
        compiler_params=_cparams(("parallel",)),
    )(cq, ckv)


def _cross_bwd(cq, ckv, d_o, name):
    S = cq.shape[0]
    n_mem = ckv.shape[0]
    tq = _blk(S, (256, 128))
    scale = CROSS_DIM ** -0.5

    def body(q_ref, kv_ref, do_ref, dq_ref, dkv_ref):
        @pl.when(pl.program_id(0) == 0)
        def _():
            dkv_ref[...] = jnp.zeros_like(dkv_ref)

        for h in range(CROSS_HEADS):
            sl = slice(h * CROSS_DIM, (h + 1) * CROSS_DIM)
            slv = slice(D_MODEL + h * CROSS_DIM, D_MODEL + (h + 1) * CROSS_DIM)
            q = q_ref[:, sl]
            k = kv_ref[:, sl]
            v = kv_ref[:, slv]
            do = do_ref[:, sl]
            s = _dot_nt(q, k) * scale
            m = jnp.max(s, axis=-1, keepdims=True)
            p = jnp.exp(s - m)
            p = p / jnp.sum(p, axis=-1, keepdims=True)
            dp = _dot_nt(do, v)
            ds = p * (dp - jnp.sum(dp * p, axis=-1, keepdims=True)) * scale
            dq_ref[:, sl] = _dot_nn(ds, k).astype(BF16)
            dkv_ref[:, sl] += _dot_tn(ds, q)
            dkv_ref[:, slv] += _dot_tn(p, do)

    row = pl.BlockSpec((tq, D_MODEL), lambda i: (i, 0))
    kv = pl.BlockSpec((n_mem, 2 * D_MODEL), lambda i: (0, 0))
    return pl.pallas_call(
        body, name=name, grid=(S // tq,),
        in_specs=[row, kv, row], out_specs=[row, kv],
        out_shape=[jax.ShapeDtypeStruct((S, D_MODEL), BF16), jax.ShapeDtypeStruct((n_mem, 2 * D_MODEL), F32)],
        compiler_params=_cparams(("arbitrary",)),
    )(cq, ckv, d_o)


FF_BLOCK = 512


def _mm_gate_up(hf, w_gu, name, comm=None):
    S, K = hf.shape
    bm = _blk(S, (1024, 512, 256, 128))
    ni = S // bm
    nj = D_FF // FF_BLOCK

    def body(*refs):
        own_in, c_in, own_out, c_out, _, sems = _host_refs(refs, 3, 3, 0, comm)
        a_ref, bg_ref, bu_ref = own_in
        g_ref, u_ref, act_ref = own_out
        if comm:
            @pl.when((pl.program_id(0) == 0) & (pl.program_id(1) == 0))
            def _():
                comm["start"](c_in, c_out, *sems)

        a = a_ref[...]
        g = _dot_nn(a, bg_ref[...])
        u = _dot_nn(a, bu_ref[...])
        g_ref[...] = g.astype(BF16)
        u_ref[...] = u.astype(BF16)
        act_ref[...] = ((g * _sigmoid(g)) * u).astype(BF16)
        if comm:
            @pl.when((pl.program_id(0) == ni - 1) & (pl.program_id(1) == nj - 1))
            def _():
                comm["wait"](c_in, c_out, *sems)

    out = pl.BlockSpec((bm, FF_BLOCK), lambda i, j: (i, j))
    sds = jax.ShapeDtypeStruct((S, D_FF), BF16)
    args = _host_call(
        comm, 3, 3,
        in_specs=[pl.BlockSpec((bm, K), lambda i, j: (i, 0)), pl.BlockSpec((K, FF_BLOCK), lambda i, j: (0, j)),
                  pl.BlockSpec((K, FF_BLOCK), lambda i, j: (0, j + nj))],
        out_specs=[out, out, out], out_shape=[sds, sds, sds], scratch=[], aliases={})
    return pl.pallas_call(
        body, name=name, grid=(ni, nj), compiler_params=_cparams(("arbitrary", "arbitrary")), **args,
    )(hf, w_gu, w_gu, *(comm["ins"] if comm else []))


def _mm_down_x(dy, w_down, g, u, name):
    S, K = dy.shape
    bm = _blk(S, (1024, 512, 256, 128))
    nj = D_FF // FF_BLOCK

    def body(a_ref, b_ref, g_ref, u_ref, dg_ref, du_ref):
        da = _dot_nt(a_ref[...], b_ref[...])
        gv = g_ref[...].astype(F32)
        sg = _sigmoid(gv)
        dg_ref[...] = (da * u_ref[...].astype(F32) * (sg * (1.0 + gv * (1.0 - sg)))).astype(BF16)
        du_ref[...] = (da * (gv * sg)).astype(BF16)

    tile = pl.BlockSpec((bm, FF_BLOCK), lambda i, j: (i, j))
    sds = jax.ShapeDtypeStruct((S, D_FF), BF16)
    return pl.pallas_call(
        body, name=name, grid=(S // bm, nj),
        in_specs=[pl.BlockSpec((bm, K), lambda i, j: (i, 0)), pl.BlockSpec((FF_BLOCK, K), lambda i, j: (j, 0)),
                  tile, tile],
        out_specs=[tile, tile], out_shape=[sds, sds],
        compiler_params=_cparams(("parallel", "arbitrary")),
    )(dy, w_down, g, u)


def _loss_head(y, target, name):
    S, D = y.shape
    tm = _blk(S, (256, 128, 64, 8))

    def body(y_ref, t_ref, dy_ref, l_ref):
        diff = y_ref[...] - t_ref[...]
        dy_ref[...] = diff * (1.0 / D)
        sq = (diff * diff) * (0.5 / D)
        part = jnp.sum(sq.reshape(tm // 8, 8, D), axis=0)

        @pl.when(pl.program_id(0) == 0)
        def _():
            l_ref[...] = part

        @pl.when(pl.program_id(0) > 0)
        def _():
            l_ref[...] += part

    row = pl.BlockSpec((tm, D), lambda i: (i, 0))
    return pl.pallas_call(
        body, name=name, grid=(S // tm,), in_specs=[row, row],
        out_specs=[row, pl.BlockSpec((8, D), lambda i: (0, 0))],
        out_shape=[jax.ShapeDtypeStruct((S, D), F32), jax.ShapeDtypeStruct((8, D), F32)],
        compiler_params=_cparams(("arbitrary",)),
    )(y, target)


def _layer_fwd(x0, mem, g, w, lb, hg_gain, bias, tag, comm=None):
    S = x0.shape[0]
    h0 = _rms_fwd(x0, g[0], f"rms0_{tag}")
    proj = _mm(h0, w["w_in"], "nn", F32, f"mm_in_{tag}")
    state = None
    cat = lax.empty((S, D_MODEL), BF16)
    for bi, d in enumerate(DILATIONS):
        last = bi == len(DILATIONS) - 1
        out = _attn_fwd(proj, bias[bi], state, d, last, cat if last else None, f"attn_fwd{d}_{tag}")
        if last:
            cat, lse = out
        else:
            state = tuple(out)
    comm = comm or {}
    cat, states, hg_a, hg_o, *carried_a = _hgrn_fwd(proj, lb, hg_gain, cat, f"hgrn_fwd_{tag}", comm.get("hgrn"))
    mix = _mm(cat, w["w_out"], "nn", F32, f"mm_out_{tag}")
    x1, hc = _norm_residual(x0, mix, g[1], g[2], f"res1_{tag}")
    cq = _mm(hc, w["w_cq"], "nn", BF16, f"mm_cq_{tag}")
    mn = _rms_fwd(mem, g[3], f"rms3_{tag}")
    ckv = _mm(mn, w["w_ckv"], "nn", BF16, f"mm_ckv_{tag}")
    cop = _cross_fwd(cq, ckv, f"cross_fwd_{tag}")
    co = _mm(cop, w["w_co"], "nn", F32, f"mm_co_{tag}")
    x2, hf = _norm_residual(x1, co, g[4], g[5], f"res2_{tag}")
    gate, up, act, *carried_b = _mm_gate_up(hf, w["w_gate_up"], f"mm_gu_{tag}", comm.get("gate_up"))
    carried = dict(hgrn=carried_a, gate_up=carried_b)
    y = _mm(act, w["w_down"], "nn", F32, f"mm_down_{tag}")
    x3 = _norm_residual(x2, y, g[6], None, f"res3_{tag}")
    saved = dict(x0=x0, h0=h0, proj=proj, cat=cat, lse=lse, states=states, hg_a=hg_a, hg_o=hg_o, mix=mix, x1=x1, hc=hc, cq=cq, mn=mn,
                 ckv=ckv, cop=cop, co=co, x2=x2, hf=hf, gate=gate, up=up, act=act, y=y)
    return x3, saved, carried


def _layer_bwd(dx3, sv, mem, g, w, lb, hg_gain, bias, tag, comm_fn=None):
    dg = [None] * N_NORMS
    gw = {}
    dy, dg[6] = _rms_bwd(sv["y"], g[6], dx3, None, BF16, f"rmsb6_{tag}")
    dgu = list(_mm_down_x(dy, w["w_down"], sv["gate"], sv["up"], f"mmb_down_x_{tag}"))
    gw["w_down"] = _mm(sv["act"], dy, "tn", BF16, f"mmb_down_w_{tag}")
    gw["w_gate_up"] = _mm(sv["hf"], dgu, "tn", BF16, f"mmb_gu_w_{tag}")
    d_hf = _mm(dgu, w["w_gate_up"], "nt", F32, f"mmb_gu_x_{tag}")
    dx2, dg[5] = _rms_bwd(sv["x2"], g[5], d_hf, dx3, F32, f"rmsb5_{tag}")
    d_co, dg[4] = _rms_bwd(sv["co"], g[4], dx2, None, BF16, f"rmsb4_{tag}")
    d_cop = _mm(d_co, w["w_co"], "nt", BF16, f"mmb_co_x_{tag}")
    gw["w_co"] = _mm(sv["cop"], d_co, "tn", BF16, f"mmb_co_w_{tag}")
    d_cq, d_ckv = _cross_bwd(sv["cq"], sv["ckv"], d_cop, f"cross_bwd_{tag}")
    gw["w_cq"] = _mm(sv["hc"], d_cq, "tn", BF16, f"mmb_cq_w_{tag}")
    d_hc = _mm(d_cq, w["w_cq"], "nt", F32, f"mmb_cq_x_{tag}")
    gw["w_ckv"] = _mm(sv["mn"], d_ckv, "tn", BF16, f"mmb_ckv_w_{tag}")
    d_mn = _mm(d_ckv, w["w_ckv"], "nt", F32, f"mmb_ckv_x_{tag}")
    _, dg[3] = _rms_bwd(mem, g[3], d_mn, None, BF16, f"rmsb3_{tag}")
    dx1, dg[2] = _rms_bwd(sv["x1"], g[2], d_hc, dx2, F32, f"rmsb2_{tag}")
    d_mix, dg[1] = _rms_bwd(sv["mix"], g[1], dx1, None, BF16, f"rmsb1_{tag}")
    d_cat = _mm(d_mix, w["w_out"], "nt", BF16, f"mmb_out_x_{tag}")
    gw["w_out"] = _mm(sv["cat"], d_mix, "tn", BF16, f"mmb_out_w_{tag}")
    acc = None
    dbias = []
    for bi, d in enumerate(DILATIONS):
        last = bi == len(DILATIONS) - 1
        dq, dk, dv, db = _attn_bwd(sv["proj"], bias[bi], sv["cat"], d_cat, sv["lse"], acc, d, last,
                                   f"attn_bwd{d}_{tag}")
        acc = (dq, dk, dv)
        dbias.append(db)
    comm = comm_fn(dict(gw)) if comm_fn else None
    d_hz, dlb, dgain, *carried = _hgrn_bwd(sv["proj"], lb, hg_gain, sv["states"], sv["hg_a"], sv["hg_o"], d_cat,
                                           f"hgrn_bwd_{tag}", comm)
    parts = [acc[0], acc[1], acc[2], d_hz]
    gw["w_in"] = _mm(sv["h0"], parts, "tn", BF16, f"mmb_in_w_{tag}")
    d_h0 = _mm(parts, w["w_in"], "nt", F32, f"mmb_in_x_{tag}")
    dx0, dg[0] = _rms_bwd(sv["x0"], g[0], d_h0, dx1, F32, f"rmsb0_{tag}")
    return dx0, gw, jnp.stack(dg), dlb, dgain, jnp.stack(dbias), carried


def _lb_all(lb_logits):
    p = jax.nn.softmax(lb_logits.astype(F32), axis=0)
    return jnp.cumsum(p, axis=0) - p


def _local_step(x, mem, target, rel_bias, lb_logits, norm_gains, hg_norm, weights_of_layer, hooks=None):
    hooks = hooks or {}
    L = lb_logits.shape[0]
    bias, bias_vjp = jax.vjp(_bias_tables, rel_bias)
    lb_all, lb_vjp = jax.vjp(_lb_all, lb_logits)
    gains = norm_gains.reshape(L, N_NORMS, 1, D_MODEL)
    saved = []
    h = x
    ws = []
    for l in range(L):
        w = weights_of_layer(l)
        ws.append(w)
        comm = hooks["fwd_comm"](l) if "fwd_comm" in hooks else None
        h, sv, carried = _layer_fwd(h, mem, gains[l], w, lb_all[l:l + 1], hg_norm[l:l + 1], bias, f"l{l}", comm)
        if comm:
            hooks["fwd_done"](l, carried)
        saved.append(sv)
    dy, lparts = _loss_head(h, target, "loss_head")
    loss = jnp.sum(lparts)
    d_gains, d_lb, d_hg, gws = [None] * L, [None] * L, [None] * L, [None] * L
    d_bias = jnp.zeros_like(bias)
    dh = dy
    for l in reversed(range(L)):
        comm_fn = functools.partial(hooks["bwd_comm"], l) if "bwd_comm" in hooks else None
        dh, gw, dgl, dlbl, dhgl, dbl, carried = _layer_bwd(dh, saved[l], mem, gains[l], ws[l], lb_all[l:l + 1],
                                                           hg_norm[l:l + 1], bias, f"l{l}", comm_fn)
        if comm_fn:
            hooks["bwd_done"](l, carried)
        d_gains[l], d_lb[l], d_hg[l], gws[l] = dgl.reshape(N_NORMS, D_MODEL), dlbl[0], dhgl[0], gw
        d_bias = d_bias + dbl
        if "layer_grads" in hooks:
            gws[l] = hooks["layer_grads"](l, gw)
    (d_rel_bias,) = bias_vjp(d_bias)
    (d_lb_logits,) = lb_vjp(jnp.stack(d_lb))
    return loss, dh, gws, d_rel_bias, d_lb_logits, jnp.stack(d_gains), jnp.stack(d_hg)


def kernel(x, mem, rel_bias, lb_logits, norm_gains, w_in, hg_norm, w_out, w_cq, w_ckv, w_co, w_gate_up, w_down, loss_target, m_rel_bias, m_lb_logits, m_norm_gains, m_w_in, m_hg_norm, m_w_out, m_w_cq, m_w_ckv, m_w_co, m_w_gate_up, m_w_down, v_rel_bias, v_lb_logits, v_norm_gains, v_w_in, v_hg_norm, v_w_out, v_w_cq, v_w_ckv, v_w_co, v_w_gate_up, v_w_down):
    return _train_step(
        x, mem, loss_target,
        dict(rel_bias=rel_bias, lb_logits=lb_logits, norm_gains=norm_gains, hg_norm=hg_norm, w_in=w_in, w_out=w_out,
             w_cq=w_cq, w_ckv=w_ckv, w_co=w_co, w_gate_up=w_gate_up, w_down=w_down),
        dict(rel_bias=m_rel_bias, lb_logits=m_lb_logits, norm_gains=m_norm_gains, hg_norm=m_hg_norm, w_in=m_w_in,
             w_out=m_w_out, w_cq=m_w_cq, w_ckv=m_w_ckv, w_co=m_w_co, w_gate_up=m_w_gate_up, w_down=m_w_down),
        dict(rel_bias=v_rel_bias, lb_logits=v_lb_logits, norm_gains=v_norm_gains, hg_norm=v_hg_norm, w_in=v_w_in,
             w_out=v_w_out, w_cq=v_w_cq, w_ckv=v_w_ckv, w_co=v_w_co, w_gate_up=v_w_gate_up, w_down=v_w_down))


BIG = ("w_in", "w_out", "w_cq", "w_ckv", "w_co", "w_gate_up", "w_down")
SHARD_AXIS = dict(w_in=1, w_out=0, w_cq=0, w_ckv=1, w_co=0, w_gate_up=1, w_down=0)
NB = len(BIG)
AG_WITH_HGRN = ("w_in", "w_out", "w_cq", "w_ckv", "w_co")
AG_WITH_FFN = ("w_gate_up", "w_down")
OUT_ORDER = ("rel_bias", "lb_logits", "norm_gains", "w_in", "hg_norm", "w_out", "w_cq", "w_ckv", "w_co",
             "w_gate_up", "w_down")
ANY = pl.BlockSpec(memory_space=pl.ANY)


def _place():
    x, y, c = lax.axis_index("x"), lax.axis_index("y"), lax.axis_index("c")
    chips = [(1 - x, y), (x, 1 - y), (1 - x, 1 - y)]
    return x, y, c, chips


def _remote(src, dst, send_sem, recv_sem, to):
    return pltpu.make_async_remote_copy(src_ref=src, dst_ref=dst, send_sem=send_sem, recv_sem=recv_sem,
                                        device_id=to, device_id_type=MESH_ID)


def _half_region(ref, axis, chip, half, lead=()):
    R, C = ref.shape[-2:]
    if axis == 0:
        rs = R // N_CHIPS
        return ref.at[(*lead, pl.ds(chip * rs + half * (rs // 2), rs // 2), slice(None))]
    cs = C // N_CHIPS
    return ref.at[(*lead, pl.ds(half * (R // 2), R // 2), pl.ds(chip * cs, cs))]


def _cast_place(n, w, l, where, name):
    _, rs, cs = w.shape
    tr = _blk(rs, (256, 128, 64, 32, 16))
    nt = rs // tr
    if SHARD_AXIS[n] == 0:
        full = (rs * N_CHIPS, cs)
        o_spec = pl.BlockSpec((tr, cs), lambda i, wh: (wh[0] * nt + i, 0))
    else:
        full = (rs, cs * N_CHIPS)
        o_spec = pl.BlockSpec((tr, cs), lambda i, wh: (i, wh[0]))

    def body(wh, w_ref, o_ref):
        o_ref[...] = w_ref[...].astype(BF16)

    return pl.pallas_call(
        body, name=name,
        grid_spec=pltpu.PrefetchScalarGridSpec(
            num_scalar_prefetch=1, grid=(nt,),
            in_specs=[pl.BlockSpec((None, tr, cs), lambda i, wh: (l, i, 0))], out_specs=o_spec),
        out_shape=jax.ShapeDtypeStruct(full, BF16), compiler_params=_cparams(("parallel",)),
    )(where, w)


def _allgather_weights(placed, name):
    names = list(BIG)
    nb = NB

    def body(*refs):
        outs = refs[NB:2 * NB]
        send, recv = refs[2 * NB:]
        x, y, c, chips = _place()
        me = 2 * x + y
        sib = (x, y, 1 - c)
        sends = []
        for j, chip in enumerate(chips):
            for wi, n in enumerate(names):
                reg = _half_region(outs[wi], SHARD_AXIS[n], me, c)
                cp = _remote(reg, reg, send.at[j * nb + wi], recv.at[j * nb + wi], (*chip, c))
                cp.start()
                sends.append(cp)
        for j, chip in enumerate(chips):
            them = 2 * chip[0] + chip[1]
            for wi, n in enumerate(names):
                reg = _half_region(outs[wi], SHARD_AXIS[n], them, c)
                _remote(reg, reg, send.at[j * nb + wi], recv.at[j * nb + wi], (*chip, c)).wait_recv()
                cp = _remote(reg, reg, send.at[(3 + j) * nb + wi], recv.at[(3 + j) * nb + wi], sib)
                cp.start()
                sends.append(cp)
        for j, chip in enumerate(chips):
            them = 2 * chip[0] + chip[1]
            for wi, n in enumerate(names):
                reg = _half_region(outs[wi], SHARD_AXIS[n], them, 1 - c)
                _remote(reg, reg, send.at[(3 + j) * nb + wi], recv.at[(3 + j) * nb + wi], sib).wait_recv()
        for cp in sends:
            cp.wait_send()

    outs = pl.pallas_call(
        body, name=name, in_specs=[ANY] * NB, out_specs=[ANY] * NB,
        out_shape=[jax.ShapeDtypeStruct(placed[n].shape, BF16) for n in names],
        scratch_shapes=[pltpu.SemaphoreType.DMA((6 * NB,)), pltpu.SemaphoreType.DMA((6 * NB,))],
        input_output_aliases={wi: wi for wi in range(NB)},
    )(*[placed[n] for n in names])
    return dict(zip(names, outs))


def _comm_spec(ins, out_shapes, aliases, nsem, start, wait):
    return dict(ins=list(ins), out_shapes=list(out_shapes), aliases=dict(aliases), nsem=nsem, start=start, wait=wait)


def _run_comm(spec, name):
    ni, no = len(spec["ins"]), len(spec["out_shapes"])

    def body(*refs):
        ins, outs = refs[:ni], refs[ni:ni + no]
        send, recv = refs[ni + no:]
        spec["start"](ins, outs, send, recv)
        spec["wait"](ins, outs, send, recv)

    return pl.pallas_call(
        body, name=name, in_specs=[ANY] * ni, out_specs=[ANY] * no, out_shape=spec["out_shapes"],
        scratch_shapes=[pltpu.SemaphoreType.DMA((spec["nsem"],)), pltpu.SemaphoreType.DMA((spec["nsem"],))],
        input_output_aliases=spec["aliases"],
    )(*spec["ins"])


def _ag_ici_spec(placed):
    names = list(placed)
    nb = len(names)

    def start(ins, outs, send, recv):
        x, y, c, chips = _place()
        me = 2 * x + y
        for j, chip in enumerate(chips):
            for wi, n in enumerate(names):
                reg = _half_region(outs[wi], SHARD_AXIS[n], me, c)
                _remote(reg, reg, send.at[j * nb + wi], recv.at[j * nb + wi], (*chip, c)).start()

    def wait(ins, outs, send, recv):
        x, y, c, chips = _place()
        me = 2 * x + y
        for j, chip in enumerate(chips):
            them = 2 * chip[0] + chip[1]
            for wi, n in enumerate(names):
                reg = _half_region(outs[wi], SHARD_AXIS[n], them, c)
                _remote(reg, reg, send.at[j * nb + wi], recv.at[j * nb + wi], (*chip, c)).wait_recv()
        for j, chip in enumerate(chips):
            for wi, n in enumerate(names):
                reg = _half_region(outs[wi], SHARD_AXIS[n], me, c)
                _remote(reg, reg, send.at[j * nb + wi], recv.at[j * nb + wi], (*chip, c)).wait_send()

    return _comm_spec([placed[n] for n in names], [jax.ShapeDtypeStruct(placed[n].shape, BF16) for n in names],
                      {wi: wi for wi in range(nb)}, 3 * nb, start, wait)


def _ag_sibling_spec(arrays):
    names = list(arrays)
    nb = len(names)

    def start(ins, outs, send, recv):
        x, y, c, chips = _place()
        for j, chip in enumerate(chips):
            them = 2 * chip[0] + chip[1]
            for wi, n in enumerate(names):
                reg = _half_region(outs[wi], SHARD_AXIS[n], them, c)
                _remote(reg, reg, send.at[j * nb + wi], recv.at[j * nb + wi], (x, y, 1 - c)).start()

    def wait(ins, outs, send, recv):
        x, y, c, chips = _place()
        for j, chip in enumerate(chips):
            them = 2 * chip[0] + chip[1]
            for wi, n in enumerate(names):
                reg = _half_region(outs[wi], SHARD_AXIS[n], them, 1 - c)
                _remote(reg, reg, send.at[j * nb + wi], recv.at[j * nb + wi], (x, y, 1 - c)).wait_recv()
        for j, chip in enumerate(chips):
            them = 2 * chip[0] + chip[1]
            for wi, n in enumerate(names):
                reg = _half_region(outs[wi], SHARD_AXIS[n], them, c)
                _remote(reg, reg, send.at[j * nb + wi], recv.at[j * nb + wi], (x, y, 1 - c)).wait_send()

    return _comm_spec([arrays[n] for n in names], [jax.ShapeDtypeStruct(arrays[n].shape, BF16) for n in names],
                      {wi: wi for wi in range(nb)}, 3 * nb, start, wait)


def _allgather_gains(g_shard, name):
    A, Cs = g_shard.shape

    def body(in_ref, out_ref, send, recv, local):
        x, y, c, chips = _place()
        me = 2 * x + y
        mine = pltpu.make_async_copy(in_ref, out_ref.at[me], local)
        mine.start()
        cps = [_remote(in_ref, out_ref.at[me], send.at[j], recv.at[j], (*chip, c)) for j, chip in enumerate(chips)]
        for cp in cps:
            cp.start()
        for j, chip in enumerate(chips):
            them = 2 * chip[0] + chip[1]
            _remote(in_ref, out_ref.at[them], send.at[j], recv.at[j], (*chip, c)).wait_recv()
        for cp in cps:
            cp.wait_send()
        mine.wait()

    return pl.pallas_call(
        body, name=name, in_specs=[ANY], out_specs=ANY,
        out_shape=jax.ShapeDtypeStruct((N_CHIPS, A, Cs), F32),
        scratch_shapes=[pltpu.SemaphoreType.DMA((3,)), pltpu.SemaphoreType.DMA((3,)), pltpu.SemaphoreType.DMA],
    )(g_shard)


def _half_shape(n, shape):
    R, C = shape
    if SHARD_AXIS[n] == 0:
        return (N_CHIPS, R // N_CHIPS // 2, C)
    return (R // 2, C)


def _as_halves(n, g):
    R, C = g.shape
    if SHARD_AXIS[n] == 0:
        return g.reshape(N_CHIPS, R // N_CHIPS, C)
    return g


def _my_half(n, ref, half):
    if SHARD_AXIS[n] == 0:
        hs = ref.shape[1] // 2
        return ref.at[:, pl.ds(half * hs, hs), :]
    hs = ref.shape[0] // 2
    return ref.at[pl.ds(half * hs, hs), :]


def _swap_sibling_halves(gw, name):
    names = list(gw)
    nb = len(names)

    def body(*refs):
        ins = refs[:nb]
        outs = refs[nb:2 * nb]
        send, recv = refs[2 * nb:]
        x, y, c, _ = _place()
        sib = (x, y, 1 - c)
        cps = []
        for wi, n in enumerate(names):
            cp = _remote(_my_half(n, ins[wi], 1 - c), outs[wi], send.at[wi], recv.at[wi], sib)
            cp.start()
            cps.append(cp)
        for cp in cps:
            cp.wait()

    shapes = []
    for n in names:
        g = gw[n]
        shapes.append((g.shape[0], g.shape[1] // 2, g.shape[2]) if g.ndim == 3 else (g.shape[0] // 2, g.shape[1]))
    outs = pl.pallas_call(
        body, name=name, in_specs=[ANY] * nb, out_specs=[ANY] * nb,
        out_shape=[jax.ShapeDtypeStruct(s, BF16) for s in shapes],
        scratch_shapes=[pltpu.SemaphoreType.DMA((nb,)), pltpu.SemaphoreType.DMA((nb,))],
    )(*[gw[n] for n in names])
    return dict(zip(names, outs))


def _pair_sum(n, g, other, c_arr, name):
    if g.ndim == 3:
        nc, rs, C = g.shape
        hs = rs // 2
        tr = _blk(hs, (256, 128, 64, 32, 16))
        nt = hs // tr
        grid = (nc, nt)
        g_spec = pl.BlockSpec((1, tr, C), lambda k, i, c_ref: (k, c_ref[0] * nt + i, 0))
        o_spec = pl.BlockSpec((1, tr, C), lambda k, i, c_ref: (k, i, 0))
    else:
        R, C = g.shape
        hs = R // 2
        cs = C // N_CHIPS
        tr = _blk(hs, (256, 128, 64, 32, 16))
        nt = hs // tr
        grid = (nt, N_CHIPS)
        g_spec = pl.BlockSpec((tr, cs), lambda i, k, c_ref: (c_ref[0] * nt + i, k))
        o_spec = pl.BlockSpec((tr, cs), lambda i, k, c_ref: (i, k))

    def body(c_ref, g_ref, o_ref, out_ref):
        out_ref[...] = (g_ref[...].astype(F32) + o_ref[...].astype(F32)).astype(BF16)

    return pl.pallas_call(
        body, name=name,
        grid_spec=pltpu.PrefetchScalarGridSpec(num_scalar_prefetch=1, grid=grid, in_specs=[g_spec, o_spec],
                                               out_specs=o_spec),
        out_shape=jax.ShapeDtypeStruct(other.shape, BF16),
        compiler_params=_cparams(("parallel", "parallel")),
    )(c_arr, g, other)


def _scatter_spec(pairs):
    names = list(pairs)
    nb = len(names)

    def piece(n, ref, chip):
        if SHARD_AXIS[n] == 0:
            return ref.at[chip]
        cs = ref.shape[1] // N_CHIPS
        return ref.at[:, pl.ds(chip * cs, cs)]

    def start(ins, outs, send, recv):
        x, y, c, chips = _place()
        for j, chip in enumerate(chips):
            them = 2 * chip[0] + chip[1]
            for wi, n in enumerate(names):
                _remote(piece(n, ins[wi], them), outs[wi].at[j], send.at[j * nb + wi], recv.at[j * nb + wi],
                        (*chip, c)).start()

    def wait(ins, outs, send, recv):
        x, y, c, chips = _place()
        me = 2 * x + y
        for j, chip in enumerate(chips):
            for wi, n in enumerate(names):
                _remote(piece(n, ins[wi], me), outs[wi].at[j], send.at[j * nb + wi], recv.at[j * nb + wi],
                        (*chip, c)).wait_recv()
        for j, chip in enumerate(chips):
            them = 2 * chip[0] + chip[1]
            for wi, n in enumerate(names):
                _remote(piece(n, ins[wi], them), outs[wi].at[j], send.at[j * nb + wi], recv.at[j * nb + wi],
                        (*chip, c)).wait_send()

    shapes = []
    for n in names:
        p = pairs[n]
        shapes.append((3,) + (p.shape[1:] if p.ndim == 3 else (p.shape[0], p.shape[1] // N_CHIPS)))
    return _comm_spec([pairs[n] for n in names], [jax.ShapeDtypeStruct(s, BF16) for s in shapes], {}, 3 * nb,
                      start, wait)


def _sum_final(n, pair, q, stack, l, where, name):
    hs, cc = q.shape[1:]
    tr = _blk(hs, (256, 128, 64, 32, 16))
    nt = hs // tr
    if SHARD_AXIS[n] == 0:
        p_spec = pl.BlockSpec((None, tr, cc), lambda i, wh: (wh[0], i, 0))
    else:
        p_spec = pl.BlockSpec((tr, cc), lambda i, wh: (i, wh[0]))

    def body(wh, p_ref, q_ref, stack_any, o_ref):
        o_ref[...] = ((p_ref[...].astype(F32) + q_ref[0].astype(F32)) + q_ref[1].astype(F32)) + q_ref[2].astype(F32)

    return pl.pallas_call(
        body, name=name,
        grid_spec=pltpu.PrefetchScalarGridSpec(
            num_scalar_prefetch=1, grid=(nt,),
            in_specs=[p_spec, pl.BlockSpec((3, tr, cc), lambda i, wh: (0, i, 0)), ANY],
            out_specs=pl.BlockSpec((None, tr, cc), lambda i, wh: (l, wh[1] * nt + i, 0))),
        out_shape=jax.ShapeDtypeStruct(stack.shape, F32),
        input_output_aliases={3: 0},
        compiler_params=_cparams(("parallel",)),
    )(where, pair, q, stack)


def _sum_slots(q, name):
    K, A, B = q.shape
    tr = _blk(A, (256, 128, 64, 32, 16, 8))

    def body(q_ref, o_ref):
        s = q_ref[0].astype(F32)
        for k in range(1, K):
            s = s + q_ref[k].astype(F32)
        o_ref[...] = s

    return pl.pallas_call(
        body, name=name, grid=(A // tr,),
        in_specs=[pl.BlockSpec((K, tr, B), lambda i: (0, i, 0))], out_specs=pl.BlockSpec((tr, B), lambda i: (i, 0)),
        out_shape=jax.ShapeDtypeStruct((A, B), F32), compiler_params=_cparams(("parallel",)),
    )(q)


def _finish_grads(stacks, small, name):
    names = list(BIG)
    A = small.shape[0]
    nh = NB

    def body(*refs):
        small_ref = refs[nh]
        outs = refs[nh + 1:nh + 1 + NB]
        slots = refs[nh + 1 + NB]
        send, recv, local = refs[nh + 2 + NB:]
        x, y, c, _ = _place()
        sib = (x, y, 1 - c)
        me = 4 * x + 2 * y + c
        locals_, sends = [], []
        for wi, n in enumerate(names):
            hr = outs[wi].shape[1] // 2
            reg = outs[wi].at[:, pl.ds(c * hr, hr), :]
            cp = _remote(reg, reg, send.at[wi], recv.at[wi], sib)
            cp.start()
            sends.append(cp)
        cp = pltpu.make_async_copy(small_ref, slots.at[me], local)
        cp.start()
        locals_.append(cp)
        peers = []
        for dx in range(2):
            for dy in range(2):
                for dc in range(2):
                    if dx or dy or dc:
                        peers.append((dx, dy, dc))
        for j, (dx, dy, dc) in enumerate(peers):
            to = (jnp.bitwise_xor(x, dx), jnp.bitwise_xor(y, dy), jnp.bitwise_xor(c, dc))
            cp = _remote(small_ref, slots.at[me], send.at[nh + j], recv.at[nh + j], to)
            cp.start()
            sends.append(cp)
        for wi, n in enumerate(names):
            hr = outs[wi].shape[1] // 2
            reg = outs[wi].at[:, pl.ds((1 - c) * hr, hr), :]
            _remote(reg, reg, send.at[wi], recv.at[wi], sib).wait_recv()
        for j, (dx, dy, dc) in enumerate(peers):
            frm = 4 * jnp.bitwise_xor(x, dx) + 2 * jnp.bitwise_xor(y, dy) + jnp.bitwise_xor(c, dc)
            _remote(small_ref, slots.at[frm], send.at[nh + j], recv.at[nh + j], sib).wait_recv()
        for cp in sends:
            cp.wait_send()
        for cp in locals_:
            cp.wait()

    res = pl.pallas_call(
        body, name=name, in_specs=[ANY] * (nh + 1), out_specs=[ANY] * (NB + 1),
        out_shape=[jax.ShapeDtypeStruct(stacks[n].shape, F32) for n in names]
        + [jax.ShapeDtypeStruct((8, A, LANES), F32)],
        scratch_shapes=[pltpu.SemaphoreType.DMA((nh + 7,)), pltpu.SemaphoreType.DMA((nh + 7,)),
                        pltpu.SemaphoreType.DMA],
        input_output_aliases={wi: wi for wi in range(NB)},
    )(*[stacks[n] for n in names], small)
    return dict(zip(names, res[:NB])), res[NB]


def _adamw(w, g, m, v, name):
    shape = w.shape
    if w.ndim == 2:
        w, g, m, v = (t.reshape((1,) + shape) for t in (w, g, m, v))
    L, A, B = w.shape
    tr = _blk(A, (128, 64, 32, 16, 8))
    c1 = 1.0 - ADAM_B1 ** ADAM_STEP
    c2 = 1.0 - ADAM_B2 ** ADAM_STEP

    def body(w_ref, g_ref, m_ref, v_ref, d_ref, nm_ref, nv_ref):
        gv = g_ref[...]
        nm = ADAM_B1 * m_ref[...] + (1.0 - ADAM_B1) * gv
        nv = ADAM_B2 * v_ref[...] + (1.0 - ADAM_B2) * (gv * gv)
        m_hat = nm / c1
        v_hat = nv / c2
        d_ref[...] = -ADAM_LR * (m_hat / (jnp.sqrt(v_hat) + ADAM_EPS) + ADAM_WD * w_ref[...])
        nm_ref[...] = nm
        nv_ref[...] = nv

    spec = pl.BlockSpec((1, tr, B), lambda l, i: (l, i, 0))
    sds = jax.ShapeDtypeStruct((L, A, B), F32)
    outs = pl.pallas_call(
        body, name=name, grid=(L, A // tr), in_specs=[spec] * 4, out_specs=[spec] * 3, out_shape=[sds] * 3,
        compiler_params=_cparams(("parallel", "parallel")),
    )(w, g, m, v)
    return tuple(o.reshape(shape) for o in outs)


SMALL_ROWS = 520


def _train_step(x, mem, target, w, m, v):
    L = w["lb_logits"].shape[0]
    cx, cy, cc = lax.axis_index("x"), lax.axis_index("y"), lax.axis_index("c")
    me = 2 * cx + cy
    c_arr = jnp.reshape(cc, (1,)).astype(jnp.int32)
    where = jnp.stack([me, cc]).astype(jnp.int32)

    placed = [{n: _cast_place(n, w[n], l, where, f"cast_{n}_l{l}") for n in BIG} for l in range(L)]
    full = [None] * L
    full[0] = _allgather_weights(placed[0], "allgather_l0")
    gs = _allgather_gains(w["norm_gains"].reshape(L * N_NORMS, -1), "allgather_gains")
    gains = jnp.transpose(gs, (1, 0, 2)).reshape(L, N_NORMS, D_MODEL)

    def fwd_comm(l):
        if l + 1 == L:
            return None
        return dict(hgrn=_ag_ici_spec({n: placed[l + 1][n] for n in AG_WITH_HGRN}),
                    gate_up=_ag_ici_spec({n: placed[l + 1][n] for n in AG_WITH_FFN}))

    def fwd_done(l, outs):
        arrived = dict(zip(AG_WITH_HGRN, outs["hgrn"]))
        arrived.update(zip(AG_WITH_FFN, outs["gate_up"]))
        arrived = {n: arrived[n] for n in BIG}
        full[l + 1] = dict(zip(BIG, _run_comm(_ag_sibling_spec(arrived), f"allgather_sibling_l{l + 1}")))

    stacks = {n: lax.empty(w[n].shape, F32) for n in BIG}
    late = {}
    in_flight = {}

    def pair_sums(l, gw, tag):
        views = {n: _as_halves(n, g) for n, g in gw.items()}
        theirs = _swap_sibling_halves(views, f"swap_halves_{tag}_l{l}")
        return {n: _pair_sum(n, views[n], theirs[n], c_arr, f"pair_sum_{n}_l{l}") for n in gw}

    def finish(l, n, pair, slot):
        stacks[n] = _sum_final(n, pair, slot, stacks[n], l, where, f"sum_chips_{n}_l{l}")

    def bwd_comm(l, gw_ready):
        pairs = pair_sums(l, gw_ready, "early")
        owners = [(l, n) for n in pairs]
        if l + 1 in late:
            pairs["w_in"] = late.pop(l + 1)
            owners.append((l + 1, "w_in"))
        in_flight[l] = (owners, pairs)
        return _scatter_spec(pairs)

    def bwd_done(l, outs):
        owners, pairs = in_flight.pop(l)
        for (lo, n), slot in zip(owners, outs):
            finish(lo, n, pairs[n], slot)

    def layer_grads(l, gw):
        pair = pair_sums(l, {"w_in": gw["w_in"]}, "late")["w_in"]
        if l == 0:
            (slot,) = _run_comm(_scatter_spec({"w_in": pair}), "scatter_l0")
            finish(0, "w_in", pair, slot)
        else:
            late[l] = pair
        return None

    loss, dx, _, d_rb, d_lb, d_gains, d_hg = _local_step(
        x[0], mem[0], target[0], w["rel_bias"], w["lb_logits"], gains, w["hg_norm"], lambda l: full[l],
        dict(fwd_comm=fwd_comm, fwd_done=fwd_done, bwd_comm=bwd_comm, bwd_done=bwd_done, layer_grads=layer_grads))

    flat = jnp.concatenate([d_rb.reshape(-1), d_lb.reshape(-1), d_hg.reshape(-1), d_gains.reshape(-1)])
    small = jnp.pad(flat, (0, SMALL_ROWS * LANES - flat.shape[0])).reshape(SMALL_ROWS, LANES)
    grads, slots = _finish_grads(stacks, small, "finish_grads")
    tot = _sum_slots(slots, "sum_small").reshape(-1)
    n_rb, n_lb = d_rb.size, d_lb.size
    grads["rel_bias"] = tot[:n_rb].reshape(d_rb.shape)
    grads["lb_logits"] = tot[n_rb:n_rb + n_lb].reshape(d_lb.shape)
    grads["hg_norm"] = tot[n_rb + n_lb:n_rb + 2 * n_lb].reshape(d_hg.shape)
    g_full = tot[n_rb + 2 * n_lb:n_rb + 2 * n_lb + d_gains.size].reshape(d_gains.shape)
    cs = D_MODEL // N_CHIPS
    grads["norm_gains"] = lax.dynamic_slice_in_dim(g_full, me * cs, cs, axis=2)

    delta, new_m, new_v = {}, {}, {}
    for n in OUT_ORDER:
        delta[n], new_m[n], new_v[n] = _adamw(w[n], grads[n], m[n], v[n], f"adamw_{n}")
    loss = lax.psum(loss, ("x", "y", "c"))
    return (loss, dx[None], *[grads[n] for n in OUT_ORDER], *[delta[n] for n in OUT_ORDER],
            *[new_m[n] for n in OUT_ORDER], *[new_v[n] for n in OUT_ORDER])
```

```python
import functools
import math

import numpy as np
import jax
import jax.numpy as jnp
from jax import lax
from jax.experimental import pallas as pl
from jax.experimental.pallas import tpu as pltpu

F32 = jnp.float32
BF16 = jnp.bfloat16
MESH_ID = pl.DeviceIdType.MESH

D_MODEL = 2048
HEAD_DIM = 128
N_HEADS = 8
ATTN_W = 1024
HG_W = 1024
HG_CHUNK = 64
Q_BLOCK = 128
DILATIONS = (16, 4, 1)
REL_BUCKETS = 32
REL_MAX_DIST = 2048
CROSS_HEADS = 4
CROSS_DIM = 512
D_FF = 5632
RMS_EPS = 1e-6
NEG_INF = -1e30
N_NORMS = 7

ADAM_LR = 0.001
ADAM_B1 = 0.9
ADAM_B2 = 0.999
ADAM_EPS = 1e-08
ADAM_WD = 0.01
ADAM_STEP = 10

VMEM_LIMIT_V7X = 56 * 1024 * 1024
LANES = 128
N_CHIPS = 4


def _cparams(sem=None):
    if sem is None:
        return pltpu.CompilerParams(vmem_limit_bytes=VMEM_LIMIT_V7X)
    return pltpu.CompilerParams(dimension_semantics=sem, vmem_limit_bytes=VMEM_LIMIT_V7X)


def _blk(n, prefs):
    for p in prefs:
        if p <= n and n % p == 0:
            return p
    return n


def _dot(a, b, dims):
    return lax.dot_general(a.astype(BF16), b.astype(BF16), (dims, ((), ())), preferred_element_type=F32)


def _dot_nn(a, b):
    return _dot(a, b, ((1,), (0,)))


def _dot_nt(a, b):
    return _dot(a, b, ((1,), (1,)))


def _dot_tn(a, b):
    return _dot(a, b, ((0,), (0,)))


def _sigmoid(x):
    return 1.0 / (1.0 + jnp.exp(-x))


def _mm(a_list, b_list, mode, out_dtype, name):
    if not isinstance(a_list, (list, tuple)):
        a_list = [a_list]
    if not isinstance(b_list, (list, tuple)):
        b_list = [b_list]
    na, nb = len(a_list), len(b_list)
    deep = (2048, 2816, 1024, 1408, 512, 256, 128)
    if mode == "nn":
        M, K = a_list[0].shape
        N = b_list[0].shape[1]
        bm, bn, bk = _blk(M, (1024, 512, 256, 128)), _blk(N, (1024, 512, 256, 128)), _blk(K, deep)
    elif mode == "nt":
        M = a_list[0].shape[0]
        K = sum(a.shape[1] for a in a_list)
        N = b_list[0].shape[0]
        bm, bn = _blk(M, (1024, 512, 256, 128)), _blk(N, (1024, 512, 256, 128))
        bk = _blk(math.gcd(*[a.shape[1] for a in a_list]), deep)
    else:
        K, M = a_list[0].shape
        N = sum(b.shape[1] for b in b_list)
        bn = _blk(math.gcd(*[b.shape[1] for b in b_list]), (1024, 2816, 512, 256, 128))
        wide = bn > 1024
        bm = _blk(M, (512, 256, 128) if wide else (1024, 1408, 512, 256, 128))
        bk = _blk(K, (1024, 512, 256, 128) if wide else (2048, 1024, 512, 256, 128))
    nk = K // bk
    grid = (M // bm, N // bn, nk)

    if mode == "nn":
        a_specs = [pl.BlockSpec((bm, bk), lambda i, j, k: (i, k))]
        b_specs = [pl.BlockSpec((bk, bn), lambda i, j, k: (k, j))]
        dims = ((1,), (0,))
    elif mode == "nt":
        a_specs, off = [], 0
        for a in a_list:
            n_i = a.shape[1] // bk
            a_specs.append(pl.BlockSpec((bm, bk), functools.partial(
                lambda i, j, k, off, n_i: (i, jnp.clip(k - off, 0, n_i - 1)), off=off, n_i=n_i)))
            off += n_i
        b_specs = [pl.BlockSpec((bn, bk), lambda i, j, k: (j, k))]
        dims = ((1,), (1,))
    else:
        a_specs = [pl.BlockSpec((bk, bm), lambda i, j, k: (k, i))]
        b_specs, off = [], 0
        for b in b_list:
            n_j = b.shape[1] // bn
            b_specs.append(pl.BlockSpec((bk, bn), functools.partial(
                lambda i, j, k, off, n_j: (jnp.where((j >= off) & (j < off + n_j), k, 0),
                                           jnp.clip(j - off, 0, n_j - 1)), off=off, n_j=n_j)))
            off += n_j
        dims = ((0,), (0,))
    a_bounds = np.cumsum([0] + [a.shape[1] // bk for a in a_list]) if mode == "nt" else None
    b_bounds = np.cumsum([0] + [b.shape[1] // bn for b in b_list]) if mode == "tn" else None

    def body(*refs):
        a_refs = refs[:na]
        b_refs = refs[na:na + nb]
        o_ref = refs[na + nb]
        acc_ref = refs[na + nb + 1] if nk > 1 else None
        j = pl.program_id(1)
        k = pl.program_id(2)

        def accumulate(p):
            if nk == 1:
                o_ref[...] = p.astype(out_dtype)
                return

            @pl.when(k == 0)
            def _():
                acc_ref[...] = p

            @pl.when(k > 0)
            def _():
                acc_ref[...] += p

        if na > 1:
            for t in range(na):
                @pl.when((k >= int(a_bounds[t])) & (k < int(a_bounds[t + 1])))
                def _(t=t):
                    accumulate(_dot(a_refs[t][...], b_refs[0][...], dims))
        elif nb > 1:
            for t in range(nb):
                @pl.when((j >= int(b_bounds[t])) & (j < int(b_bounds[t + 1])))
                def _(t=t):
                    accumulate(_dot(a_refs[0][...], b_refs[t][...], dims))
        else:
            accumulate(_dot(a_refs[0][...], b_refs[0][...], dims))

        if nk > 1:
            @pl.when(k == nk - 1)
            def _():
                o_ref[...] = acc_ref[...].astype(out_dtype)

    return pl.pallas_call(
        body, name=name, grid=grid,
        in_specs=a_specs + b_specs,
        out_specs=pl.BlockSpec((bm, bn), lambda i, j, k: (i, j)),
        out_shape=jax.ShapeDtypeStruct((M, N), out_dtype),
        scratch_shapes=[pltpu.VMEM((bm, bn), F32)] if nk > 1 else [],
        compiler_params=_cparams(("parallel", "parallel", "arbitrary")),
    )(*a_list, *b_list)


def _rms_fwd(x, g, name):
    S, D = x.shape
    tm = _blk(S, (256, 128, 64, 8))

    def body(x_ref, g_ref, o_ref):
        xv = x_ref[...]
        r = lax.rsqrt(jnp.mean(xv * xv, axis=-1, keepdims=True) + RMS_EPS)
        o_ref[...] = ((xv * r) * g_ref[...]).astype(o_ref.dtype)

    return pl.pallas_call(
        body, name=name, grid=(S // tm,),
        in_specs=[pl.BlockSpec((tm, D), lambda i: (i, 0)), pl.BlockSpec((1, D), lambda i: (0, 0))],
        out_specs=pl.BlockSpec((tm, D), lambda i: (i, 0)),
        out_shape=jax.ShapeDtypeStruct((S, D), BF16),
        compiler_params=_cparams(("parallel",)),
    )(x, g)


def _norm_residual(x, t, g, g_next, name):
    S, D = x.shape
    tm = _blk(S, (256, 128, 64, 8))
    chain = g_next is not None

    def body(*refs):
        if chain:
            x_ref, t_ref, g_ref, gn_ref, o_ref, h_ref = refs
        else:
            x_ref, t_ref, g_ref, o_ref = refs
        tv = t_ref[...]
        r = lax.rsqrt(jnp.mean(tv * tv, axis=-1, keepdims=True) + RMS_EPS)
        xn = x_ref[...] + (tv * r) * g_ref[...]
        o_ref[...] = xn
        if chain:
            rn = lax.rsqrt(jnp.mean(xn * xn, axis=-1, keepdims=True) + RMS_EPS)
            h_ref[...] = ((xn * rn) * gn_ref[...]).astype(BF16)

    row = pl.BlockSpec((tm, D), lambda i: (i, 0))
    vec = pl.BlockSpec((1, D), lambda i: (0, 0))
    return pl.pallas_call(
        body, name=name, grid=(S // tm,),
        in_specs=[row, row, vec] + ([vec] if chain else []),
        out_specs=[row, row] if chain else row,
        out_shape=[jax.ShapeDtypeStruct((S, D), F32), jax.ShapeDtypeStruct((S, D), BF16)] if chain
        else jax.ShapeDtypeStruct((S, D), F32),
        compiler_params=_cparams(("parallel",)),
    )(*([x, t, g, g_next] if chain else [x, t, g]))


def _rms_bwd(x, g, dh, res, out_dtype, name):
    S, D = x.shape
    tm = _blk(S, (256, 128, 64, 8))
    has_res = res is not None

    def body(*refs):
        if has_res:
            x_ref, g_ref, dh_ref, res_ref, dx_ref, dg_ref = refs
        else:
            x_ref, g_ref, dh_ref, dx_ref, dg_ref = refs
        xv = x_ref[...]
        r = lax.rsqrt(jnp.mean(xv * xv, axis=-1, keepdims=True) + RMS_EPS)
        xh = xv * r
        dhv = dh_ref[...].astype(F32)
        gd = dhv * g_ref[...]
        dx = r * (gd - xh * jnp.mean(gd * xh, axis=-1, keepdims=True))
        if has_res:
            dx = dx + res_ref[...]
        dx_ref[...] = dx.astype(out_dtype)
        part = jnp.sum(dhv * xh, axis=0, keepdims=True)

        @pl.when(pl.program_id(0) == 0)
        def _():
            dg_ref[...] = part

        @pl.when(pl.program_id(0) > 0)
        def _():
            dg_ref[...] += part

    row = pl.BlockSpec((tm, D), lambda i: (i, 0))
    vec = pl.BlockSpec((1, D), lambda i: (0, 0))
    ins = [x, g, dh] + ([res] if has_res else [])
    return pl.pallas_call(
        body, name=name, grid=(S // tm,),
        in_specs=[row, vec, row] + ([row] if has_res else []),
        out_specs=[row, vec],
        out_shape=[jax.ShapeDtypeStruct((S, D), out_dtype), jax.ShapeDtypeStruct((1, D), F32)],
        compiler_params=_cparams(("arbitrary",)),
    )(*ins)


def _rel_bucket(dist):
    max_exact = REL_BUCKETS // 2
    d_f = jnp.maximum(dist, 1).astype(jnp.float32)
    large = max_exact + (jnp.log(d_f / max_exact) / math.log(REL_MAX_DIST / max_exact)
                         * (REL_BUCKETS - max_exact)).astype(jnp.int32)
    large = jnp.minimum(large, REL_BUCKETS - 1)
    return jnp.where(dist < max_exact, dist, large)


def _bias_tables(rel_bias):
    qi = jnp.arange(Q_BLOCK)[:, None]
    kj = jnp.arange(2 * Q_BLOCK)[None, :]
    m = qi - kj + Q_BLOCK
    band_ok = (m >= 0) & (m <= Q_BLOCK)
    tabs = []
    for d in DILATIONS:
        bucket = _rel_bucket(jnp.maximum(m, 0) * d)
        onehot = (bucket[:, :, None] == jnp.arange(REL_BUCKETS)[None, None, :]).astype(F32)
        bias = jnp.einsum("qkb,bh->hqk", onehot, rel_bias.astype(F32), precision=lax.Precision.HIGHEST)
        tabs.append(jnp.where(band_ok[None], bias, NEG_INF))
    return jnp.stack(tabs)


HEADS_PER_STEP = {16: 1, 4: 1, 1: 8}


def _rows(r, d):
    return pl.ds(r, Q_BLOCK, stride=d) if d > 1 else pl.ds(0, Q_BLOCK)


def _attn_fwd(proj, bias, state, d, last, cat_in, name):
    S = proj.shape[0]
    R = Q_BLOCK * d
    nsb = S // R
    first = state is None
    scale = HEAD_DIM ** -0.5
    hb = HEADS_PER_STEP[d]
    W = hb * HEAD_DIM

    def body(*refs):
        q_ref, kp_ref, kc_ref, vp_ref, vc_ref, b_ref = refs[:6]
        pos = 6
        if not first:
            m_in, l_in, a_in = refs[pos:pos + 3]
            pos += 3
        if last:
            pos += 1
            o_ref, lse_ref, o_tmp = refs[pos:pos + 3]
        else:
            m_out, l_out, a_out = refs[pos:pos + 3]
        n = pl.program_id(0)
        for hh in range(hb):
            ls = slice(hh * HEAD_DIM, (hh + 1) * HEAD_DIM)
            bp = b_ref[hh, :, :Q_BLOCK]
            bc = b_ref[hh, :, Q_BLOCK:]
            for r in range(d):
                rows = _rows(r, d)
                q = q_ref[rows, ls]
                sp = _dot_nt(q, kp_ref[rows, ls]) * scale + bp
                sp = jnp.where(n == 0, NEG_INF, sp)
                sc = _dot_nt(q, kc_ref[rows, ls]) * scale + bc
                mrow = jnp.maximum(jnp.max(sp, axis=-1, keepdims=True), jnp.max(sc, axis=-1, keepdims=True))
                if first:
                    m_new = mrow
                else:
                    m_old = m_in[rows, ls][:, :1]
                    m_new = jnp.maximum(m_old, mrow)
                pp = jnp.exp(sp - m_new)
                pc = jnp.exp(sc - m_new)
                lrow = jnp.sum(pp, axis=-1, keepdims=True) + jnp.sum(pc, axis=-1, keepdims=True)
                pv = _dot_nn(pp, vp_ref[rows, ls]) + _dot_nn(pc, vc_ref[rows, ls])
                if first:
                    l_new, a_new = lrow, pv
                else:
                    alpha = jnp.exp(m_old - m_new)
                    l_new = alpha * l_in[rows, ls][:, :1] + lrow
                    a_new = alpha * a_in[rows, ls] + pv
                if last:
                    o_tmp[rows, ls] = a_new / l_new
                    lse_ref[rows, ls] = jnp.broadcast_to(m_new + jnp.log(l_new), (Q_BLOCK, HEAD_DIM))
                else:
                    m_out[rows, ls] = jnp.broadcast_to(m_new, (Q_BLOCK, HEAD_DIM))
                    l_out[rows, ls] = jnp.broadcast_to(l_new, (Q_BLOCK, HEAD_DIM))
                    a_out[rows, ls] = a_new
        if last:
            o_ref[...] = o_tmp[...].astype(BF16)

    ng = N_HEADS // hb

    def col(g0):
        return pl.BlockSpec((R, W), lambda n, h: (n, g0 * ng + h))

    def col_prev(g0):
        return pl.BlockSpec((R, W), lambda n, h: (jnp.maximum(n - 1, 0), g0 * ng + h))

    in_specs = [col(0), col_prev(1), col(1), col_prev(2), col(2),
                pl.BlockSpec((hb, Q_BLOCK, 2 * Q_BLOCK), lambda n, h: (h, 0, 0))]
    ins = [proj, proj, proj, proj, proj, bias]
    if not first:
        in_specs += [col(0)] * 3
        ins += list(state)
    st = jax.ShapeDtypeStruct((S, ATTN_W), F32)
    if last:
        in_specs.append(pl.BlockSpec(memory_space=pl.ANY))
        ins.append(cat_in)
        out_specs = [col(0), col(0)]
        out_shape = [jax.ShapeDtypeStruct(cat_in.shape, BF16), st]
        scratch = [pltpu.VMEM((R, W), F32)]
        aliases = {len(ins) - 1: 0}
    else:
        out_specs = [col(0)] * 3
        out_shape = [st, st, st]
        scratch = []
        aliases = {}
    return pl.pallas_call(
        body, name=name, grid=(nsb, ng), in_specs=in_specs, out_specs=out_specs,
        out_shape=out_shape, scratch_shapes=scratch, input_output_aliases=aliases,
        compiler_params=_cparams(("arbitrary", "arbitrary")),
    )(*ins)


def _attn_bwd(proj, bias, cat, d_cat, lse, acc, d, last, name):
    S = proj.shape[0]
    R = Q_BLOCK * d
    nsb = S // R
    first = acc is None
    scale = HEAD_DIM ** -0.5
    odt = BF16 if last else F32
    hb = HEADS_PER_STEP[d]
    W = hb * HEAD_DIM

    def body(*refs):
        q_ref, kp_ref, kc_ref, vp_ref, vc_ref, b_ref, o_ref, do_ref, lse_ref = refs[:9]
        pos = 9
        if not first:
            dq_in, dk_in, dv_in = refs[pos:pos + 3]
            pos += 3
        dq_out, dk_out, dv_out, db_out = refs[pos:pos + 4]
        pos += 4
        ck, cv, o_f, do_f, dq_t, dk_t, dv_t, db_acc = refs[pos:pos + 8]
        n = pl.program_id(1)

        @pl.when(n == 0)
        def _():
            db_acc[...] = jnp.zeros_like(db_acc)
            ck[...] = jnp.zeros_like(ck)
            cv[...] = jnp.zeros_like(cv)

        @pl.when(n < nsb)
        def _():
            o_f[...] = o_ref[...].astype(F32)
            do_f[...] = do_ref[...].astype(F32)
            for hh in range(hb):
                ls = slice(hh * HEAD_DIM, (hh + 1) * HEAD_DIM)
                bp = b_ref[hh, :, :Q_BLOCK]
                bc = b_ref[hh, :, Q_BLOCK:]
                for r in range(d):
                    rows = _rows(r, d)
                    q = q_ref[rows, ls].astype(BF16)
                    kp = kp_ref[rows, ls].astype(BF16)
                    kc = kc_ref[rows, ls].astype(BF16)
                    vp = vp_ref[rows, ls].astype(BF16)
                    vc = vc_ref[rows, ls].astype(BF16)
                    do_f32 = do_f[rows, ls]
                    do = do_f32.astype(BF16)
                    lse_r = lse_ref[rows, ls][:, :1]
                    sp = _dot_nt(q, kp) * scale + bp
                    sp = jnp.where(n == 0, NEG_INF, sp)
                    sc = _dot_nt(q, kc) * scale + bc
                    pp = jnp.exp(sp - lse_r)
                    pc = jnp.exp(sc - lse_r)
                    dd = jnp.sum(do_f32 * o_f[rows, ls], axis=-1, keepdims=True)
                    dsp = pp * (_dot_nt(do, vp) - dd)
                    dsc = pc * (_dot_nt(do, vc) - dd)
                    db_acc[hh, :, :Q_BLOCK] += dsp
                    db_acc[hh, :, Q_BLOCK:] += dsc
                    dq = (_dot_nn(dsp, kp) + _dot_nn(dsc, kc)) * scale
                    dk_prev = ck[rows, ls] + _dot_tn(dsp, q) * scale
                    dv_prev = cv[rows, ls] + _dot_tn(pp, do)
                    ck[rows, ls] = _dot_tn(dsc, q) * scale
                    cv[rows, ls] = _dot_tn(pc, do)
                    if not first:
                        dq = dq + dq_in[rows, ls]
                        dk_prev = dk_prev + dk_in[rows, ls]
                        dv_prev = dv_prev + dv_in[rows, ls]
                    dq_t[rows, ls] = dq
                    dk_t[rows, ls] = dk_prev
                    dv_t[rows, ls] = dv_prev
            dq_out[...] = dq_t[...].astype(odt)

            @pl.when(n > 0)
            def _():
                dk_out[...] = dk_t[...].astype(odt)
                dv_out[...] = dv_t[...].astype(odt)

        @pl.when(n == nsb)
        def _():
            if first:
                dk_out[...] = ck[...].astype(odt)
                dv_out[...] = cv[...].astype(odt)
            else:
                dk_out[...] = (ck[...] + dk_in[...]).astype(odt)
                dv_out[...] = (cv[...] + dv_in[...]).astype(odt)
            db_out[...] = db_acc[...]

    last_n = nsb - 1
    ng = N_HEADS // hb

    def cur(g0):
        return pl.BlockSpec((R, W), lambda h, n: (jnp.minimum(n, last_n), g0 * ng + h))

    def prev(g0):
        return pl.BlockSpec((R, W), lambda h, n: (jnp.maximum(jnp.minimum(n, last_n) - 1, 0), g0 * ng + h))

    delayed = pl.BlockSpec((R, W), lambda h, n: (jnp.maximum(n - 1, 0), h))
    in_specs = [cur(0), prev(1), cur(1), prev(2), cur(2),
                pl.BlockSpec((hb, Q_BLOCK, 2 * Q_BLOCK), lambda h, n: (h, 0, 0)),
                cur(0), cur(0), cur(0)]
    ins = [proj, proj, proj, proj, proj, bias, cat, d_cat, lse]
    if not first:
        in_specs += [cur(0), delayed, delayed]
        ins += list(acc)
    st = jax.ShapeDtypeStruct((S, ATTN_W), odt)
    tile = pltpu.VMEM((R, W), F32)
    return pl.pallas_call(
        body, name=name, grid=(ng, nsb + 1), in_specs=in_specs,
        out_specs=[cur(0), delayed, delayed,
                   pl.BlockSpec((hb, Q_BLOCK, 2 * Q_BLOCK), lambda h, n: (h, 0, 0))],
        out_shape=[st, st, st, jax.ShapeDtypeStruct((N_HEADS, Q_BLOCK, 2 * Q_BLOCK), F32)],
        scratch_shapes=[tile] * 7 + [pltpu.VMEM((hb, Q_BLOCK, 2 * Q_BLOCK), F32)],
        compiler_params=_cparams(("arbitrary", "arbitrary")),
    )(*ins)


HG_LEVELS = (32, 16, 8, 4, 2, 1)
N_LEV = len(HG_LEVELS)


def _hg_consts():
    C = HG_CHUNK
    t = np.arange(C)
    mq, mk, masks = [], [], []
    for B in HG_LEVELS:
        up = (t // B) % 2 == 1
        bs = (t // B) * B
        be = bs + B - 1
        mq.append(up[:, None] & (t[None, :] >= bs[:, None]) & (t[None, :] <= t[:, None]))
        mk.append((~up)[:, None] & (t[None, :] > t[:, None]) & (t[None, :] <= be[:, None]))
        masks.append(up[:, None] & (~up)[None, :] & ((t[:, None] // (2 * B)) == (t[None, :] // (2 * B))))
    masks.append(np.eye(C, dtype=bool))
    mb = t[None, :] <= t[:, None]
    mw = t[None, :] > t[:, None]
    m_all = np.concatenate(mq + mk + [mb, mw], axis=0).astype(np.float32)
    return jnp.asarray(m_all, BF16), jnp.asarray(np.stack(masks).astype(np.float32))


def _split3(v):
    hi = v.astype(BF16)
    r1 = v - hi.astype(F32)
    mid = r1.astype(BF16)
    lo = (r1 - mid.astype(F32)).astype(BF16)
    return jnp.concatenate([hi, mid, lo], axis=1)


def _hg_chunk_fwd(fz, iv, qz, lbh, m_all, mask_ref, st_t):
    C = HG_CHUNK
    sig = _sigmoid(fz)
    f = lbh + (1.0 - lbh) * sig
    lf = jnp.log(f)
    kk = 1.0 - f
    sq = _sigmoid(qz)
    qq = qz * sq
    a3 = lax.dot_general(m_all, _split3(lf), (((1,), (0,)), ((), ())), preferred_element_type=F32)
    args = a3[:, :HEAD_DIM] + a3[:, HEAD_DIM:2 * HEAD_DIM] + a3[:, 2 * HEAD_DIM:]
    e = jnp.exp(args)
    qs = [qq * e[j * C:(j + 1) * C] for j in range(N_LEV)]
    ks = [kk * e[(N_LEV + j) * C:(N_LEV + j + 1) * C] for j in range(N_LEV)]
    a = mask_ref[N_LEV] * _dot_nt(qq, kk)
    for j in range(N_LEV):
        a = a + mask_ref[j] * _dot_nt(qs[j], ks[j])
    eb = e[2 * N_LEV * C:(2 * N_LEV + 1) * C]
    ew = e[(2 * N_LEV + 1) * C:]
    qe = qq * eb
    w = kk * ew
    o = _dot_nt(qe, st_t) + _dot_nn(a, iv)
    eb_last = eb[C - 1:C, :]
    new_st = st_t * eb_last + _dot_tn(iv, w)
    return dict(sig=sig, f=f, kk=kk, sq=sq, qq=qq, e=e, qs=qs, ks=ks, a=a, eb=eb, ew=ew, qe=qe, w=w, o=o,
                eb_last=eb_last, new_st=new_st)


def _hg_rows(S):
    return _blk(S, (256, 128, 64))


def _host_refs(refs, n_in, n_out, n_scratch, comm):
    nci = len(comm["ins"]) if comm else 0
    nco = len(comm["out_shapes"]) if comm else 0
    a = n_in
    b = a + nci
    c = b + n_out
    d = c + nco
    e = d + n_scratch
    return refs[:a], refs[a:b], refs[b:c], refs[c:d], refs[d:e], refs[e:]


def _host_call(comm, n_in, n_out, in_specs, out_specs, out_shape, scratch, aliases):
    if comm:
        in_specs = in_specs + [ANY] * len(comm["ins"])
        out_specs = out_specs + [ANY] * len(comm["out_shapes"])
        out_shape = out_shape + comm["out_shapes"]
        scratch = scratch + [pltpu.SemaphoreType.DMA((comm["nsem"],)), pltpu.SemaphoreType.DMA((comm["nsem"],))]
        aliases = dict(aliases)
        aliases.update({n_in + i: n_out + o for i, o in comm["aliases"].items()})
    return dict(in_specs=in_specs, out_specs=out_specs, out_shape=out_shape, scratch_shapes=scratch,
                input_output_aliases=aliases)


def _hgrn_fwd(proj, lb, gain, cat_in, name, comm=None):
    S = proj.shape[0]
    rows = _hg_rows(S)
    cb = rows // HG_CHUNK
    nsteps = S // rows
    m_all, masks = _hg_consts()

    def body(*refs):
        own_in, c_in, own_out, c_out, own_scr, sems = _host_refs(refs, 9, 4, 1, comm)
        fz_ref, iv_ref, qz_ref, gz_ref, lb_ref, gain_ref, m_ref, mask_ref, _ = own_in
        o_ref, st_ref, a_ref, raw_ref = own_out
        (st_scr,) = own_scr

        @pl.when(pl.program_id(0) == 0)
        def _():
            st_scr[...] = jnp.zeros_like(st_scr)
            if comm:
                comm["start"](c_in, c_out, *sems)

        m_all_v = m_ref[...]

        def chunk(c, carry):
            rs = pl.ds(pl.multiple_of(c * HG_CHUNK, HG_CHUNK), HG_CHUNK)
            for h in range(N_HEADS):
                sl = slice(h * HEAD_DIM, (h + 1) * HEAD_DIM)
                st_t = st_scr[h]
                st_ref[c, h] = st_t
                gz = gz_ref[rs, sl]
                iv = iv_ref[rs, sl]
                q = _hg_chunk_fwd(fz_ref[rs, sl], iv, qz_ref[rs, sl], lb_ref[:, sl], m_all_v, mask_ref, st_t)
                st_scr[h] = q["new_st"]
                o = q["o"]
                a_ref[c, h] = q["a"].astype(BF16)
                raw_ref[rs, sl] = o
                r = lax.rsqrt(jnp.mean(o * o, axis=-1, keepdims=True) + RMS_EPS)
                y = ((o * r) * gain_ref[:, sl]) * (gz * _sigmoid(gz))
                o_ref[rs, sl] = y.astype(BF16)
            return carry

        lax.fori_loop(0, cb, chunk, 0)
        if comm:
            @pl.when(pl.program_id(0) == nsteps - 1)
            def _():
                comm["wait"](c_in, c_out, *sems)

    def col(c):
        return pl.BlockSpec((rows, HG_W), lambda i: (i, c))

    vec = pl.BlockSpec((1, HG_W), lambda i: (0, 0))
    args = _host_call(
        comm, 9, 4,
        in_specs=[col(3), col(4), col(5), col(6), vec, vec,
                  pl.BlockSpec(m_all.shape, lambda i: (0, 0)), pl.BlockSpec(masks.shape, lambda i: (0, 0, 0)),
                  pl.BlockSpec(memory_space=pl.ANY)],
        out_specs=[col(1), pl.BlockSpec((cb, N_HEADS, HEAD_DIM, HEAD_DIM), lambda i: (i, 0, 0, 0)),
                   pl.BlockSpec((cb, N_HEADS, HG_CHUNK, HG_CHUNK), lambda i: (i, 0, 0, 0)), col(0)],
        out_shape=[jax.ShapeDtypeStruct(cat_in.shape, BF16),
                   jax.ShapeDtypeStruct((S // HG_CHUNK, N_HEADS, HEAD_DIM, HEAD_DIM), F32),
                   jax.ShapeDtypeStruct((S // HG_CHUNK, N_HEADS, HG_CHUNK, HG_CHUNK), BF16),
                   jax.ShapeDtypeStruct((S, HG_W), F32)],
        scratch=[pltpu.VMEM((N_HEADS, HEAD_DIM, HEAD_DIM), F32)],
        aliases={8: 0})
    return pl.pallas_call(
        body, name=name, grid=(nsteps,), compiler_params=_cparams(("arbitrary",)), **args,
    )(proj, proj, proj, proj, lb, gain, m_all, masks, cat_in, *(comm["ins"] if comm else []))


def _hgrn_bwd(proj, lb, gain, states, a_saved, o_saved, d_cat, name, comm=None):
    S = proj.shape[0]
    rows = _hg_rows(S)
    cb = rows // HG_CHUNK
    nblk = S // rows
    C = HG_CHUNK
    m_all, masks = _hg_consts()

    def body(*refs):
        own_in, c_in, own_out, c_out, own_scr, sems = _host_refs(refs, 12, 3, 1, comm)
        fz_ref, iv_ref, qz_ref, gz_ref, lb_ref, gain_ref, m_ref, mask_ref, st_ref, a_ref, raw_ref, dy_ref = own_in
        dz_ref, dlb_ref, dgain_ref = own_out
        (dst_scr,) = own_scr

        @pl.when(pl.program_id(0) == 0)
        def _():
            dst_scr[...] = jnp.zeros_like(dst_scr)
            dlb_ref[...] = jnp.zeros_like(dlb_ref)
            dgain_ref[...] = jnp.zeros_like(dgain_ref)
            if comm:
                comm["start"](c_in, c_out, *sems)

        m_all_v = m_ref[...]
        last_row = lax.broadcasted_iota(jnp.int32, (C, HEAD_DIM), 0) == C - 1

        def chunk(ci, carry):
            c = cb - 1 - ci
            rs = pl.ds(pl.multiple_of(c * C, C), C)
            for h in range(N_HEADS):
                sl = slice(h * HEAD_DIM, (h + 1) * HEAD_DIM)
                lbh = lb_ref[:, sl]
                gh = gain_ref[:, sl]
                st_t = st_ref[c, h]
                fz = fz_ref[rs, sl]
                iv = iv_ref[rs, sl]
                qz = qz_ref[rs, sl]
                gz = gz_ref[rs, sl]
                q = _hg_chunk_fwd(fz, iv, qz, lbh, m_all_v, mask_ref, st_t)
                o = raw_ref[rs, sl]
                a_fwd = a_ref[c, h]
                r = lax.rsqrt(jnp.mean(o * o, axis=-1, keepdims=True) + RMS_EPS)
                on = o * r
                sg = _sigmoid(gz)
                gate = gz * sg
                dy = dy_ref[rs, sl].astype(F32)
                dgain_ref[:, sl] += jnp.sum(dy * on * gate, axis=0, keepdims=True)
                dgz = (dy * on * gh) * (sg * (1.0 + gz * (1.0 - sg)))
                don = dy * gh * gate
                do = r * (don - on * jnp.mean(don * on, axis=-1, keepdims=True))
                da = _dot_nt(do, iv)
                dv = _dot_tn(a_fwd, do)
                dqe = _dot_nn(do, st_t)
                dst_new = _dot_tn(do, q["qe"])
                dsp = dst_scr[h]
                dw = _dot_nn(iv, dsp)
                dv = dv + _dot_nt(q["w"], dsp)
                d_eb_last = jnp.sum(dsp * st_t, axis=0, keepdims=True)
                dst_scr[h] = dsp * q["eb_last"] + dst_new
                dad = da * mask_ref[N_LEV]
                dq = _dot_nn(dad, q["kk"])
                dk = _dot_tn(dad, q["qq"])
                dargs_q = []
                dargs_k = []
                for j in range(N_LEV):
                    daj = da * mask_ref[j]
                    dqj = _dot_nn(daj, q["ks"][j])
                    dkj = _dot_tn(daj, q["qs"][j])
                    dq = dq + dqj * q["e"][j * C:(j + 1) * C]
                    dk = dk + dkj * q["e"][(N_LEV + j) * C:(N_LEV + j + 1) * C]
                    dargs_q.append(dqj * q["qs"][j])
                    dargs_k.append(dkj * q["ks"][j])
                dq = dq + dqe * q["eb"]
                darg_b = dqe * q["qe"] + jnp.where(last_row, d_eb_last * q["eb_last"], 0.0)
                dk = dk + dw * q["ew"]
                darg_w = dw * q["w"]
                dall = jnp.concatenate(dargs_q + dargs_k + [darg_b, darg_w], axis=0)
                hi = dall.astype(BF16)
                lo = (dall - hi.astype(F32)).astype(BF16)
                dl2 = lax.dot_general(m_all_v, jnp.concatenate([hi, lo], axis=1), (((0,), (0,)), ((), ())),
                                      preferred_element_type=F32)
                dlf = dl2[:, :HEAD_DIM] + dl2[:, HEAD_DIM:]
                df = dlf / q["f"] - dk
                sig = q["sig"]
                dlb_ref[:, sl] += jnp.sum(df * (1.0 - sig), axis=0, keepdims=True)
                dfz = df * (1.0 - lbh) * (sig * (1.0 - sig))
                sq = q["sq"]
                dqz = dq * (sq * (1.0 + qz * (1.0 - sq)))
                dz_ref[rs, h * HEAD_DIM:(h + 1) * HEAD_DIM] = dfz.astype(BF16)
                dz_ref[rs, HG_W + h * HEAD_DIM:HG_W + (h + 1) * HEAD_DIM] = dv.astype(BF16)
                dz_ref[rs, 2 * HG_W + h * HEAD_DIM:2 * HG_W + (h + 1) * HEAD_DIM] = dqz.astype(BF16)
                dz_ref[rs, 3 * HG_W + h * HEAD_DIM:3 * HG_W + (h + 1) * HEAD_DIM] = dgz.astype(BF16)
            return carry

        lax.fori_loop(0, cb, chunk, 0)
        if comm:
            @pl.when(pl.program_id(0) == nblk - 1)
            def _():
                comm["wait"](c_in, c_out, *sems)

    def col(c):
        return pl.BlockSpec((rows, HG_W), lambda i: (nblk - 1 - i, c))

    vec = pl.BlockSpec((1, HG_W), lambda i: (0, 0))
    args = _host_call(
        comm, 12, 3,
        in_specs=[col(3), col(4), col(5), col(6), vec, vec,
                  pl.BlockSpec(m_all.shape, lambda i: (0, 0)), pl.BlockSpec(masks.shape, lambda i: (0, 0, 0)),
                  pl.BlockSpec((cb, N_HEADS, HEAD_DIM, HEAD_DIM), lambda i: (nblk - 1 - i, 0, 0, 0)),
                  pl.BlockSpec((cb, N_HEADS, HG_CHUNK, HG_CHUNK), lambda i: (nblk - 1 - i, 0, 0, 0)),
                  col(0), col(1)],
        out_specs=[pl.BlockSpec((rows, 4 * HG_W), lambda i: (nblk - 1 - i, 0)), vec, vec],
        out_shape=[jax.ShapeDtypeStruct((S, 4 * HG_W), BF16), jax.ShapeDtypeStruct((1, HG_W), F32),
                   jax.ShapeDtypeStruct((1, HG_W), F32)],
        scratch=[pltpu.VMEM((N_HEADS, HEAD_DIM, HEAD_DIM), F32)],
        aliases={})
    return pl.pallas_call(
        body, name=name, grid=(nblk,), compiler_params=_cparams(("arbitrary",)), **args,
    )(proj, proj, proj, proj, lb, gain, m_all, masks, states, a_saved, o_saved, d_cat,
      *(comm["ins"] if comm else []))


def _cross_fwd(cq, ckv, name):
    S = cq.shape[0]
    n_mem = ckv.shape[0]
    tq = _blk(S, (256, 128))
    scale = CROSS_DIM ** -0.5

    def body(q_ref, kv_ref, o_ref):
        for h in range(CROSS_HEADS):
            sl = slice(h * CROSS_DIM, (h + 1) * CROSS_DIM)
            k = kv_ref[:, sl]
            v = kv_ref[:, D_MODEL + h * CROSS_DIM:D_MODEL + (h + 1) * CROSS_DIM]
            s = _dot_nt(q_ref[:, sl], k) * scale
            m = jnp.max(s, axis=-1, keepdims=True)
            p = jnp.exp(s - m)
            p = p / jnp.sum(p, axis=-1, keepdims=True)
            o_ref[:, sl] = _dot_nn(p, v).astype(BF16)

    return pl.pallas_call(
        body, name=name, grid=(S // tq,),
        in_specs=[pl.BlockSpec((tq, D_MODEL), lambda i: (i, 0)), pl.BlockSpec((n_mem, 2 * D_MODEL), lambda i: (0, 0))],
        out_specs=pl.BlockSpec((tq, D_MODEL), lambda i: (i, 0)),
        out_shape=jax.ShapeDtypeStruct((S, D_MODEL), BF16),
        compiler_params=_cparams(("parallel",)),
    )(cq, ckv)


def _cross_bwd(cq, ckv, d_o, name):
    S = cq.shape[0]
    n_mem = ckv.shape[0]
    tq = _blk(S, (256, 128))
    scale = CROSS_DIM ** -0.5

    def body(q_ref, kv_ref, do_ref, dq_ref, dkv_ref):
        @pl.when(pl.program_id(0) == 0)
        def _():
            dkv_ref[...] = jnp.zeros_like(dkv_ref)

        for h in range(CROSS_HEADS):
            sl = slice(h * CROSS_DIM, (h + 1) * CROSS_DIM)
            slv = slice(D_MODEL + h * CROSS_DIM, D_MODEL + (h + 1) * CROSS_DIM)
            q = q_ref[:, sl]
            k = kv_ref[:, sl]
            v = kv_ref[:, slv]
            do = do_ref[:, sl]
            s = _dot_nt(q, k) * scale
            m = jnp.max(s, axis=-1, keepdims=True)
            p = jnp.exp(s - m)
            p = p / jnp.sum(p, axis=-1, keepdims=True)
            dp = _dot_nt(do, v)
            ds = p * (dp - jnp.sum(dp * p, axis=-1, keepdims=True)) * scale
            dq_ref[:, sl] = _dot_nn(ds, k).astype(BF16)
            dkv_ref[:, sl] += _dot_tn(ds, q)
            dkv_ref[:, slv] += _dot_tn(p, do)

    row = pl.BlockSpec((tq, D_MODEL), lambda i: (i, 0))
    kv = pl.BlockSpec((n_mem, 2 * D_MODEL), lambda i: (0, 0))
    return pl.pallas_call(
        body, name=name, grid=(S // tq,),
        in_specs=[row, kv, row], out_specs=[row, kv],
        out_shape=[jax.ShapeDtypeStruct((S, D_MODEL), BF16), jax.ShapeDtypeStruct((n_mem, 2 * D_MODEL), F32)],
        compiler_params=_cparams(("arbitrary",)),
    )(cq, ckv, d_o)


FF_BLOCK = 512


def _mm_gate_up(hf, w_gu, name, comm=None):
    S, K = hf.shape
    bm = _blk(S, (1024, 512, 256, 128))
    ni = S // bm
    nj = D_FF // FF_BLOCK

    def body(*refs):
        own_in, c_in, own_out, c_out, _, sems = _host_refs(refs, 3, 3, 0, comm)
        a_ref, bg_ref, bu_ref = own_in
        g_ref, u_ref, act_ref = own_out
        if comm:
            @pl.when((pl.program_id(0) == 0) & (pl.program_id(1) == 0))
            def _():
                comm["start"](c_in, c_out, *sems)

        a = a_ref[...]
        g = _dot_nn(a, bg_ref[...])
        u = _dot_nn(a, bu_ref[...])
        g_ref[...] = g.astype(BF16)
        u_ref[...] = u.astype(BF16)
        act_ref[...] = ((g * _sigmoid(g)) * u).astype(BF16)
        if comm:
            @pl.when((pl.program_id(0) == ni - 1) & (pl.program_id(1) == nj - 1))
            def _():
                comm["wait"](c_in, c_out, *sems)

    out = pl.BlockSpec((bm, FF_BLOCK), lambda i, j: (i, j))
    sds = jax.ShapeDtypeStruct((S, D_FF), BF16)
    args = _host_call(
        comm, 3, 3,
        in_specs=[pl.BlockSpec((bm, K), lambda i, j: (i, 0)), pl.BlockSpec((K, FF_BLOCK), lambda i, j: (0, j)),
                  pl.BlockSpec((K, FF_BLOCK), lambda i, j: (0, j + nj))],
        out_specs=[out, out, out], out_shape=[sds, sds, sds], scratch=[], aliases={})
    return pl.pallas_call(
        body, name=name, grid=(ni, nj), compiler_params=_cparams(("arbitrary", "arbitrary")), **args,
    )(hf, w_gu, w_gu, *(comm["ins"] if comm else []))


def _mm_down_x(dy, w_down, g, u, name):
    S, K = dy.shape
    bm = _blk(S, (1024, 512, 256, 128))
    nj = D_FF // FF_BLOCK

    def body(a_ref, b_ref, g_ref, u_ref, dg_ref, du_ref):
        da = _dot_nt(a_ref[...], b_ref[...])
        gv = g_ref[...].astype(F32)
        sg = _sigmoid(gv)
        dg_ref[...] = (da * u_ref[...].astype(F32) * (sg * (1.0 + gv * (1.0 - sg)))).astype(BF16)
        du_ref[...] = (da * (gv * sg)).astype(BF16)

    tile = pl.BlockSpec((bm, FF_BLOCK), lambda i, j: (i, j))
    sds = jax.ShapeDtypeStruct((S, D_FF), BF16)
    return pl.pallas_call(
        body, name=name, grid=(S // bm, nj),
        in_specs=[pl.BlockSpec((bm, K), lambda i, j: (i, 0)), pl.BlockSpec((FF_BLOCK, K), lambda i, j: (j, 0)),
                  tile, tile],
        out_specs=[tile, tile], out_shape=[sds, sds],
        compiler_params=_cparams(("parallel", "arbitrary")),
    )(dy, w_down, g, u)


def _loss_head(y, target, name):
    S, D = y.shape
    tm = _blk(S, (256, 128, 64, 8))

    def body(y_ref, t_ref, dy_ref, l_ref):
        diff = y_ref[...] - t_ref[...]
        dy_ref[...] = diff * (1.0 / D)
        sq = (diff * diff) * (0.5 / D)
        part = jnp.sum(sq.reshape(tm // 8, 8, D), axis=0)

        @pl.when(pl.program_id(0) == 0)
        def _():
            l_ref[...] = part

        @pl.when(pl.program_id(0) > 0)
        def _():
            l_ref[...] += part

    row = pl.BlockSpec((tm, D), lambda i: (i, 0))
    return pl.pallas_call(
        body, name=name, grid=(S // tm,), in_specs=[row, row],
        out_specs=[row, pl.BlockSpec((8, D), lambda i: (0, 0))],
        out_shape=[jax.ShapeDtypeStruct((S, D), F32), jax.ShapeDtypeStruct((8, D), F32)],
        compiler_params=_cparams(("arbitrary",)),
    )(y, target)


def _layer_fwd(x0, mem, g, w, lb, hg_gain, bias, tag, comm=None):
    S = x0.shape[0]
    h0 = _rms_fwd(x0, g[0], f"rms0_{tag}")
    proj = _mm(h0, w["w_in"], "nn", F32, f"mm_in_{tag}")
    state = None
    cat = lax.empty((S, D_MODEL), BF16)
    for bi, d in enumerate(DILATIONS):
        last = bi == len(DILATIONS) - 1
        out = _attn_fwd(proj, bias[bi], state, d, last, cat if last else None, f"attn_fwd{d}_{tag}")
        if last:
            cat, lse = out
        else:
            state = tuple(out)
    comm = comm or {}
    cat, states, hg_a, hg_o, *carried_a = _hgrn_fwd(proj, lb, hg_gain, cat, f"hgrn_fwd_{tag}", comm.get("hgrn"))
    mix = _mm(cat, w["w_out"], "nn", F32, f"mm_out_{tag}")
    x1, hc = _norm_residual(x0, mix, g[1], g[2], f"res1_{tag}")
    cq = _mm(hc, w["w_cq"], "nn", BF16, f"mm_cq_{tag}")
    mn = _rms_fwd(mem, g[3], f"rms3_{tag}")
    ckv = _mm(mn, w["w_ckv"], "nn", BF16, f"mm_ckv_{tag}")
    cop = _cross_fwd(cq, ckv, f"cross_fwd_{tag}")
    co = _mm(cop, w["w_co"], "nn", F32, f"mm_co_{tag}")
    x2, hf = _norm_residual(x1, co, g[4], g[5], f"res2_{tag}")
    gate, up, act, *carried_b = _mm_gate_up(hf, w["w_gate_up"], f"mm_gu_{tag}", comm.get("gate_up"))
    carried = dict(hgrn=carried_a, gate_up=carried_b)
    y = _mm(act, w["w_down"], "nn", F32, f"mm_down_{tag}")
    x3 = _norm_residual(x2, y, g[6], None, f"res3_{tag}")
    saved = dict(x0=x0, h0=h0, proj=proj, cat=cat, lse=lse, states=states, hg_a=hg_a, hg_o=hg_o, mix=mix, x1=x1, hc=hc, cq=cq, mn=mn,
                 ckv=ckv, cop=cop, co=co, x2=x2, hf=hf, gate=gate, up=up, act=act, y=y)
    return x3, saved, carried


def _layer_bwd(dx3, sv, mem, g, w, lb, hg_gain, bias, tag, comm_fn=None):
    dg = [None] * N_NORMS
    gw = {}
    dy, dg[6] = _rms_bwd(sv["y"], g[6], dx3, None, BF16, f"rmsb6_{tag}")
    dgu = list(_mm_down_x(dy, w["w_down"], sv["gate"], sv["up"], f"mmb_down_x_{tag}"))
    gw["w_down"] = _mm(sv["act"], dy, "tn", BF16, f"mmb_down_w_{tag}")
    gw["w_gate_up"] = _mm(sv["hf"], dgu, "tn", BF16, f"mmb_gu_w_{tag}")
    d_hf = _mm(dgu, w["w_gate_up"], "nt", F32, f"mmb_gu_x_{tag}")
    dx2, dg[5] = _rms_bwd(sv["x2"], g[5], d_hf, dx3, F32, f"rmsb5_{tag}")
    d_co, dg[4] = _rms_bwd(sv["co"], g[4], dx2, None, BF16, f"rmsb4_{tag}")
    d_cop = _mm(d_co, w["w_co"], "nt", BF16, f"mmb_co_x_{tag}")
    gw["w_co"] = _mm(sv["cop"], d_co, "tn", BF16, f"mmb_co_w_{tag}")
    d_cq, d_ckv = _cross_bwd(sv["cq"], sv["ckv"], d_cop, f"cross_bwd_{tag}")
    gw["w_cq"] = _mm(sv["hc"], d_cq, "tn", BF16, f"mmb_cq_w_{tag}")
    d_hc = _mm(d_cq, w["w_cq"], "nt", F32, f"mmb_cq_x_{tag}")
    gw["w_ckv"] = _mm(sv["mn"], d_ckv, "tn", BF16, f"mmb_ckv_w_{tag}")
    d_mn = _mm(d_ckv, w["w_ckv"], "nt", F32, f"mmb_ckv_x_{tag}")
    _, dg[3] = _rms_bwd(mem, g[3], d_mn, None, BF16, f"rmsb3_{tag}")
    dx1, dg[2] = _rms_bwd(sv["x1"], g[2], d_hc, dx2, F32, f"rmsb2_{tag}")
    d_mix, dg[1] = _rms_bwd(sv["mix"], g[1], dx1, None, BF16, f"rmsb1_{tag}")
    d_cat = _mm(d_mix, w["w_out"], "nt", BF16, f"mmb_out_x_{tag}")
    gw["w_out"] = _mm(sv["cat"], d_mix, "tn", BF16, f"mmb_out_w_{tag}")
    acc = None
    dbias = []
    for bi, d in enumerate(DILATIONS):
        last = bi == len(DILATIONS) - 1
        dq, dk, dv, db = _attn_bwd(sv["proj"], bias[bi], sv["cat"], d_cat, sv["lse"], acc, d, last,
                                   f"attn_bwd{d}_{tag}")
        acc = (dq, dk, dv)
        dbias.append(db)
    comm = comm_fn(dict(gw)) if comm_fn else None
    d_hz, dlb, dgain, *carried = _hgrn_bwd(sv["proj"], lb, hg_gain, sv["states"], sv["hg_a"], sv["hg_o"], d_cat,
                                           f"hgrn_bwd_{tag}", comm)
    parts = [acc[0], acc[1], acc[2], d_hz]
    gw["w_in"] = _mm(sv["h0"], parts, "tn", BF16, f"mmb_in_w_{tag}")
    d_h0 = _mm(parts, w["w_in"], "nt", F32, f"mmb_in_x_{tag}")
    dx0, dg[0] = _rms_bwd(sv["x0"], g[0], d_h0, dx1, F32, f"rmsb0_{tag}")
    return dx0, gw, jnp.stack(dg), dlb, dgain, jnp.stack(dbias), carried


def _lb_all(lb_logits):
    p = jax.nn.softmax(lb_logits.astype(F32), axis=0)
    return jnp.cumsum(p, axis=0) - p


def _local_step(x, mem, target, rel_bias, lb_logits, norm_gains, hg_norm, weights_of_layer, hooks=None):
    hooks = hooks or {}
    L = lb_logits.shape[0]
    bias, bias_vjp = jax.vjp(_bias_tables, rel_bias)
    lb_all, lb_vjp = jax.vjp(_lb_all, lb_logits)
    gains = norm_gains.reshape(L, N_NORMS, 1, D_MODEL)
    saved = []
    h = x
    ws = []
    for l in range(L):
        w = weights_of_layer(l)
        ws.append(w)
        comm = hooks["fwd_comm"](l) if "fwd_comm" in hooks else None
        h, sv, carried = _layer_fwd(h, mem, gains[l], w, lb_all[l:l + 1], hg_norm[l:l + 1], bias, f"l{l}", comm)
        if comm:
            hooks["fwd_done"](l, carried)
        saved.append(sv)
    dy, lparts = _loss_head(h, target, "loss_head")
    loss = jnp.sum(lparts)
    d_gains, d_lb, d_hg, gws = [None] * L, [None] * L, [None] * L, [None] * L
    d_bias = jnp.zeros_like(bias)
    dh = dy
    for l in reversed(range(L)):
        comm_fn = functools.partial(hooks["bwd_comm"], l) if "bwd_comm" in hooks else None
        dh, gw, dgl, dlbl, dhgl, dbl, carried = _layer_bwd(dh, saved[l], mem, gains[l], ws[l], lb_all[l:l + 1],
                                                           hg_norm[l:l + 1], bias, f"l{l}", comm_fn)
        if comm_fn:
            hooks["bwd_done"](l, carried)
        d_gains[l], d_lb[l], d_hg[l], gws[l] = dgl.reshape(N_NORMS, D_MODEL), dlbl[0], dhgl[0], gw
        d_bias = d_bias + dbl
        if "layer_grads" in hooks:
            gws[l] = hooks["layer_grads"](l, gw)
    (d_rel_bias,) = bias_vjp(d_bias)
    (d_lb_logits,) = lb_vjp(jnp.stack(d_lb))
    return loss, dh, gws, d_rel_bias, d_lb_logits, jnp.stack(d_gains), jnp.stack(d_hg)


def kernel(x, mem, rel_bias, lb_logits, norm_gains, w_in, hg_norm, w_out, w_cq, w_ckv, w_co, w_gate_up, w_down, loss_target, m_rel_bias, m_lb_logits, m_norm_gains, m_w_in, m_hg_norm, m_w_out, m_w_cq, m_w_ckv, m_w_co, m_w_gate_up, m_w_down, v_rel_bias, v_lb_logits, v_norm_gains, v_w_in, v_hg_norm, v_w_out, v_w_cq, v_w_ckv, v_w_co, v_w_gate_up, v_w_down):
    return _train_step(
        x, mem, loss_target,
        dict(rel_bias=rel_bias, lb_logits=lb_logits, norm_gains=norm_gains, hg_norm=hg_norm, w_in=w_in, w_out=w_out,
             w_cq=w_cq, w_ckv=w_ckv, w_co=w_co, w_gate_up=w_gate_up, w_down=w_down),
        dict(rel_bias=m_rel_bias, lb_logits=m_lb_logits, norm_gains=m_norm_gains, hg_norm=m_hg_norm, w_in=m_w_in,
             w_out=m_w_out, w_cq=m_w_cq, w_ckv=m_w_ckv, w_co=m_w_co, w_gate_up=m_w_gate_up, w_down=m_w_down),
        dict(rel_bias=v_rel_bias, lb_logits=v_lb_logits, norm_gains=v_norm_gains, hg_norm=v_hg_norm, w_in=v_w_in,
             w_out=v_w_out, w_cq=v_w_cq, w_ckv=v_w_ckv, w_co=v_w_co, w_gate_up=v_w_gate_up, w_down=v_w_down))


BIG = ("w_in", "w_out", "w_cq", "w_ckv", "w_co", "w_gate_up", "w_down")
SHARD_AXIS = dict(w_in=1, w_out=0, w_cq=0, w_ckv=1, w_co=0, w_gate_up=1, w_down=0)
NB = len(BIG)
AG_WITH_HGRN = ("w_in", "w_out", "w_cq", "w_ckv", "w_co")
AG_WITH_FFN = ("w_gate_up", "w_down")
OUT_ORDER = ("rel_bias", "lb_logits", "norm_gains", "w_in", "hg_norm", "w_out", "w_cq", "w_ckv", "w_co",
             "w_gate_up", "w_down")
ANY = pl.BlockSpec(memory_space=pl.ANY)


def _place():
    x, y, c = lax.axis_index("x"), lax.axis_index("y"), lax.axis_index("c")
    chips = [(1 - x, y), (x, 1 - y), (1 - x, 1 - y)]
    return x, y, c, chips


def _remote(src, dst, send_sem, recv_sem, to):
    return pltpu.make_async_remote_copy(src_ref=src, dst_ref=dst, send_sem=send_sem, recv_sem=recv_sem,
                                        device_id=to, device_id_type=MESH_ID)


def _half_region(ref, axis, chip, half, lead=()):
    R, C = ref.shape[-2:]
    if axis == 0:
        rs = R // N_CHIPS
        return ref.at[(*lead, pl.ds(chip * rs + half * (rs // 2), rs // 2), slice(None))]
    cs = C // N_CHIPS
    return ref.at[(*lead, pl.ds(half * (R // 2), R // 2), pl.ds(chip * cs, cs))]


def _cast_place(n, w, l, where, name):
    _, rs, cs = w.shape
    tr = _blk(rs, (256, 128, 64, 32, 16))
    nt = rs // tr
    if SHARD_AXIS[n] == 0:
        full = (rs * N_CHIPS, cs)
        o_spec = pl.BlockSpec((tr, cs), lambda i, wh: (wh[0] * nt + i, 0))
    else:
        full = (rs, cs * N_CHIPS)
        o_spec = pl.BlockSpec((tr, cs), lambda i, wh: (i, wh[0]))

    def body(wh, w_ref, o_ref):
        o_ref[...] = w_ref[...].astype(BF16)

    return pl.pallas_call(
        body, name=name,
        grid_spec=pltpu.PrefetchScalarGridSpec(
            num_scalar_prefetch=1, grid=(nt,),
            in_specs=[pl.BlockSpec((None, tr, cs), lambda i, wh: (l, i, 0))], out_specs=o_spec),
        out_shape=jax.ShapeDtypeStruct(full, BF16), compiler_params=_cparams(("parallel",)),
    )(where, w)


def _allgather_weights(placed, name):
    names = list(BIG)
    nb = NB

    def body(*refs):
        outs = refs[NB:2 * NB]
        send, recv = refs[2 * NB:]
        x, y, c, chips = _place()
        me = 2 * x + y
        sib = (x, y, 1 - c)
        sends = []
        for j, chip in enumerate(chips):
            for wi, n in enumerate(names):
                reg = _half_region(outs[wi], SHARD_AXIS[n], me, c)
                cp = _remote(reg, reg, send.at[j * nb + wi], recv.at[j * nb + wi], (*chip, c))
                cp.start()
                sends.append(cp)
        for j, chip in enumerate(chips):
            them = 2 * chip[0] + chip[1]
            for wi, n in enumerate(names):
                reg = _half_region(outs[wi], SHARD_AXIS[n], them, c)
                _remote(reg, reg, send.at[j * nb + wi], recv.at[j * nb + wi], (*chip, c)).wait_recv()
                cp = _remote(reg, reg, send.at[(3 + j) * nb + wi], recv.at[(3 + j) * nb + wi], sib)
                cp.start()
                sends.append(cp)
        for j, chip in enumerate(chips):
            them = 2 * chip[0] + chip[1]
            for wi, n in enumerate(names):
                reg = _half_region(outs[wi], SHARD_AXIS[n], them, 1 - c)
                _remote(reg, reg, send.at[(3 + j) * nb + wi], recv.at[(3 + j) * nb + wi], sib).wait_recv()
        for cp in sends:
            cp.wait_send()

    outs = pl.pallas_call(
        body, name=name, in_specs=[ANY] * NB, out_specs=[ANY] * NB,
        out_shape=[jax.ShapeDtypeStruct(placed[n].shape, BF16) for n in names],
        scratch_shapes=[pltpu.SemaphoreType.DMA((6 * NB,)), pltpu.SemaphoreType.DMA((6 * NB,))],
        input_output_aliases={wi: wi for wi in range(NB)},
    )(*[placed[n] for n in names])
    return dict(zip(names, outs))


def _comm_spec(ins, out_shapes, aliases, nsem, start, wait):
    return dict(ins=list(ins), out_shapes=list(out_shapes), aliases=dict(aliases), nsem=nsem, start=start, wait=wait)


def _run_comm(spec, name):
    ni, no = len(spec["ins"]), len(spec["out_shapes"])

    def body(*refs):
        ins, outs = refs[:ni], refs[ni:ni + no]
        send, recv = refs[ni + no:]
        spec["start"](ins, outs, send, recv)
        spec["wait"](ins, outs, send, recv)

    return pl.pallas_call(
        body, name=name, in_specs=[ANY] * ni, out_specs=[ANY] * no, out_shape=spec["out_shapes"],
        scratch_shapes=[pltpu.SemaphoreType.DMA((spec["nsem"],)), pltpu.SemaphoreType.DMA((spec["nsem"],))],
        input_output_aliases=spec["aliases"],
    )(*spec["ins"])


def _ag_ici_spec(placed):
    names = list(placed)
    nb = len(names)

    def start(ins, outs, send, recv):
        x, y, c, chips = _place()
        me = 2 * x + y
        for j, chip in enumerate(chips):
            for wi, n in enumerate(names):
                reg = _half_region(outs[wi], SHARD_AXIS[n], me, c)
                _remote(reg, reg, send.at[j * nb + wi], recv.at[j * nb + wi], (*chip, c)).start()

    def wait(ins, outs, send, recv):
        x, y, c, chips = _place()
        me = 2 * x + y
        for j, chip in enumerate(chips):
            them = 2 * chip[0] + chip[1]
            for wi, n in enumerate(names):
                reg = _half_region(outs[wi], SHARD_AXIS[n], them, c)
                _remote(reg, reg, send.at[j * nb + wi], recv.at[j * nb + wi], (*chip, c)).wait_recv()
        for j, chip in enumerate(chips):
            for wi, n in enumerate(names):
                reg = _half_region(outs[wi], SHARD_AXIS[n], me, c)
                _remote(reg, reg, send.at[j * nb + wi], recv.at[j * nb + wi], (*chip, c)).wait_send()

    return _comm_spec([placed[n] for n in names], [jax.ShapeDtypeStruct(placed[n].shape, BF16) for n in names],
                      {wi: wi for wi in range(nb)}, 3 * nb, start, wait)


def _ag_sibling_spec(arrays):
    names = list(arrays)
    nb = len(names)

    def start(ins, outs, send, recv):
        x, y, c, chips = _place()
        for j, chip in enumerate(chips):
            them = 2 * chip[0] + chip[1]
            for wi, n in enumerate(names):
                reg = _half_region(outs[wi], SHARD_AXIS[n], them, c)
                _remote(reg, reg, send.at[j * nb + wi], recv.at[j * nb + wi], (x, y, 1 - c)).start()

    def wait(ins, outs, send, recv):
        x, y, c, chips = _place()
        for j, chip in enumerate(chips):
            them = 2 * chip[0] + chip[1]
            for wi, n in enumerate(names):
                reg = _half_region(outs[wi], SHARD_AXIS[n], them, 1 - c)
                _remote(reg, reg, send.at[j * nb + wi], recv.at[j * nb + wi], (x, y, 1 - c)).wait_recv()
        for j, chip in enumerate(chips):
            them = 2 * chip[0] + chip[1]
            for wi, n in enumerate(names):
                reg = _half_region(outs[wi], SHARD_AXIS[n], them, c)
                _remote(reg, reg, send.at[j * nb + wi], recv.at[j * nb + wi], (x, y, 1 - c)).wait_send()

    return _comm_spec([arrays[n] for n in names], [jax.ShapeDtypeStruct(arrays[n].shape, BF16) for n in names],
                      {wi: wi for wi in range(nb)}, 3 * nb, start, wait)


def _allgather_gains(g_shard, name):
    A, Cs = g_shard.shape

    def body(in_ref, out_ref, send, recv, local):
        x, y, c, chips = _place()
        me = 2 * x + y
        mine = pltpu.make_async_copy(in_ref, out_ref.at[me], local)
        mine.start()
        cps = [_remote(in_ref, out_ref.at[me], send.at[j], recv.at[j], (*chip, c)) for j, chip in enumerate(chips)]
        for cp in cps:
            cp.start()
        for j, chip in enumerate(chips):
            them = 2 * chip[0] + chip[1]
            _remote(in_ref, out_ref.at[them], send.at[j], recv.at[j], (*chip, c)).wait_recv()
        for cp in cps:
            cp.wait_send()
        mine.wait()

    return pl.pallas_call(
        body, name=name, in_specs=[ANY], out_specs=ANY,
        out_shape=jax.ShapeDtypeStruct((N_CHIPS, A, Cs), F32),
        scratch_shapes=[pltpu.SemaphoreType.DMA((3,)), pltpu.SemaphoreType.DMA((3,)), pltpu.SemaphoreType.DMA],
    )(g_shard)


def _half_shape(n, shape):
    R, C = shape
    if SHARD_AXIS[n] == 0:
        return (N_CHIPS, R // N_CHIPS // 2, C)
    return (R // 2, C)


def _as_halves(n, g):
    R, C = g.shape
    if SHARD_AXIS[n] == 0:
        return g.reshape(N_CHIPS, R // N_CHIPS, C)
    return g


def _my_half(n, ref, half):
    if SHARD_AXIS[n] == 0:
        hs = ref.shape[1] // 2
        return ref.at[:, pl.ds(half * hs, hs), :]
    hs = ref.shape[0] // 2
    return ref.at[pl.ds(half * hs, hs), :]


def _swap_sibling_halves(gw, name):
    names = list(gw)
    nb = len(names)

    def body(*refs):
        ins = refs[:nb]
        outs = refs[nb:2 * nb]
        send, recv = refs[2 * nb:]
        x, y, c, _ = _place()
        sib = (x, y, 1 - c)
        cps = []
        for wi, n in enumerate(names):
            cp = _remote(_my_half(n, ins[wi], 1 - c), outs[wi], send.at[wi], recv.at[wi], sib)
            cp.start()
            cps.append(cp)
        for cp in cps:
            cp.wait()

    shapes = []
    for n in names:
        g = gw[n]
        shapes.append((g.shape[0], g.shape[1] // 2, g.shape[2]) if g.ndim == 3 else (g.shape[0] // 2, g.shape[1]))
    outs = pl.pallas_call(
        body, name=name, in_specs=[ANY] * nb, out_specs=[ANY] * nb,
        out_shape=[jax.ShapeDtypeStruct(s, BF16) for s in shapes],
        scratch_shapes=[pltpu.SemaphoreType.DMA((nb,)), pltpu.SemaphoreType.DMA((nb,))],
    )(*[gw[n] for n in names])
    return dict(zip(names, outs))


def _pair_sum(n, g, other, c_arr, name):
    if g.ndim == 3:
        nc, rs, C = g.shape
        hs = rs // 2
        tr = _blk(hs, (256, 128, 64, 32, 16))
        nt = hs // tr
        grid = (nc, nt)
        g_spec = pl.BlockSpec((1, tr, C), lambda k, i, c_ref: (k, c_ref[0] * nt + i, 0))
        o_spec = pl.BlockSpec((1, tr, C), lambda k, i, c_ref: (k, i, 0))
    else:
        R, C = g.shape
        hs = R // 2
        cs = C // N_CHIPS
        tr = _blk(hs, (256, 128, 64, 32, 16))
        nt = hs // tr
        grid = (nt, N_CHIPS)
        g_spec = pl.BlockSpec((tr, cs), lambda i, k, c_ref: (c_ref[0] * nt + i, k))
        o_spec = pl.BlockSpec((tr, cs), lambda i, k, c_ref: (i, k))

    def body(c_ref, g_ref, o_ref, out_ref):
        out_ref[...] = (g_ref[...].astype(F32) + o_ref[...].astype(F32)).astype(BF16)

    return pl.pallas_call(
        body, name=name,
        grid_spec=pltpu.PrefetchScalarGridSpec(num_scalar_prefetch=1, grid=grid, in_specs=[g_spec, o_spec],
                                               out_specs=o_spec),
        out_shape=jax.ShapeDtypeStruct(other.shape, BF16),
        compiler_params=_cparams(("parallel", "parallel")),
    )(c_arr, g, other)


def _scatter_spec(pairs):
    names = list(pairs)
    nb = len(names)

    def piece(n, ref, chip):
        if SHARD_AXIS[n] == 0:
            return ref.at[chip]
        cs = ref.shape[1] // N_CHIPS
        return ref.at[:, pl.ds(chip * cs, cs)]

    def start(ins, outs, send, recv):
        x, y, c, chips = _place()
        for j, chip in enumerate(chips):
            them = 2 * chip[0] + chip[1]
            for wi, n in enumerate(names):
                _remote(piece(n, ins[wi], them), outs[wi].at[j], send.at[j * nb + wi], recv.at[j * nb + wi],
                        (*chip, c)).start()

    def wait(ins, outs, send, recv):
        x, y, c, chips = _place()
        me = 2 * x + y
        for j, chip in enumerate(chips):
            for wi, n in enumerate(names):
                _remote(piece(n, ins[wi], me), outs[wi].at[j], send.at[j * nb + wi], recv.at[j * nb + wi],
                        (*chip, c)).wait_recv()
        for j, chip in enumerate(chips):
            them = 2 * chip[0] + chip[1]
            for wi, n in enumerate(names):
                _remote(piece(n, ins[wi], them), outs[wi].at[j], send.at[j * nb + wi], recv.at[j * nb + wi],
                        (*chip, c)).wait_send()

    shapes = []
    for n in names:
        p = pairs[n]
        shapes.append((3,) + (p.shape[1:] if p.ndim == 3 else (p.shape[0], p.shape[1] // N_CHIPS)))
    return _comm_spec([pairs[n] for n in names], [jax.ShapeDtypeStruct(s, BF16) for s in shapes], {}, 3 * nb,
                      start, wait)


def _sum_final(n, pair, q, stack, l, where, name):
    hs, cc = q.shape[1:]
    tr = _blk(hs, (256, 128, 64, 32, 16))
    nt = hs // tr
    if SHARD_AXIS[n] == 0:
        p_spec = pl.BlockSpec((None, tr, cc), lambda i, wh: (wh[0], i, 0))
    else:
        p_spec = pl.BlockSpec((tr, cc), lambda i, wh: (i, wh[0]))

    def body(wh, p_ref, q_ref, stack_any, o_ref):
        o_ref[...] = ((p_ref[...].astype(F32) + q_ref[0].astype(F32)) + q_ref[1].astype(F32)) + q_ref[2].astype(F32)

    return pl.pallas_call(
        body, name=name,
        grid_spec=pltpu.PrefetchScalarGridSpec(
            num_scalar_prefetch=1, grid=(nt,),
            in_specs=[p_spec, pl.BlockSpec((3, tr, cc), lambda i, wh: (0, i, 0)), ANY],
            out_specs=pl.BlockSpec((None, tr, cc), lambda i, wh: (l, wh[1] * nt + i, 0))),
        out_shape=jax.ShapeDtypeStruct(stack.shape, F32),
        input_output_aliases={3: 0},
        compiler_params=_cparams(("parallel",)),
    )(where, pair, q, stack)


def _sum_slots(q, name):
    K, A, B = q.shape
    tr = _blk(A, (256, 128, 64, 32, 16, 8))

    def body(q_ref, o_ref):
        s = q_ref[0].astype(F32)
        for k in range(1, K):
            s = s + q_ref[k].astype(F32)
        o_ref[...] = s

    return pl.pallas_call(
        body, name=name, grid=(A // tr,),
        in_specs=[pl.BlockSpec((K, tr, B), lambda i: (0, i, 0))], out_specs=pl.BlockSpec((tr, B), lambda i: (i, 0)),
        out_shape=jax.ShapeDtypeStruct((A, B), F32), compiler_params=_cparams(("parallel",)),
    )(q)


def _finish_grads(stacks, small, name):
    names = list(BIG)
    A = small.shape[0]
    nh = NB

    def body(*refs):
        small_ref = refs[nh]
        outs = refs[nh + 1:nh + 1 + NB]
        slots = refs[nh + 1 + NB]
        send, recv, local = refs[nh + 2 + NB:]
        x, y, c, _ = _place()
        sib = (x, y, 1 - c)
        me = 4 * x + 2 * y + c
        locals_, sends = [], []
        for wi, n in enumerate(names):
            hr = outs[wi].shape[1] // 2
            reg = outs[wi].at[:, pl.ds(c * hr, hr), :]
            cp = _remote(reg, reg, send.at[wi], recv.at[wi], sib)
            cp.start()
            sends.append(cp)
        cp = pltpu.make_async_copy(small_ref, slots.at[me], local)
        cp.start()
        locals_.append(cp)
        peers = []
        for dx in range(2):
            for dy in range(2):
                for dc in range(2):
                    if dx or dy or dc:
                        peers.append((dx, dy, dc))
        for j, (dx, dy, dc) in enumerate(peers):
            to = (jnp.bitwise_xor(x, dx), jnp.bitwise_xor(y, dy), jnp.bitwise_xor(c, dc))
            cp = _remote(small_ref, slots.at[me], send.at[nh + j], recv.at[nh + j], to)
            cp.start()
            sends.append(cp)
        for wi, n in enumerate(names):
            hr = outs[wi].shape[1] // 2
            reg = outs[wi].at[:, pl.ds((1 - c) * hr, hr), :]
            _remote(reg, reg, send.at[wi], recv.at[wi], sib).wait_recv()
        for j, (dx, dy, dc) in enumerate(peers):
            frm = 4 * jnp.bitwise_xor(x, dx) + 2 * jnp.bitwise_xor(y, dy) + jnp.bitwise_xor(c, dc)
            _remote(small_ref, slots.at[frm], send.at[nh + j], recv.at[nh + j], sib).wait_recv()
        for cp in sends:
            cp.wait_send()
        for cp in locals_:
            cp.wait()

    res = pl.pallas_call(
        body, name=name, in_specs=[ANY] * (nh + 1), out_specs=[ANY] * (NB + 1),
        out_shape=[jax.ShapeDtypeStruct(stacks[n].shape, F32) for n in names]
        + [jax.ShapeDtypeStruct((8, A, LANES), F32)],
        scratch_shapes=[pltpu.SemaphoreType.DMA((nh + 7,)), pltpu.SemaphoreType.DMA((nh + 7,)),
                        pltpu.SemaphoreType.DMA],
        input_output_aliases={wi: wi for wi in range(NB)},
    )(*[stacks[n] for n in names], small)
    return dict(zip(names, res[:NB])), res[NB]


def _adamw(w, g, m, v, name):
    shape = w.shape
    if w.ndim == 2:
        w, g, m, v = (t.reshape((1,) + shape) for t in (w, g, m, v))
    L, A, B = w.shape
    tr = _blk(A, (128, 64, 32, 16, 8))
    c1 = 1.0 - ADAM_B1 ** ADAM_STEP
    c2 = 1.0 - ADAM_B2 ** ADAM_STEP

    def body(w_ref, g_ref, m_ref, v_ref, d_ref, nm_ref, nv_ref):
        gv = g_ref[...]
        nm = ADAM_B1 * m_ref[...] + (1.0 - ADAM_B1) * gv
        nv = ADAM_B2 * v_ref[...] + (1.0 - ADAM_B2) * (gv * gv)
        m_hat = nm / c1
        v_hat = nv / c2
        d_ref[...] = -ADAM_LR * (m_hat / (jnp.sqrt(v_hat) + ADAM_EPS) + ADAM_WD * w_ref[...])
        nm_ref[...] = nm
        nv_ref[...] = nv

    spec = pl.BlockSpec((1, tr, B), lambda l, i: (l, i, 0))
    sds = jax.ShapeDtypeStruct((L, A, B), F32)
    outs = pl.pallas_call(
        body, name=name, grid=(L, A // tr), in_specs=[spec] * 4, out_specs=[spec] * 3, out_shape=[sds] * 3,
        compiler_params=_cparams(("parallel", "parallel")),
    )(w, g, m, v)
    return tuple(o.reshape(shape) for o in outs)


SMALL_ROWS = 520


def _train_step(x, mem, target, w, m, v):
    L = w["lb_logits"].shape[0]
    cx, cy, cc = lax.axis_index("x"), lax.axis_index("y"), lax.axis_index("c")
    me = 2 * cx + cy
    c_arr = jnp.reshape(cc, (1,)).astype(jnp.int32)
    where = jnp.stack([me, cc]).astype(jnp.int32)

    placed = [{n: _cast_place(n, w[n], l, where, f"cast_{n}_l{l}") for n in BIG} for l in range(L)]
    full = [None] * L
    full[0] = _allgather_weights(placed[0], "allgather_l0")
    gs = _allgather_gains(w["norm_gains"].reshape(L * N_NORMS, -1), "allgather_gains")
    gains = jnp.transpose(gs, (1, 0, 2)).reshape(L, N_NORMS, D_MODEL)

    def fwd_comm(l):
        if l + 1 == L:
            return None
        return dict(hgrn=_ag_ici_spec({n: placed[l + 1][n] for n in AG_WITH_HGRN}),
                    gate_up=_ag_ici_spec({n: placed[l + 1][n] for n in AG_WITH_FFN}))

    def fwd_done(l, outs):
        arrived = dict(zip(AG_WITH_HGRN, outs["hgrn"]))
        arrived.update(zip(AG_WITH_FFN, outs["gate_up"]))
        arrived = {n: arrived[n] for n in BIG}
        full[l + 1] = dict(zip(BIG, _run_comm(_ag_sibling_spec(arrived), f"allgather_sibling_l{l + 1}")))

    stacks = {n: lax.empty(w[n].shape, F32) for n in BIG}
    late = {}
    in_flight = {}

    def pair_sums(l, gw, tag):
        views = {n: _as_halves(n, g) for n, g in gw.items()}
        theirs = _swap_sibling_halves(views, f"swap_halves_{tag}_l{l}")
        return {n: _pair_sum(n, views[n], theirs[n], c_arr, f"pair_sum_{n}_l{l}") for n in gw}

    def finish(l, n, pair, slot):
        stacks[n] = _sum_final(n, pair, slot, stacks[n], l, where, f"sum_chips_{n}_l{l}")

    def bwd_comm(l, gw_ready):
        pairs = pair_sums(l, gw_ready, "early")
        owners = [(l, n) for n in pairs]
        if l + 1 in late:
            pairs["w_in"] = late.pop(l + 1)
            owners.append((l + 1, "w_in"))
        in_flight[l] = (owners, pairs)
        return _scatter_spec(pairs)

    def bwd_done(l, outs):
        owners, pairs = in_flight.pop(l)
        for (lo, n), slot in zip(owners, outs):
            finish(lo, n, pairs[n], slot)

    def layer_grads(l, gw):
        pair = pair_sums(l, {"w_in": gw["w_in"]}, "late")["w_in"]
        if l == 0:
            (slot,) = _run_comm(_scatter_spec({"w_in": pair}), "scatter_l0")
            finish(0, "w_in", pair, slot)
        else:
            late[l] = pair
        return None

    loss, dx, _, d_rb, d_lb, d_gains, d_hg = _local_step(
        x[0], mem[0], target[0], w["rel_bias"], w["lb_logits"], gains, w["hg_norm"], lambda l: full[l],
        dict(fwd_comm=fwd_comm, fwd_done=fwd_done, bwd_comm=bwd_comm, bwd_done=bwd_done, layer_grads=layer_grads))

    flat = jnp.concatenate([d_rb.reshape(-1), d_lb.reshape(-1), d_hg.reshape(-1), d_gains.reshape(-1)])
    small = jnp.pad(flat, (0, SMALL_ROWS * LANES - flat.shape[0])).reshape(SMALL_ROWS, LANES)
    grads, slots = _finish_grads(stacks, small, "finish_grads")
    tot = _sum_slots(slots, "sum_small").reshape(-1)
    n_rb, n_lb = d_rb.size, d_lb.size
    grads["rel_bias"] = tot[:n_rb].reshape(d_rb.shape)
    grads["lb_logits"] = tot[n_rb:n_rb + n_lb].reshape(d_lb.shape)
    grads["hg_norm"] = tot[n_rb + n_lb:n_rb + 2 * n_lb].reshape(d_hg.shape)
    g_full = tot[n_rb + 2 * n_lb:n_rb + 2 * n_lb + d_gains.size].reshape(d_gains.shape)
    cs = D_MODEL // N_CHIPS
    grads["norm_gains"] = lax.dynamic_slice_in_dim(g_full, me * cs, cs, axis=2)

    delta, new_m, new_v = {}, {}, {}
    for n in OUT_ORDER:
        delta[n], new_m[n], new_v[n] = _adamw(w[n], grads[n], m[n], v[n], f"adamw_{n}")
    loss = lax.psum(loss, ("x", "y", "c"))
    return (loss, dx[None], *[grads[n] for n in OUT_ORDER], *[delta[n] for n in OUT_ORDER],
            *[new_m[n] for n in OUT_ORDER], *[new_v[n] for n in OUT_ORDER])
```
